```python
import math, functools
import jax, jax.numpy as jnp
from jax import lax
import numpy as np

D_MODEL = 1024
BATCH = 2
SEQ = 8192
DEPTH = 1
DEC_BATCH = 128
DEC_SEQ = 1
PAST_LEN = 2048
PAGE_SIZE = 128

N_HEADS = 8
HEAD_DIM = 64
ATT_WIDTH = N_HEADS * HEAD_DIM
IDX_HEADS = 4
IDX_DIM = 64
TOPK_MAX = 256
Q_BLOCK = 128
N_BUCKETS = 32
MAX_DISTANCE = 128
SSM_WIDTH = 512
GROUP = 16
N_GROUPS = SSM_WIDTH // GROUP
STATE_DIM = 64
DT_MIN = 0.001
DT_MAX = 0.1
D_FF = 4 * D_MODEL
CONV_W = 3
EPS = 1e-6
SPLITS = (ATT_WIDTH, ATT_WIDTH, ATT_WIDTH, IDX_HEADS * IDX_DIM, IDX_DIM, IDX_HEADS, SSM_WIDTH, D_MODEL, D_MODEL)
IN_WIDTH = sum(SPLITS)

kernel_name = "dsa_s5_gated_hybrid_step"


def rmsnorm(x, g):
    xf = x.astype(jnp.float32)
    inv = lax.rsqrt(jnp.mean(xf * xf, axis=-1, keepdims=True) + EPS)
    return (xf * inv).astype(x.dtype) * g


def t5_bucket(rel):
    n = jnp.maximum(-rel, 0)
    max_exact = N_BUCKETS // 2
    nf = jnp.maximum(n, 1).astype(jnp.float32)
    large = max_exact + (jnp.log(nf / max_exact) / math.log(MAX_DISTANCE / max_exact)
                         * (N_BUCKETS - max_exact)).astype(jnp.int32)
    large = jnp.minimum(large, N_BUCKETS - 1)
    return jnp.where(n < max_exact, n, large)


def indexer_scores(q_idx, w_idx, k_idx):
    s = jnp.einsum('bthd,bsd->bths', q_idx, k_idx).astype(jnp.float32) * IDX_DIM ** -0.5
    return jnp.einsum('bths,bth->bts', jax.nn.relu(s), w_idx.astype(jnp.float32))


def sparse_attend(q, k_sel, v_sel, sel, q_pos, valid, rel_bias):
    logits = jnp.einsum('bthd,btkhd->bthk', q, k_sel).astype(jnp.float32) * HEAD_DIM ** -0.5
    bias = rel_bias[t5_bucket(sel - q_pos[None, :, None])]
    logits = logits + jnp.moveaxis(bias, -1, -2).astype(jnp.float32)
    logits = jnp.where(valid[:, :, None, :], logits, -jnp.inf)
    p = jax.nn.softmax(logits, axis=-1).astype(v_sel.dtype)
    return jnp.einsum('bthk,btkhd->bthd', p, v_sel)


def prompt_attention(q, k, v, q_idx, w_idx, k_idx, rel_bias):
    bsz, t = q.shape[:2]
    topk = min(TOPK_MAX, t // 4)
    key_pos = jnp.arange(t)
    bidx = jnp.arange(bsz)[:, None, None]

    def block(i):
        s0 = i * Q_BLOCK
        qb = lax.dynamic_slice_in_dim(q, s0, Q_BLOCK, axis=1)
        qib = lax.dynamic_slice_in_dim(q_idx, s0, Q_BLOCK, axis=1)
        wib = lax.dynamic_slice_in_dim(w_idx, s0, Q_BLOCK, axis=1)
        q_pos = s0 + jnp.arange(Q_BLOCK)
        score = indexer_scores(qib, wib, k_idx)
        causal = key_pos[None, :] <= q_pos[:, None]
        score = jnp.where(causal[None], score, -jnp.inf)
        _, sel = lax.top_k(score, topk)
        valid = sel <= q_pos[None, :, None]
        return sparse_attend(qb, k[bidx, sel], v[bidx, sel], sel, q_pos, valid, rel_bias)

    out = lax.map(block, jnp.arange(t // Q_BLOCK))
    return jnp.moveaxis(out, 0, 1).reshape(bsz, t, ATT_WIDTH)


def sample_attention(q, k, v, q_idx, w_idx, k_idx, cache_k, cache_v, cache_kidx, page_table, rel_bias):
    db, tq = q.shape[:2]
    past = page_table.shape[1] * PAGE_SIZE
    n_keys = past + tq
    topk = min(TOPK_MAX, n_keys // 4)
    kidx_past = cache_kidx[page_table].reshape(db, past, IDX_DIM)
    kidx_all = jnp.concatenate([kidx_past, k_idx.astype(kidx_past.dtype)], axis=1)
    q_pos = past + jnp.arange(tq)
    score = indexer_scores(q_idx, w_idx, kidx_all)
    causal = jnp.arange(n_keys)[None, :] <= q_pos[:, None]
    score = jnp.where(causal[None], score, -jnp.inf)
    _, sel = lax.top_k(score, topk)
    valid = sel <= q_pos[None, :, None]
    bidx = jnp.arange(db)[:, None, None]
    is_past = (sel < past)[..., None, None]
    sp = jnp.minimum(sel, past - 1)
    phys = page_table[bidx, sp // PAGE_SIZE]
    off = sp % PAGE_SIZE
    sn = jnp.clip(sel - past, 0, tq - 1)
    k_sel = jnp.where(is_past, cache_k[phys, off], k[bidx, sn].astype(cache_k.dtype))
    v_sel = jnp.where(is_past, cache_v[phys, off], v[bidx, sn].astype(cache_v.dtype))
    return sparse_attend(q, k_sel, v_sel, sel, q_pos, valid, rel_bias).reshape(db, tq, ATT_WIDTH)


def ssm_branch(u, x0_re, x0_im, lam_re, lam_im, log_dt, b_re, b_im, c_re, c_im, d_skip, w_glu, b_glu):
    f32 = jnp.float32
    bsz, t, _ = u.shape
    lam = lax.complex(lam_re.astype(f32), lam_im.astype(f32))
    dt = jnp.exp(log_dt.astype(f32))[:, None]
    lam_bar = jnp.exp(lam * dt)
    b = lax.complex(b_re.astype(f32), b_im.astype(f32))
    b_bar = ((lam_bar - 1.0) / lam)[..., None] * b
    uf = u.astype(f32)
    ug = uf.reshape(bsz, t, N_GROUPS, GROUP).astype(jnp.complex64)
    bu = jnp.einsum('gpc,btgc->btgp', b_bar, ug)
    x0 = lax.complex(x0_re.astype(f32), x0_im.astype(f32))
    bu = bu.at[:, 0].add(lam_bar * x0)
    a = jnp.broadcast_to(lam_bar, bu.shape)

    def combine(e1, e2):
        a1, b1 = e1
        a2, b2 = e2
        return a2 * a1, a2 * b1 + b2

    _, xs = lax.associative_scan(combine, (a, bu), axis=1)
    c = lax.complex(c_re.astype(f32), c_im.astype(f32))
    y = jnp.einsum('gcp,btgp->btgc', c, xs).real.reshape(bsz, t, SSM_WIDTH) + d_skip.astype(f32) * uf
    g = jax.nn.gelu(y)
    y = g * jax.nn.sigmoid(g @ w_glu.astype(f32) + b_glu.astype(f32))
    x_last = xs[:, -1]
    return y.astype(u.dtype), jnp.real(x_last), jnp.imag(x_last)


def conv_ffn(h, conv_prev, w_up, conv_w, conv_b, w_down):
    up = h @ w_up
    t = up.shape[1]
    ext = jnp.concatenate([conv_prev.astype(up.dtype), up], axis=1)
    acc = conv_b + ext[:, 0:t] * conv_w[0]
    for j in range(1, CONV_W):
        acc = acc + ext[:, j:j + t] * conv_w[j]
    a, b = jnp.split(acc, 2, axis=-1)
    out = (jax.nn.gelu(a) * b) @ w_down
    return out, ext[:, -(CONV_W - 1):]


def trunk_layer(x, attend, x0_re, x0_im, conv_prev, lw):
    (w_in, g_pre_mix, g_post_mix, lam_re, lam_im, log_dt, b_re, b_im, c_re, c_im, d_skip,
     w_glu, b_glu, w_att_out, w_ssm_out, w_o, g_pre_ffn, g_post_ffn, w_up, conv_w, conv_b, w_down) = lw
    bsz, t, _ = x.shape
    h = rmsnorm(x, g_pre_mix)
    points = np.cumsum(SPLITS)[:-1].tolist()
    q, k, v, q_idx, k_idx, w_idx, u, g_att, g_ssm = jnp.split(h @ w_in, points, axis=-1)
    q = q.reshape(bsz, t, N_HEADS, HEAD_DIM)
    k = k.reshape(bsz, t, N_HEADS, HEAD_DIM)
    v = v.reshape(bsz, t, N_HEADS, HEAD_DIM)
    q_idx = q_idx.reshape(bsz, t, IDX_HEADS, IDX_DIM)
    w_idx = w_idx * IDX_HEADS ** -0.5
    y_att = attend(q, k, v, q_idx, w_idx, k_idx)
    y_ssm, s_re, s_im = ssm_branch(u, x0_re, x0_im, lam_re, lam_im, log_dt, b_re, b_im,
                                   c_re, c_im, d_skip, w_glu, b_glu)
    merged = jax.nn.sigmoid(g_att) * (y_att @ w_att_out) + jax.nn.sigmoid(g_ssm) * (y_ssm @ w_ssm_out)
    x = x + rmsnorm(merged @ w_o, g_post_mix)
    f, conv_new = conv_ffn(rmsnorm(x, g_pre_ffn), conv_prev, w_up, conv_w, conv_b, w_down)
    x = x + rmsnorm(f, g_post_ffn)
    return x, (k, v, k_idx, s_re, s_im, conv_new)


def setup_inputs(seed: int = 0) -> dict:
    key = jax.random.key(seed)
    keys = list(jax.random.split(key, 40))

    def nrm(i, shape, scale):
        return jax.random.normal(keys[i], shape, jnp.float32) * scale

    n_pages = PAST_LEN // PAGE_SIZE
    n_used = DEC_BATCH * n_pages
    n_pool = n_used + n_used // 4
    page_table = jax.random.permutation(keys[0], n_pool)[:n_used].reshape(DEC_BATCH, n_pages).astype(jnp.int32)
    lam_im = (math.pi * jnp.arange(STATE_DIM, dtype=jnp.float32))[None, None, :] + nrm(13, (DEPTH, N_GROUPS, STATE_DIM), 0.01)
    return {
        "x_prompt": nrm(1, (BATCH, SEQ, D_MODEL), 1.0),
        "x_sample": nrm(2, (DEC_BATCH, DEC_SEQ, D_MODEL), 1.0),
        "cache_k": nrm(3, (DEPTH, n_pool, PAGE_SIZE, N_HEADS, HEAD_DIM), 1.0),
        "cache_v": nrm(4, (DEPTH, n_pool, PAGE_SIZE, N_HEADS, HEAD_DIM), 1.0),
        "cache_kidx": nrm(5, (DEPTH, n_pool, PAGE_SIZE, IDX_DIM), 1.0),
        "state_ssm_re": nrm(6, (DEPTH, DEC_BATCH, N_GROUPS, STATE_DIM), 1.0),
        "state_ssm_im": nrm(7, (DEPTH, DEC_BATCH, N_GROUPS, STATE_DIM), 1.0),
        "state_conv": nrm(8, (DEPTH, DEC_BATCH, CONV_W - 1, 2 * D_FF), 1.0),
        "page_table": page_table,
        "rel_bias": nrm(9, (N_BUCKETS, N_HEADS), 0.5),
        "w_in": nrm(10, (DEPTH, D_MODEL, IN_WIDTH), D_MODEL ** -0.5),
        "g_pre_mix": 1.0 + nrm(11, (DEPTH, D_MODEL), 0.1),
        "g_post_mix": 1.0 + nrm(12, (DEPTH, D_MODEL), 0.1),
        "lam_re": -0.5 + nrm(14, (DEPTH, N_GROUPS, STATE_DIM), 0.01),
        "lam_im": lam_im,
        "log_dt": jax.random.uniform(keys[15], (DEPTH, N_GROUPS), jnp.float32, math.log(DT_MIN), math.log(DT_MAX)),
        "b_re": nrm(16, (DEPTH, N_GROUPS, STATE_DIM, GROUP), (2 * GROUP) ** -0.5),
        "b_im": nrm(17, (DEPTH, N_GROUPS, STATE_DIM, GROUP), (2 * GROUP) ** -0.5),
        "c_re": nrm(18, (DEPTH, N_GROUPS, GROUP, STATE_DIM), (2 * STATE_DIM) ** -0.5),
        "c_im": nrm(19, (DEPTH, N_GROUPS, GROUP, STATE_DIM), (2 * STATE_DIM) ** -0.5),
        "d_skip": nrm(20, (DEPTH, SSM_WIDTH), 1.0),
        "w_glu": nrm(21, (DEPTH, SSM_WIDTH, SSM_WIDTH), SSM_WIDTH ** -0.5),
        "b_glu": nrm(22, (DEPTH, SSM_WIDTH), 0.01),
        "w_att_out": nrm(23, (DEPTH, ATT_WIDTH, D_MODEL), ATT_WIDTH ** -0.5),
        "w_ssm_out": nrm(24, (DEPTH, SSM_WIDTH, D_MODEL), SSM_WIDTH ** -0.5),
        "w_o": nrm(25, (DEPTH, D_MODEL, D_MODEL), D_MODEL ** -0.5),
        "g_pre_ffn": 1.0 + nrm(26, (DEPTH, D_MODEL), 0.1),
        "g_post_ffn": 1.0 + nrm(27, (DEPTH, D_MODEL), 0.1),
        "w_up": nrm(28, (DEPTH, D_MODEL, 2 * D_FF), D_MODEL ** -0.5),
        "conv_w": nrm(29, (DEPTH, CONV_W, 2 * D_FF), CONV_W ** -0.5),
        "conv_b": nrm(30, (DEPTH, 2 * D_FF), 0.01),
        "w_down": nrm(31, (DEPTH, D_FF, D_MODEL), D_FF ** -0.5),
    }


def reference(x_prompt, x_sample, cache_k, cache_v, cache_kidx, state_ssm_re, state_ssm_im, state_conv,
              page_table, rel_bias, w_in, g_pre_mix, g_post_mix, lam_re, lam_im, log_dt, b_re, b_im,
              c_re, c_im, d_skip, w_glu, b_glu, w_att_out, w_ssm_out, w_o, g_pre_ffn, g_post_ffn,
              w_up, conv_w, conv_b, w_down):
    bp = x_prompt.shape[0]
    y_p, y_s = x_prompt, x_sample
    outs_p, outs_s = [], []
    for l in range(DEPTH):
        lw = (w_in[l], g_pre_mix[l], g_post_mix[l], lam_re[l], lam_im[l], log_dt[l], b_re[l], b_im[l],
              c_re[l], c_im[l], d_skip[l], w_glu[l], b_glu[l], w_att_out[l], w_ssm_out[l], w_o[l],
              g_pre_ffn[l], g_post_ffn[l], w_up[l], conv_w[l], conv_b[l], w_down[l])
        zero_ssm = jnp.zeros((bp, N_GROUPS, STATE_DIM), jnp.float32)
        zero_conv = jnp.zeros((bp, CONV_W - 1, 2 * D_FF), x_prompt.dtype)
        attend_p = functools.partial(prompt_attention, rel_bias=rel_bias)
        y_p, st_p = trunk_layer(y_p, attend_p, zero_ssm, zero_ssm, zero_conv, lw)
        attend_s = functools.partial(sample_attention, cache_k=cache_k[l], cache_v=cache_v[l],
                                     cache_kidx=cache_kidx[l], page_table=page_table, rel_bias=rel_bias)
        y_s, st_s = trunk_layer(y_s, attend_s, state_ssm_re[l], state_ssm_im[l], state_conv[l], lw)
        outs_p.append(st_p)
        outs_s.append(st_s)
    k_p, v_p, ki_p, sr_p, si_p, cv_p = [jnp.stack(a) for a in zip(*outs_p)]
    k_s, v_s, ki_s, sr_s, si_s, cv_s = [jnp.stack(a) for a in zip(*outs_s)]
    return (y_p, y_s, k_p, v_p, ki_p, sr_p, si_p, cv_p, k_s, v_s, ki_s, sr_s, si_s, cv_s)
```

```python
import functools
import math

import numpy as np
import jax
import jax.numpy as jnp
from jax import lax
from jax.experimental import pallas as pl
from jax.experimental.pallas import tpu as pltpu

F32 = jnp.float32
BF16 = jnp.bfloat16
I32 = jnp.int32

D_MODEL = 1024
PAGE_SIZE = 128
N_HEADS = 8
HEAD_DIM = 64
ATT_WIDTH = N_HEADS * HEAD_DIM
IDX_HEADS = 4
IDX_DIM = 64
TOPK_MAX = 256
N_BUCKETS = 32
MAX_DISTANCE = 128
SSM_WIDTH = 512
GROUP = 16
N_GROUPS = SSM_WIDTH // GROUP
STATE_DIM = 64
N_STATE = N_GROUPS * STATE_DIM
D_FF = 4 * D_MODEL
CONV_W = 3
EPS = 1e-6
SPLITS = (ATT_WIDTH, ATT_WIDTH, ATT_WIDTH, IDX_HEADS * IDX_DIM, IDX_DIM, IDX_HEADS, SSM_WIDTH, D_MODEL, D_MODEL)

LANES = 128
KEY_EXCLUDED = -(2 ** 31)
CODE_NEVER = 2 ** 30
IDX_BITS = 14
VMEM_LIMIT = 56 * 1024 * 1024

_C_Q, _C_K, _C_V = 0, ATT_WIDTH, 2 * ATT_WIDTH
_C_QI = 3 * ATT_WIDTH
_C_KW = _C_QI + IDX_HEADS * IDX_DIM
_C_U = _C_KW + LANES
_C_GA = _C_U + SSM_WIDTH
_C_GS = _C_GA + D_MODEL
_C_END = _C_GS + D_MODEL


def _params(*sem):
    return pltpu.CompilerParams(dimension_semantics=sem, vmem_limit_bytes=VMEM_LIMIT)


def _rms(x, g):
    inv = lax.rsqrt(jnp.mean(x * x, axis=-1, keepdims=True) + EPS)
    return (x * inv) * g


def _dot(a, b):
    return jnp.dot(a, b, preferred_element_type=F32)


def _dot_nt(a, b):
    return lax.dot_general(a, b, (((1,), (1,)), ((), ())), preferred_element_type=F32)


def _bucket_starts():
    n = np.arange(0, 1 << IDX_BITS, dtype=np.int32)
    max_exact = N_BUCKETS // 2
    nf = np.maximum(n, 1).astype(np.float32)
    large = max_exact + (np.log(nf / np.float32(max_exact)) / np.float32(math.log(MAX_DISTANCE / max_exact))
                         * np.float32(N_BUCKETS - max_exact)).astype(np.int32)
    large = np.minimum(large, N_BUCKETS - 1)
    bucket = np.where(n < max_exact, n, large)
    assert np.all(np.diff(bucket) >= 0)
    starts = [int(np.argmax(bucket >= b)) for b in range(N_BUCKETS)]
    assert all(bucket[s] == b for b, s in enumerate(starts))
    return starts


_BUCKET_START = _bucket_starts()


def _bias_tiles_body(rb_ref, tile_ref, row_ref, *, tq, past):
    h = pl.program_id(0)

    def bias_of(dist):
        val = jnp.full(dist.shape, rb_ref[0, h], F32)
        for b in range(1, N_BUCKETS):
            val = jnp.where(dist >= _BUCKET_START[b], rb_ref[b, h], val)
        return val

    r = lax.broadcasted_iota(I32, (tq, tq), 0)
    c = lax.broadcasted_iota(I32, (tq, tq), 1)
    tile_ref[0, 2] = bias_of(r - c)
    tile_ref[0, 1] = bias_of(r - c + tq)
    tile_ref[0, 0] = bias_of(r - c + 2 * tq)
    lane = lax.broadcasted_iota(I32, (1, past + LANES), 1)
    row_ref[0] = bias_of(past - lane)


def _bias_tiles(rel_bias, tq, past):
    assert tq + 1 >= _BUCKET_START[N_BUCKETS - 1]
    return pl.pallas_call(
        functools.partial(_bias_tiles_body, tq=tq, past=past),
        grid=(N_HEADS,),
        in_specs=[pl.BlockSpec(memory_space=pltpu.SMEM)],
        out_specs=[pl.BlockSpec((1, 3, tq, tq), lambda h: (h, 0, 0, 0)),
                   pl.BlockSpec((1, 1, past + LANES), lambda h: (h, 0, 0))],
        out_shape=[jax.ShapeDtypeStruct((N_HEADS, 3, tq, tq), F32),
                   jax.ShapeDtypeStruct((N_HEADS, 1, past + LANES), F32)],
        compiler_params=_params("arbitrary"),
        name="bias_tiles",
    )(rel_bias)


def _in_proj_body(x_ref, g_ref, w_ref, q_ref, k_ref, v_ref, qi_ref, kw_ref, u_ref, sga_ref, sgs_ref,
                  *t_refs, transposed):
    h = _rms(x_ref[...], g_ref[...]).astype(BF16)

    def proj(c0, c1):
        return _dot(h, w_ref[:, c0:c1])

    q_ref[...] = proj(_C_Q, _C_K) * HEAD_DIM ** -0.5
    k = proj(_C_K, _C_V)
    v = proj(_C_V, _C_QI)
    k_ref[...] = k
    v_ref[...] = v
    qi_ref[...] = proj(_C_QI, _C_KW)
    kw = proj(_C_KW, _C_U)
    lane = lax.broadcasted_iota(I32, kw.shape, 1)
    kw = jnp.where(lane >= IDX_DIM, kw * IDX_HEADS ** -0.5, kw)
    kw_ref[...] = kw
    u_ref[...] = proj(_C_U, _C_GA)
    sga_ref[...] = jax.nn.sigmoid(proj(_C_GA, _C_GS))
    sgs_ref[...] = jax.nn.sigmoid(proj(_C_GS, _C_END))
    if transposed:
        kt_ref, vt_ref, kit_ref = t_refs
        tm = k.shape[0]
        kt_ref[0] = k.T.reshape(N_HEADS, HEAD_DIM, tm).astype(BF16)
        vt_ref[0] = v.T.reshape(N_HEADS, HEAD_DIM, tm).astype(BF16)
        kit_ref[0] = kw.T[:IDX_DIM].astype(BF16)


def _in_proj(x2, g, w_packed, bsz, t, tm, transposed):
    m = bsz * t
    nt = t // tm
    row = lambda b, i: (b * nt + i, 0)
    widths = (ATT_WIDTH, ATT_WIDTH, ATT_WIDTH, IDX_HEADS * IDX_DIM, LANES, SSM_WIDTH, D_MODEL, D_MODEL)
    out_specs = [pl.BlockSpec((tm, w), row) for w in widths]
    out_shape = [jax.ShapeDtypeStruct((m, w), F32) for w in widths]
    if transposed:
        out_specs += [pl.BlockSpec((1, N_HEADS, HEAD_DIM, tm), lambda b, i: (b, 0, 0, i)),
                      pl.BlockSpec((1, N_HEADS, HEAD_DIM, tm), lambda b, i: (b, 0, 0, i)),
                      pl.BlockSpec((1, IDX_DIM, tm), lambda b, i: (b, 0, i))]
        out_shape += [jax.ShapeDtypeStruct((bsz, N_HEADS, HEAD_DIM, t), BF16),
                      jax.ShapeDtypeStruct((bsz, N_HEADS, HEAD_DIM, t), BF16),
                      jax.ShapeDtypeStruct((bsz, IDX_DIM, t), BF16)]
    return pl.pallas_call(
        functools.partial(_in_proj_body, transposed=transposed),
        grid=(bsz, nt),
        in_specs=[pl.BlockSpec((tm, D_MODEL), row),
                  pl.BlockSpec((1, D_MODEL), lambda b, i: (0, 0)),
                  pl.BlockSpec((D_MODEL, _C_END), lambda b, i: (0, 0))],
        out_specs=out_specs,
        out_shape=out_shape,
        compiler_params=_params("arbitrary", "arbitrary"),
        name="in_proj",
    )(x2, g, w_packed)


def _sortable_key(score):
    bits = pltpu.bitcast(score, I32)
    return jnp.where(bits < 0, bits ^ 0x7FFFFFFF, bits)


def _topk_mask(keys_ref, mask_ref, nblk, width, n_rows, topk, rows_per_pass):
    rp = rows_per_pass

    def per_rows(rc, _):
        rows = pl.ds(pl.multiple_of(rc * rp, rp), rp)

        def count(pred):
            def body(j, acc):
                cols = pl.ds(pl.multiple_of(j * width, width), width)
                c = jnp.where(pred(keys_ref[rows, cols]), 1, 0)
                for t in range(width // LANES):
                    acc = acc + c[:, t * LANES:(t + 1) * LANES]
                return acc
            acc = lax.fori_loop(0, nblk, body, jnp.zeros((rp, LANES), I32))
            return jnp.sum(acc, axis=1, keepdims=True)

        n_nonneg = count(lambda k: k >= 0)
        prefix = jnp.where(n_nonneg >= topk, 0, KEY_EXCLUDED).astype(I32)

        def value_bit(b, prefix):
            cand = prefix | jnp.left_shift(1, 30 - b)
            n_ge = count(lambda k: k >= cand)
            return jnp.where(n_ge >= topk, cand, prefix)
        thr = lax.fori_loop(0, 31, value_bit, prefix)

        def encode(j, _):
            cols = pl.ds(pl.multiple_of(j * width, width), width)
            k = keys_ref[rows, cols]
            col = j * width + lax.broadcasted_iota(I32, (rp, width), 1)
            code = jnp.where(k > thr, -1, jnp.where(k == thr, col, CODE_NEVER))
            keys_ref[rows, cols] = jnp.where(k == KEY_EXCLUDED, CODE_NEVER, code)
            return 0
        lax.fori_loop(0, nblk, encode, 0)

        def index_bit(b, bound):
            cand = bound | jnp.left_shift(1, IDX_BITS - 1 - b)
            n_lt = count(lambda c: c < cand)
            return jnp.where(n_lt <= topk, cand, bound)
        bound = lax.fori_loop(0, IDX_BITS, index_bit, jnp.zeros((rp, 1), I32))

        def emit(j, _):
            cols = pl.ds(pl.multiple_of(j * width, width), width)
            mask_ref[rows, cols] = jnp.where(keys_ref[rows, cols] < bound, 0.0, -jnp.inf).astype(F32)
            return 0
        lax.fori_loop(0, nblk, emit, 0)
        return 0

    lax.fori_loop(0, n_rows // rp, per_rows, 0)


def _prompt_attn_body(q_ref, qi_ref, kw_ref, kt_ref, vt_ref, kit_ref, bias_ref, o_ref, keys_ref, mask_ref,
                      *, tq, topk):
    i = pl.program_id(1)
    nblk = i + 1
    row = lax.broadcasted_iota(I32, (tq, tq), 0)
    col = lax.broadcasted_iota(I32, (tq, tq), 1)

    qi = [qi_ref[:, h * IDX_DIM:(h + 1) * IDX_DIM].astype(BF16) for h in range(IDX_HEADS)]
    wi = [kw_ref[:, IDX_DIM + h:IDX_DIM + h + 1] for h in range(IDX_HEADS)]

    def score_block(j, _):
        cols = pl.ds(pl.multiple_of(j * tq, tq), tq)
        kt = kit_ref[0, :, cols]
        s = None
        for h in range(IDX_HEADS):
            sh = jnp.maximum(_dot(qi[h], kt) * IDX_DIM ** -0.5, 0.0) * wi[h]
            s = sh if s is None else s + sh
        causal = (j < i) | (col <= row)
        keys_ref[:, cols] = jnp.where(causal, _sortable_key(s), KEY_EXCLUDED)
        return 0
    lax.fori_loop(0, nblk, score_block, 0)

    _topk_mask(keys_ref, mask_ref, nblk, tq, tq, topk, rows_per_pass=64)

    outs = []
    for h in range(N_HEADS):
        qh = q_ref[:, h * HEAD_DIM:(h + 1) * HEAD_DIM].astype(BF16)

        def attend(j, carry, h=h, qh=qh):
            m, l, acc = carry
            cols = pl.ds(pl.multiple_of(j * tq, tq), tq)
            s = _dot(qh, kt_ref[0, h, :, cols])
            s = s + bias_ref[h, jnp.clip(j - i + 2, 0, 2)] + mask_ref[:, cols]
            m_new = jnp.maximum(m, jnp.max(s, axis=1, keepdims=True))
            m_fin = jnp.where(m_new == -jnp.inf, 0.0, m_new)
            p = jnp.exp(s - m_fin)
            alpha = jnp.exp(m - m_fin)
            l = alpha * l + jnp.sum(p, axis=1, keepdims=True)
            acc = alpha * acc + _dot_nt(p.astype(BF16), vt_ref[0, h, :, cols])
            return m_new, l, acc

        init = (jnp.full((tq, 1), -jnp.inf, F32), jnp.zeros((tq, 1), F32), jnp.zeros((tq, HEAD_DIM), F32))
        _, l, acc = lax.fori_loop(0, nblk, attend, init)
        outs.append(acc / l)
    o_ref[...] = jnp.concatenate(outs, axis=1)


def _prompt_attn(q, qi, kw, kt, vt, kit, bias_tiles, bsz, t, tq, topk):
    nq = t // tq
    row = lambda b, i: (b * nq + i, 0)
    return pl.pallas_call(
        functools.partial(_prompt_attn_body, tq=tq, topk=topk),
        grid=(bsz, nq),
        in_specs=[pl.BlockSpec((tq, ATT_WIDTH), row),
                  pl.BlockSpec((tq, IDX_HEADS * IDX_DIM), row),
                  pl.BlockSpec((tq, LANES), row),
                  pl.BlockSpec((1, N_HEADS, HEAD_DIM, t), lambda b, i: (b, 0, 0, 0), pipeline_mode=pl.Buffered(1)),
                  pl.BlockSpec((1, N_HEADS, HEAD_DIM, t), lambda b, i: (b, 0, 0, 0), pipeline_mode=pl.Buffered(1)),
                  pl.BlockSpec((1, IDX_DIM, t), lambda b, i: (b, 0, 0), pipeline_mode=pl.Buffered(1)),
                  pl.BlockSpec((N_HEADS, 3, tq, tq), lambda b, i: (0, 0, 0, 0), pipeline_mode=pl.Buffered(1))],
        out_specs=pl.BlockSpec((tq, ATT_WIDTH), row),
        out_shape=jax.ShapeDtypeStruct((bsz * t, ATT_WIDTH), F32),
        scratch_shapes=[pltpu.VMEM((tq, t), I32), pltpu.VMEM((tq, t), F32)],
        compiler_params=_params("arbitrary", "arbitrary"),
        name="prompt_attn",
    )(q, qi, kw, kt, vt, kit, bias_tiles)


def _sample_scores_body(pt_ref, qi_ref, w_ref, kn_ref, *rest, n_pages):
    page_refs, key_ref = rest[:n_pages], rest[n_pages]
    qi = qi_ref[...].astype(BF16)
    w = w_ref[...]

    def weighted(s):
        return jnp.sum(jnp.maximum(s * IDX_DIM ** -0.5, 0.0) * w, axis=0, keepdims=True)

    for p in range(n_pages):
        s = _dot_nt(qi, page_refs[p][...].astype(BF16))
        key_ref[:, p * PAGE_SIZE:(p + 1) * PAGE_SIZE] = _sortable_key(weighted(s))
    s_self = jnp.sum(qi.astype(F32) * kn_ref[...].astype(BF16).astype(F32), axis=1, keepdims=True)
    lane = lax.broadcasted_iota(I32, (1, LANES), 1)
    key_ref[:, n_pages * PAGE_SIZE:] = jnp.where(lane == 0, _sortable_key(weighted(s_self)), KEY_EXCLUDED)


def _sample_scores(page_table, qi3, w3, kn3, cache_kidx):
    db, n_pages = page_table.shape
    n_cols = n_pages * PAGE_SIZE + LANES
    per = lambda s, pt: (s, 0, 0)
    page_specs = [pl.BlockSpec((None, PAGE_SIZE, IDX_DIM), functools.partial(lambda s, pt, p: (pt[s, p], 0, 0), p=p))
                  for p in range(n_pages)]
    return pl.pallas_call(
        functools.partial(_sample_scores_body, n_pages=n_pages),
        grid_spec=pltpu.PrefetchScalarGridSpec(
            num_scalar_prefetch=1, grid=(db,),
            in_specs=[pl.BlockSpec((None, IDX_HEADS, IDX_DIM), per),
                      pl.BlockSpec((None, IDX_HEADS, 1), per),
                      pl.BlockSpec((None, 1, IDX_DIM), per)] + page_specs,
            out_specs=pl.BlockSpec((None, 1, n_cols), per)),
        out_shape=jax.ShapeDtypeStruct((db, 1, n_cols), I32),
        compiler_params=_params("arbitrary"),
        name="sample_scores",
    )(page_table, qi3, w3, kn3, *([cache_kidx] * n_pages))


def _sample_select_body(key_ref, mask_ref, keys_scr, *, n_rows, n_cols, topk):
    keys_scr[...] = key_ref[...]
    _topk_mask(keys_scr, mask_ref, n_cols // LANES, LANES, n_rows, topk, rows_per_pass=min(n_rows, 64))


def _sample_select(keys, topk):
    n_rows, n_cols = keys.shape
    return pl.pallas_call(
        functools.partial(_sample_select_body, n_rows=n_rows, n_cols=n_cols, topk=topk),
        out_shape=jax.ShapeDtypeStruct((n_rows, n_cols), F32),
        scratch_shapes=[pltpu.VMEM((n_rows, n_cols), I32)],
        compiler_params=pltpu.CompilerParams(vmem_limit_bytes=VMEM_LIMIT),
        name="sample_select",
    )(keys)


def _sample_attn_body(pt_ref, q_ref, kn_ref, vn_ref, mask_ref, bias_ref, *rest, n_pages):
    k_refs, v_refs, o_ref = rest[:n_pages], rest[n_pages:2 * n_pages], rest[2 * n_pages]
    past = n_pages * PAGE_SIZE
    head_of_lane = lax.broadcasted_iota(I32, (N_HEADS, ATT_WIDTH), 1) // HEAD_DIM
    own = head_of_lane == lax.broadcasted_iota(I32, (N_HEADS, ATT_WIDTH), 0)
    qbd = jnp.where(own, q_ref[...], 0.0).astype(BF16)
    bias = bias_ref[:, 0, :]
    mask = mask_ref[...]
    s = jnp.concatenate([_dot_nt(qbd, k_refs[p][...].astype(BF16)) for p in range(n_pages)], axis=1)
    s = s + bias[:, :past] + mask[:, :past]
    s_self = jnp.sum(qbd.astype(F32) * kn_ref[...].astype(BF16).astype(F32), axis=1, keepdims=True)
    s_self = s_self + bias[:, past:past + 1] + mask[:, past:past + 1]
    m = jnp.maximum(jnp.max(s, axis=1, keepdims=True), s_self)
    p_past = jnp.exp(s - m)
    p_self = jnp.exp(s_self - m)
    l = jnp.sum(p_past, axis=1, keepdims=True) + p_self
    acc = p_self.astype(BF16).astype(F32) * vn_ref[...].astype(BF16).astype(F32)
    p_past = p_past.astype(BF16)
    for p in range(n_pages):
        acc = acc + _dot(p_past[:, p * PAGE_SIZE:(p + 1) * PAGE_SIZE], v_refs[p][...].astype(BF16))
    o_ref[...] = jnp.sum(jnp.where(own, acc / l, 0.0), axis=0, keepdims=True)


def _sample_attn(page_table, q3, kn3, vn3, mask3, bias_row, cache_k, cache_v):
    db, n_pages = page_table.shape
    n_cols = n_pages * PAGE_SIZE + LANES
    per = lambda s, pt: (s, 0, 0)
    page_specs = [pl.BlockSpec((None, PAGE_SIZE, ATT_WIDTH), functools.partial(lambda s, pt, p: (pt[s, p], 0, 0), p=p))
                  for p in range(n_pages)]
    return pl.pallas_call(
        functools.partial(_sample_attn_body, n_pages=n_pages),
        grid_spec=pltpu.PrefetchScalarGridSpec(
            num_scalar_prefetch=1, grid=(db,),
            in_specs=[pl.BlockSpec((None, 1, ATT_WIDTH), per),
                      pl.BlockSpec((None, 1, ATT_WIDTH), per),
                      pl.BlockSpec((None, 1, ATT_WIDTH), per),
                      pl.BlockSpec((None, 1, n_cols), per),
                      pl.BlockSpec((N_HEADS, 1, n_cols), lambda s, pt: (0, 0, 0))] + page_specs + page_specs,
            out_specs=pl.BlockSpec((None, 1, ATT_WIDTH), per)),
        out_shape=jax.ShapeDtypeStruct((db, 1, ATT_WIDTH), F32),
        compiler_params=_params("arbitrary"),
        name="sample_attn",
    )(page_table, q3, kn3, vn3, mask3, bias_row, *([cache_k] * n_pages), *([cache_v] * n_pages))


def _cmul(ar, ai, br, bi):
    return ar * br - ai * bi, ar * bi + ai * br


def _ssm_prep_body(lre_f, lim_f, ldt_f, lre_r, lim_r, ldt_r, bre_ref, bim_ref, pre_ref, pim_ref, bbr_ref, bbi_ref,
                   *, chunk):
    def lam_bar(lre, lim, ldt):
        dt = jnp.exp(ldt)
        mag = jnp.exp(lre * dt)
        return mag * jnp.cos(lim * dt), mag * jnp.sin(lim * dt)

    lbr, lbi = lam_bar(lre_f[...], lim_f[...], ldt_f[...])
    pr = jnp.broadcast_to(lbr, (chunk, N_STATE))
    pi = jnp.broadcast_to(lbi, (chunk, N_STATE))
    row = lax.broadcasted_iota(I32, (chunk, N_STATE), 0)
    d = 1
    while d < chunk:
        sr = jnp.where(row >= d, pltpu.roll(pr, d, 0), 1.0)
        si = jnp.where(row >= d, pltpu.roll(pi, d, 0), 0.0)
        pr, pi = _cmul(pr, pi, sr, si)
        d *= 2
    pre_ref[...] = pr
    pim_ref[...] = pi

    lre, lim = lre_r[...], lim_r[...]
    lbr, lbi = lam_bar(lre, lim, ldt_r[...])
    den = lre * lre + lim * lim
    nr, ni = lbr - 1.0, lbi
    cr = (nr * lre + ni * lim) / den
    ci = (ni * lre - nr * lim) / den
    bbr, bbi = _cmul(cr, ci, bre_ref[...], bim_ref[...])
    bbr_ref[...] = bbr
    bbi_ref[...] = bbi


def _ssm_prep(lam_re, lam_im, log_dt, b_re, b_im, chunk):
    flat = lambda a: a.reshape(1, N_STATE)
    rep = lambda a: jnp.repeat(a, GROUP, axis=0)
    ldt = jnp.broadcast_to(log_dt[:, None], (N_GROUPS, STATE_DIM))
    bt = lambda b: jnp.swapaxes(b, 1, 2).reshape(SSM_WIDTH, STATE_DIM)
    return pl.pallas_call(
        functools.partial(_ssm_prep_body, chunk=chunk),
        out_shape=[jax.ShapeDtypeStruct((chunk, N_STATE), F32), jax.ShapeDtypeStruct((chunk, N_STATE), F32),
                   jax.ShapeDtypeStruct((SSM_WIDTH, STATE_DIM), F32), jax.ShapeDtypeStruct((SSM_WIDTH, STATE_DIM), F32)],
        compiler_params=pltpu.CompilerParams(vmem_limit_bytes=VMEM_LIMIT),
        name="ssm_prep",
    )(flat(lam_re), flat(lam_im), flat(ldt), rep(lam_re), rep(lam_im), rep(ldt), bt(b_re), bt(b_im))


def _block_diag(blocks):
    g, r, c = blocks.shape
    eye = jnp.eye(g, dtype=blocks.dtype)
    return (blocks[:, :, None, :] * eye[:, None, :, None]).reshape(g * r, g * c)


def _ssm_readout(u, xr, xi, wc_ref, dskip_ref, wglu_ref, bglu_ref):
    y = _dot(jnp.concatenate([xr, xi], axis=1).astype(BF16), wc_ref[...]) + dskip_ref[...] * u
    g = jax.nn.gelu(y)
    return g * jax.nn.sigmoid(_dot(g.astype(BF16), wglu_ref[...]) + bglu_ref[...])


def _ssm_scan_body(u_ref, x0r_ref, x0i_ref, pre_ref, pim_ref, wb_ref, wc_ref, dskip_ref, wglu_ref, bglu_ref,
                   y_ref, sr_ref, si_ref, cr_ref, ci_ref, *, chunk):
    c = pl.program_id(1)

    @pl.when(c == 0)
    def _():
        cr_ref[...] = x0r_ref[0]
        ci_ref[...] = x0i_ref[0]

    u = u_ref[...]
    bu = _dot(u.astype(BF16), wb_ref[...])
    xr, xi = bu[:, :N_STATE], bu[:, N_STATE:]
    row = lax.broadcasted_iota(I32, (chunk, N_STATE), 0)
    d = 1
    while d < chunk:
        sr = jnp.where(row >= d, pltpu.roll(xr, d, 0), 0.0)
        si = jnp.where(row >= d, pltpu.roll(xi, d, 0), 0.0)
        tr, ti = _cmul(pre_ref[d - 1:d, :], pim_ref[d - 1:d, :], sr, si)
        xr, xi = xr + tr, xi + ti
        d *= 2
    tr, ti = _cmul(pre_ref[...], pim_ref[...], cr_ref[...], ci_ref[...])
    xr, xi = xr + tr, xi + ti
    cr_ref[...] = xr[chunk - 1:chunk, :]
    ci_ref[...] = xi[chunk - 1:chunk, :]
    sr_ref[0] = xr[chunk - 1:chunk, :]
    si_ref[0] = xi[chunk - 1:chunk, :]
    y_ref[...] = _ssm_readout(u, xr, xi, wc_ref, dskip_ref, wglu_ref, bglu_ref)


def _ssm_scan(u, x0r, x0i, pre, pim, wb, wc, dskip, wglu, bglu, bsz, t, chunk):
    nc = t // chunk
    row = lambda b, c: (b * nc + c, 0)
    const = lambda b, c: (0, 0)
    state = pl.BlockSpec((1, 1, N_STATE), lambda b, c: (b, 0, 0))
    return pl.pallas_call(
        functools.partial(_ssm_scan_body, chunk=chunk),
        grid=(bsz, nc),
        in_specs=[pl.BlockSpec((chunk, SSM_WIDTH), row), state, state,
                  pl.BlockSpec((chunk, N_STATE), const), pl.BlockSpec((chunk, N_STATE), const),
                  pl.BlockSpec((SSM_WIDTH, 2 * N_STATE), const), pl.BlockSpec((2 * N_STATE, SSM_WIDTH), const),
                  pl.BlockSpec((1, SSM_WIDTH), const), pl.BlockSpec((SSM_WIDTH, SSM_WIDTH), const),
                  pl.BlockSpec((1, SSM_WIDTH), const)],
        out_specs=[pl.BlockSpec((chunk, SSM_WIDTH), row), state, state],
        out_shape=[jax.ShapeDtypeStruct((bsz * t, SSM_WIDTH), F32),
                   jax.ShapeDtypeStruct((bsz, 1, N_STATE), F32), jax.ShapeDtypeStruct((bsz, 1, N_STATE), F32)],
        scratch_shapes=[pltpu.VMEM((1, N_STATE), F32), pltpu.VMEM((1, N_STATE), F32)],
        compiler_params=_params("arbitrary", "arbitrary"),
        name="ssm_scan",
    )(u, x0r, x0i, pre, pim, wb, wc, dskip, wglu, bglu)


def _ssm_step_body(u_ref, x0r_ref, x0i_ref, pre_ref, pim_ref, wb_ref, wc_ref, dskip_ref, wglu_ref, bglu_ref,
                   y_ref, sr_ref, si_ref):
    u = u_ref[...]
    bu = _dot(u.astype(BF16), wb_ref[...])
    tr, ti = _cmul(pre_ref[0:1, :], pim_ref[0:1, :], x0r_ref[...], x0i_ref[...])
    xr, xi = bu[:, :N_STATE] + tr, bu[:, N_STATE:] + ti
    sr_ref[...] = xr
    si_ref[...] = xi
    y_ref[...] = _ssm_readout(u, xr, xi, wc_ref, dskip_ref, wglu_ref, bglu_ref)


def _ssm_step(u, x0r, x0i, pre, pim, wb, wc, dskip, wglu, bglu):
    n = u.shape[0]
    return pl.pallas_call(
        _ssm_step_body,
        out_shape=[jax.ShapeDtypeStruct((n, SSM_WIDTH), F32),
                   jax.ShapeDtypeStruct((n, N_STATE), F32), jax.ShapeDtypeStruct((n, N_STATE), F32)],
        compiler_params=pltpu.CompilerParams(vmem_limit_bytes=VMEM_LIMIT),
        name="ssm_step",
    )(u, x0r, x0i, pre, pim, wb, wc, dskip, wglu, bglu)


def _merge_body(x_ref, ya_ref, ys_ref, sga_ref, sgs_ref, wa_ref, ws_ref, wo_ref, gpost_ref, gpre_ref,
                x1_ref, h2_ref):
    merged = (sga_ref[...] * _dot(ya_ref[...].astype(BF16), wa_ref[...])
              + sgs_ref[...] * _dot(ys_ref[...].astype(BF16), ws_ref[...]))
    x1 = x_ref[...] + _rms(_dot(merged.astype(BF16), wo_ref[...]), gpost_ref[...])
    x1_ref[...] = x1
    h2_ref[...] = _rms(x1, gpre_ref[...]).astype(BF16)


def _merge(x2, ya, ys, sga, sgs, wa, ws, wo, gpost, gpre, tm):
    m = x2.shape[0]
    row = lambda i: (i, 0)
    const = lambda i: (0, 0)
    return pl.pallas_call(
        _merge_body,
        grid=(m // tm,),
        in_specs=[pl.BlockSpec((tm, D_MODEL), row), pl.BlockSpec((tm, ATT_WIDTH), row),
                  pl.BlockSpec((tm, SSM_WIDTH), row), pl.BlockSpec((tm, D_MODEL), row),
                  pl.BlockSpec((tm, D_MODEL), row),
                  pl.BlockSpec((ATT_WIDTH, D_MODEL), const), pl.BlockSpec((SSM_WIDTH, D_MODEL), const),
                  pl.BlockSpec((D_MODEL, D_MODEL), const), pl.BlockSpec((1, D_MODEL), const),
                  pl.BlockSpec((1, D_MODEL), const)],
        out_specs=[pl.BlockSpec((tm, D_MODEL), row), pl.BlockSpec((tm, D_MODEL), row)],
        out_shape=[jax.ShapeDtypeStruct((m, D_MODEL), F32), jax.ShapeDtypeStruct((m, D_MODEL), BF16)],
        compiler_params=_params("arbitrary"),
        name="merge",
    )(x2, ya, ys, sga, sgs, wa, ws, wo, gpost, gpre)


def _ffn_body(h_ref, x_ref, wua_ref, wub_ref, cwa_ref, cwb_ref, cba_ref, cbb_ref, wd_ref, g_ref, pa_ref, pb_ref,
              y_ref, oa_ref, ob_ref, f_ref, *carry, seq, tiles_per_seq):
    i, j = pl.program_id(0), pl.program_id(1)
    nj = pl.num_programs(1)
    h = h_ref[...]
    tm = h.shape[0]

    def conv(up, cw_ref, cb_ref, prev_ref, carry_ref, out_ref):
        if seq:
            @pl.when(i % tiles_per_seq == 0)
            def _():
                carry_ref[j, 0:CONV_W - 1, :] = prev_ref[0]
            row = lax.broadcasted_iota(I32, up.shape, 0)
            before1 = carry_ref[j, 1:2, :]
            before2 = carry_ref[j, 0:1, :]
            m1 = jnp.where(row == 0, before1, pltpu.roll(up, 1, 0))
            m2 = jnp.where(row == 0, before2, jnp.where(row == 1, before1, pltpu.roll(up, 2, 0)))
            tail = up[tm - (CONV_W - 1):, :]
            carry_ref[j, 0:CONV_W - 1, :] = tail
            out_ref[0] = tail
        else:
            m2, m1 = prev_ref[:, 0, :], prev_ref[:, 1, :]
            out_ref[...] = up
        return cb_ref[...] + m2 * cw_ref[0:1, :] + m1 * cw_ref[1:2, :] + up * cw_ref[2:3, :]

    ca, cb = carry if seq else (None, None)
    a = conv(_dot(h, wua_ref[...]), cwa_ref, cba_ref, pa_ref, ca, oa_ref)
    b = conv(_dot(h, wub_ref[...]), cwb_ref, cbb_ref, pb_ref, cb, ob_ref)
    part = _dot((jax.nn.gelu(a) * b).astype(BF16), wd_ref[...])

    @pl.when(j == 0)
    def _():
        f_ref[...] = part

    @pl.when(j > 0)
    def _():
        f_ref[...] += part

    @pl.when(j == nj - 1)
    def _():
        y_ref[...] = x_ref[...] + _rms(f_ref[...], g_ref[...])


def _ffn(h2, x1, w_up, conv_w, conv_b, w_down, g_post, conv_prev, bsz, t, tm, tf, seq):
    m = bsz * t
    nj = D_FF // tf
    row = lambda i, j: (i, 0)
    if seq:
        tiles_per_seq = t // tm
        prev_a = pl.BlockSpec((1, CONV_W - 1, tf), lambda i, j: (i // tiles_per_seq, 0, j))
        prev_b = pl.BlockSpec((1, CONV_W - 1, tf), lambda i, j: (i // tiles_per_seq, 0, j + nj))
        out_tail = pl.BlockSpec((1, CONV_W - 1, tf), lambda i, j: (i, 0, j))
        tail_shape = jax.ShapeDtypeStruct((m // tm, CONV_W - 1, D_FF), F32)
        scratch = [pltpu.VMEM((nj, 8, tf), F32), pltpu.VMEM((nj, 8, tf), F32)]
    else:
        tiles_per_seq = 1
        prev_a = pl.BlockSpec((tm, CONV_W - 1, tf), lambda i, j: (i, 0, j))
        prev_b = pl.BlockSpec((tm, CONV_W - 1, tf), lambda i, j: (i, 0, j + nj))
        out_tail = pl.BlockSpec((tm, tf), lambda i, j: (i, j))
        tail_shape = jax.ShapeDtypeStruct((m, D_FF), F32)
        scratch = []
    return pl.pallas_call(
        functools.partial(_ffn_body, seq=seq, tiles_per_seq=tiles_per_seq),
        grid=(m // tm, nj),
        in_specs=[pl.BlockSpec((tm, D_MODEL), row), pl.BlockSpec((tm, D_MODEL), row),
                  pl.BlockSpec((D_MODEL, tf), lambda i, j: (0, j)),
                  pl.BlockSpec((D_MODEL, tf), lambda i, j: (0, j + nj)),
                  pl.BlockSpec((CONV_W, tf), lambda i, j: (0, j)),
                  pl.BlockSpec((CONV_W, tf), lambda i, j: (0, j + nj)),
                  pl.BlockSpec((1, tf), lambda i, j: (0, j)),
                  pl.BlockSpec((1, tf), lambda i, j: (0, j + nj)),
                  pl.BlockSpec((tf, D_MODEL), lambda i, j: (j, 0)),
                  pl.BlockSpec((1, D_MODEL), lambda i, j: (0, 0)),
                  prev_a, prev_b],
        out_specs=[pl.BlockSpec((tm, D_MODEL), row), out_tail, out_tail],
        out_shape=[jax.ShapeDtypeStruct((m, D_MODEL), F32), tail_shape, tail_shape],
        scratch_shapes=[pltpu.VMEM((tm, D_MODEL), F32)] + scratch,
        compiler_params=_params("arbitrary", "arbitrary"),
        name="conv_ffn",
    )(h2, x1, w_up, w_up, conv_w, conv_w, conv_b, conv_b, w_down, g_post, conv_prev, conv_prev)


def _pack_w_in(w_in):
    points = np.cumsum(SPLITS)[:-1].tolist()
    wq, wk, wv, wqi, wki, wwi, wu, wga, wgs = jnp.split(w_in, points, axis=-1)
    pad = jnp.zeros((D_MODEL, LANES - IDX_DIM - IDX_HEADS), w_in.dtype)
    return jnp.concatenate([wq, wk, wv, wqi, wki, wwi, pad, wu, wga, wgs], axis=-1).astype(BF16)


def _layer_weights(lw, chunk):
    (w_in, g_pre_mix, g_post_mix, lam_re, lam_im, log_dt, b_re, b_im, c_re, c_im, d_skip,
     w_glu, b_glu, w_att_out, w_ssm_out, w_o, g_pre_ffn, g_post_ffn, w_up, conv_w, conv_b, w_down) = lw
    pre, pim, bbr, bbi = _ssm_prep(lam_re, lam_im, log_dt, b_re, b_im, chunk)
    wb = jnp.concatenate([_block_diag(bbr.reshape(N_GROUPS, GROUP, STATE_DIM)),
                          _block_diag(bbi.reshape(N_GROUPS, GROUP, STATE_DIM))], axis=1).astype(BF16)
    wc = jnp.concatenate([_block_diag(jnp.swapaxes(c_re, 1, 2)),
                          _block_diag(-jnp.swapaxes(c_im, 1, 2))], axis=0).astype(BF16)
    vec = lambda a: a.reshape(1, -1)
    return dict(
        w_in=_pack_w_in(w_in), g_pre_mix=vec(g_pre_mix), g_post_mix=vec(g_post_mix),
        pre=pre, pim=pim, wb=wb, wc=wc, d_skip=vec(d_skip), w_glu=w_glu.astype(BF16), b_glu=vec(b_glu),
        w_att_out=w_att_out.astype(BF16), w_ssm_out=w_ssm_out.astype(BF16), w_o=w_o.astype(BF16),
        g_pre_ffn=vec(g_pre_ffn), g_post_ffn=vec(g_post_ffn), w_up=w_up.astype(BF16), conv_w=conv_w,
        conv_b=vec(conv_b), w_down=w_down.astype(BF16))


def _prompt_layer(x, w, bias_tiles, tq, tm, chunk, tf):
    bsz, t, _ = x.shape
    x2 = x.reshape(bsz * t, D_MODEL)
    q, k, v, qi, kw, u, sga, sgs, kt, vt, kit = _in_proj(x2, w["g_pre_mix"], w["w_in"], bsz, t, tm, True)
    y_att = _prompt_attn(q, qi, kw, kt, vt, kit, bias_tiles, bsz, t, tq, min(TOPK_MAX, t // 4))
    zero_state = jnp.zeros((bsz, 1, N_STATE), F32)
    y_ssm, s_re, s_im = _ssm_scan(u, zero_state, zero_state, w["pre"], w["pim"], w["wb"], w["wc"], w["d_skip"],
                                  w["w_glu"], w["b_glu"], bsz, t, chunk)
    x1, h2 = _merge(x2, y_att, y_ssm, sga, sgs, w["w_att_out"], w["w_ssm_out"], w["w_o"],
                    w["g_post_mix"], w["g_pre_ffn"], tm)
    zero_conv = jnp.zeros((bsz, CONV_W - 1, 2 * D_FF), F32)
    y, tail_a, tail_b = _ffn(h2, x1, w["w_up"], w["conv_w"], w["conv_b"], w["w_down"], w["g_post_ffn"],
                             zero_conv, bsz, t, tm, tf, True)
    state = (k.reshape(bsz, t, N_HEADS, HEAD_DIM), v.reshape(bsz, t, N_HEADS, HEAD_DIM),
             kw[:, :IDX_DIM].reshape(bsz, t, IDX_DIM),
             s_re.reshape(bsz, N_GROUPS, STATE_DIM), s_im.reshape(bsz, N_GROUPS, STATE_DIM),
             jnp.concatenate([tail_a, tail_b], axis=-1)[t // tm - 1::t // tm])
    return y.reshape(bsz, t, D_MODEL), state


def _sample_layer(x, w, bias_row, cache_k, cache_v, cache_kidx, page_table, st_re, st_im, st_conv, tf):
    db, tq, _ = x.shape
    assert tq == 1, "the sample group is decoded one token per sequence"
    n_pool = cache_k.shape[0]
    past = page_table.shape[1] * PAGE_SIZE
    x2 = x.reshape(db, D_MODEL)
    q, k, v, qi, kw, u, sga, sgs = _in_proj(x2, w["g_pre_mix"], w["w_in"], 1, db, db, False)
    keys = _sample_scores(page_table, qi.reshape(db, IDX_HEADS, IDX_DIM),
                          kw[:, IDX_DIM:IDX_DIM + IDX_HEADS].reshape(db, IDX_HEADS, 1),
                          kw[:, :IDX_DIM].reshape(db, 1, IDX_DIM), cache_kidx)
    mask = _sample_select(keys.reshape(db, past + LANES), min(TOPK_MAX, (past + tq) // 4))
    y_att = _sample_attn(page_table, q.reshape(db, 1, ATT_WIDTH), k.reshape(db, 1, ATT_WIDTH),
                         v.reshape(db, 1, ATT_WIDTH), mask.reshape(db, 1, past + LANES), bias_row,
                         cache_k.reshape(n_pool, PAGE_SIZE, ATT_WIDTH), cache_v.reshape(n_pool, PAGE_SIZE, ATT_WIDTH))
    y_ssm, s_re, s_im = _ssm_step(u, st_re.reshape(db, N_STATE), st_im.reshape(db, N_STATE), w["pre"], w["pim"],
                                  w["wb"], w["wc"], w["d_skip"], w["w_glu"], w["b_glu"])
    x1, h2 = _merge(x2, y_att.reshape(db, ATT_WIDTH), y_ssm, sga, sgs, w["w_att_out"], w["w_ssm_out"], w["w_o"],
                    w["g_post_mix"], w["g_pre_ffn"], db)
    y, up_a, up_b = _ffn(h2, x1, w["w_up"], w["conv_w"], w["conv_b"], w["w_down"], w["g_post_ffn"],
                         st_conv, db, 1, db, tf, False)
    conv_new = jnp.concatenate([st_conv[:, 1:], jnp.concatenate([up_a, up_b], axis=-1)[:, None, :]], axis=1)
    state = (k.reshape(db, 1, N_HEADS, HEAD_DIM), v.reshape(db, 1, N_HEADS, HEAD_DIM),
             kw[:, :IDX_DIM].reshape(db, 1, IDX_DIM),
             s_re.reshape(db, N_GROUPS, STATE_DIM), s_im.reshape(db, N_GROUPS, STATE_DIM), conv_new)
    return y.reshape(db, 1, D_MODEL), state


def kernel(x_prompt, x_sample, cache_k, cache_v, cache_kidx, state_ssm_re, state_ssm_im, state_conv, page_table,
           rel_bias, w_in, g_pre_mix, g_post_mix, lam_re, lam_im, log_dt, b_re, b_im, c_re, c_im, d_skip, w_glu,
           b_glu, w_att_out, w_ssm_out, w_o, g_pre_ffn, g_post_ffn, w_up, conv_w, conv_b, w_down):
    depth = w_in.shape[0]
    t = x_prompt.shape[1]
    past = page_table.shape[1] * PAGE_SIZE
    tq = min(256, t)
    tm = min(512, t)
    chunk = min(256, t)
    tf = 512
    bias_tiles, bias_row = _bias_tiles(rel_bias, tq, past)
    y_p, y_s = x_prompt, x_sample
    outs_p, outs_s = [], []
    for l in range(depth):
        lw = (w_in[l], g_pre_mix[l], g_post_mix[l], lam_re[l], lam_im[l], log_dt[l], b_re[l], b_im[l],
              c_re[l], c_im[l], d_skip[l], w_glu[l], b_glu[l], w_att_out[l], w_ssm_out[l], w_o[l],
              g_pre_ffn[l], g_post_ffn[l], w_up[l], conv_w[l], conv_b[l], w_down[l])
        w = _layer_weights(lw, chunk)
        y_p, st_p = _prompt_layer(y_p, w, bias_tiles, tq, tm, chunk, tf)
        y_s, st_s = _sample_layer(y_s, w, bias_row, cache_k[l], cache_v[l], cache_kidx[l], page_table,
                                  state_ssm_re[l], state_ssm_im[l], state_conv[l], tf)
        outs_p.append(st_p)
        outs_s.append(st_s)
    k_p, v_p, ki_p, sr_p, si_p, cv_p = [jnp.stack(a) for a in zip(*outs_p)]
    k_s, v_s, ki_s, sr_s, si_s, cv_s = [jnp.stack(a) for a in zip(*outs_s)]
    return (y_p, y_s, k_p, v_p, ki_p, sr_p, si_p, cv_p, k_s, v_s, ki_s, sr_s, si_s, cv_s)
```

```python
import functools
import math

import numpy as np
import jax
import jax.numpy as jnp
from jax import lax
from jax.experimental import pallas as pl
from jax.experimental.pallas import tpu as pltpu

F32 = jnp.float32
BF16 = jnp.bfloat16
I32 = jnp.int32

D_MODEL = 1024
PAGE_SIZE = 128
N_HEADS = 8
HEAD_DIM = 64
ATT_WIDTH = N_HEADS * HEAD_DIM
IDX_HEADS = 4
IDX_DIM = 64
TOPK_MAX = 256
N_BUCKETS = 32
MAX_DISTANCE = 128
SSM_WIDTH = 512
GROUP = 16
N_GROUPS = SSM_WIDTH // GROUP
STATE_DIM = 64
N_STATE = N_GROUPS * STATE_DIM
D_FF = 4 * D_MODEL
CONV_W = 3
EPS = 1e-6
SPLITS = (ATT_WIDTH, ATT_WIDTH, ATT_WIDTH, IDX_HEADS * IDX_DIM, IDX_DIM, IDX_HEADS, SSM_WIDTH, D_MODEL, D_MODEL)

LANES = 128
KEY_EXCLUDED = -(2 ** 31)
CODE_NEVER = 2 ** 30
IDX_BITS = 14
BF16_SUBLANES = 16
V_ROWS = HEAD_DIM + BF16_SUBLANES
MASKED = -1e30
VMEM_LIMIT = 56 * 1024 * 1024

_C_Q, _C_K, _C_V = 0, ATT_WIDTH, 2 * ATT_WIDTH
_C_QI = 3 * ATT_WIDTH
_C_KW = _C_QI + IDX_HEADS * IDX_DIM
_C_U = _C_KW + LANES
_C_GA = _C_U + SSM_WIDTH
_C_GS = _C_GA + D_MODEL
_C_END = _C_GS + D_MODEL


def _params(*sem):
    return pltpu.CompilerParams(dimension_semantics=sem, vmem_limit_bytes=VMEM_LIMIT)


def _rms(x, g):
    inv = lax.rsqrt(jnp.mean(x * x, axis=-1, keepdims=True) + EPS)
    return (x * inv) * g


def _dot(a, b):
    return jnp.dot(a, b, preferred_element_type=F32)


def _dot_nt(a, b):
    return lax.dot_general(a, b, (((1,), (1,)), ((), ())), preferred_element_type=F32)


def _bucket_starts():
    n = np.arange(0, 1 << IDX_BITS, dtype=np.int32)
    max_exact = N_BUCKETS // 2
    nf = np.maximum(n, 1).astype(np.float32)
    large = max_exact + (np.log(nf / np.float32(max_exact)) / np.float32(math.log(MAX_DISTANCE / max_exact))
                         * np.float32(N_BUCKETS - max_exact)).astype(np.int32)
    large = np.minimum(large, N_BUCKETS - 1)
    bucket = np.where(n < max_exact, n, large)
    assert np.all(np.diff(bucket) >= 0)
    starts = [int(np.argmax(bucket >= b)) for b in range(N_BUCKETS)]
    assert all(bucket[s] == b for b, s in enumerate(starts))
    return starts


_BUCKET_START = _bucket_starts()


def _bias_tiles_body(rb_ref, tile_ref, row_ref, *, tq, past):
    h = pl.program_id(0)

    def bias_of(dist):
        val = jnp.full(dist.shape, rb_ref[0, h], F32)
        for b in range(1, N_BUCKETS):
            val = jnp.where(dist >= _BUCKET_START[b], rb_ref[b, h], val)
        return val

    r = lax.broadcasted_iota(I32, (tq, tq), 0)
    c = lax.broadcasted_iota(I32, (tq, tq), 1)
    far = rb_ref[N_BUCKETS - 1, h]
    tile_ref[0, 1] = bias_of(r - c) - far
    tile_ref[0, 0] = bias_of(r - c + tq) - far
    lane = lax.broadcasted_iota(I32, (1, past + LANES), 1)
    row_ref[0] = bias_of(past - lane)


def _bias_tiles(rel_bias, tq, past):
    assert tq + 1 >= _BUCKET_START[N_BUCKETS - 1]
    return pl.pallas_call(
        functools.partial(_bias_tiles_body, tq=tq, past=past),
        grid=(N_HEADS,),
        in_specs=[pl.BlockSpec(memory_space=pltpu.SMEM)],
        out_specs=[pl.BlockSpec((1, 2, tq, tq), lambda h: (h, 0, 0, 0)),
                   pl.BlockSpec((1, 1, past + LANES), lambda h: (h, 0, 0))],
        out_shape=[jax.ShapeDtypeStruct((N_HEADS, 2, tq, tq), F32),
                   jax.ShapeDtypeStruct((N_HEADS, 1, past + LANES), F32)],
        compiler_params=_params("arbitrary"),
        name="bias_tiles",
    )(rel_bias)


def _in_proj_body(x_ref, g_ref, w_ref, q_ref, k_ref, v_ref, qi_ref, kw_ref, u_ref, sga_ref, sgs_ref,
                  *t_refs, transposed):
    h = _rms(x_ref[...], g_ref[...]).astype(BF16)

    def proj(c0, c1):
        return _dot(h, w_ref[:, c0:c1])

    q_ref[...] = (proj(_C_Q, _C_K) * HEAD_DIM ** -0.5).astype(BF16)
    k = proj(_C_K, _C_V)
    v = proj(_C_V, _C_QI)
    k_ref[...] = k
    v_ref[...] = v
    qi_ref[...] = proj(_C_QI, _C_KW)
    kw = proj(_C_KW, _C_U)
    lane = lax.broadcasted_iota(I32, kw.shape, 1)
    kw = jnp.where(lane >= IDX_DIM, kw * IDX_HEADS ** -0.5, kw)
    kw_ref[...] = kw
    u_ref[...] = proj(_C_U, _C_GA)
    sga_ref[...] = jax.nn.sigmoid(proj(_C_GA, _C_GS))
    sgs_ref[...] = jax.nn.sigmoid(proj(_C_GS, _C_END))
    if transposed:
        kt_ref, vt_ref, kit_ref = t_refs
        tm = k.shape[0]
        kt_ref[0] = k.T.reshape(N_HEADS, HEAD_DIM, tm).astype(BF16)
        vt_ref[0, :, :HEAD_DIM, :] = v.T.reshape(N_HEADS, HEAD_DIM, tm).astype(BF16)
        vt_ref[0, :, HEAD_DIM:, :] = jnp.ones((N_HEADS, V_ROWS - HEAD_DIM, tm), BF16)
        kit_ref[0] = kw.T[:IDX_DIM].astype(BF16)


def _in_proj(x2, g, w_packed, bsz, t, tm, transposed):
    m = bsz * t
    nt = t // tm
    row = lambda b, i: (b * nt + i, 0)
    widths = (ATT_WIDTH, ATT_WIDTH, ATT_WIDTH, IDX_HEADS * IDX_DIM, LANES, SSM_WIDTH, D_MODEL, D_MODEL)
    out_specs = [pl.BlockSpec((tm, w), row) for w in widths]
    out_shape = [jax.ShapeDtypeStruct((m, w), BF16 if n == 0 else F32) for n, w in enumerate(widths)]
    if transposed:
        out_specs += [pl.BlockSpec((1, N_HEADS, HEAD_DIM, tm), lambda b, i: (b, 0, 0, i)),
                      pl.BlockSpec((1, N_HEADS, V_ROWS, tm), lambda b, i: (b, 0, 0, i)),
                      pl.BlockSpec((1, IDX_DIM, tm), lambda b, i: (b, 0, i))]
        out_shape += [jax.ShapeDtypeStruct((bsz, N_HEADS, HEAD_DIM, t), BF16),
                      jax.ShapeDtypeStruct((bsz, N_HEADS, V_ROWS, t), BF16),
                      jax.ShapeDtypeStruct((bsz, IDX_DIM, t), BF16)]
    return pl.pallas_call(
        functools.partial(_in_proj_body, transposed=transposed),
        grid=(bsz, nt),
        in_specs=[pl.BlockSpec((tm, D_MODEL), row),
                  pl.BlockSpec((1, D_MODEL), lambda b, i: (0, 0)),
                  pl.BlockSpec((D_MODEL, _C_END), lambda b, i: (0, 0))],
        out_specs=out_specs,
        out_shape=out_shape,
        compiler_params=_params("arbitrary", "arbitrary"),
        name="in_proj",
    )(x2, g, w_packed)


def _sortable_key(score):
    bits = pltpu.bitcast(score, I32)
    return jnp.where(bits < 0, bits ^ 0x7FFFFFFF, bits)


def _topk_mask(keys_ref, mask_ref, nblk, width, n_rows, topk, n_groups):
    rg = n_rows // n_groups
    groups = [pl.ds(g * rg, rg) for g in range(n_groups)]

    def col_block(j):
        return pl.ds(pl.multiple_of(j * width, width), width)

    def count(pred, bounds):
        def body(j, accs):
            out = []
            for rows, bound, acc in zip(groups, bounds, accs):
                hit = jnp.where(pred(keys_ref[rows, col_block(j)], bound), 1, 0)
                for t in range(width // LANES):
                    acc = acc + hit[:, t * LANES:(t + 1) * LANES]
                out.append(acc)
            return tuple(out)
        accs = lax.fori_loop(0, nblk, body, tuple(jnp.zeros((rg, LANES), I32) for _ in groups))
        return [jnp.sum(acc, axis=1, keepdims=True) for acc in accs]

    zero = tuple(jnp.zeros((rg, 1), I32) for _ in groups)

    n_nonneg = count(lambda k, b: k >= b, zero)
    prefix = tuple(jnp.where(n >= topk, 0, KEY_EXCLUDED).astype(I32) for n in n_nonneg)

    def value_bit(b, prefix):
        cand = [p | jnp.left_shift(1, 30 - b) for p in prefix]
        n_ge = count(lambda k, c: k >= c, cand)
        return tuple(jnp.where(n >= topk, c, p) for n, c, p in zip(n_ge, cand, prefix))
    thr = lax.fori_loop(0, 31, value_bit, prefix)

    def encode(j, _):
        for rows, t in zip(groups, thr):
            k = keys_ref[rows, col_block(j)]
            col = j * width + lax.broadcasted_iota(I32, (rg, width), 1)
            code = jnp.where(k > t, -1, jnp.where(k == t, col, CODE_NEVER))
            keys_ref[rows, col_block(j)] = jnp.where(k == KEY_EXCLUDED, CODE_NEVER, code)
        return 0
    lax.fori_loop(0, nblk, encode, 0)

    def index_bit(b, bound):
        cand = [p | jnp.left_shift(1, IDX_BITS - 1 - b) for p in bound]
        n_lt = count(lambda c, x: c < x, cand)
        return tuple(jnp.where(n <= topk, c, p) for n, c, p in zip(n_lt, cand, bound))
    bound = lax.fori_loop(0, IDX_BITS, index_bit, zero)

    def emit(j, _):
        for rows, x in zip(groups, bound):
            mask_ref[rows, col_block(j)] = jnp.where(keys_ref[rows, col_block(j)] < x, 0.0, MASKED).astype(F32)
        return 0
    lax.fori_loop(0, nblk, emit, 0)


def _prompt_attn_body(q_ref, qi_ref, kw_ref, kt_ref, vt_ref, kit_ref, bias_ref, o_ref, keys_ref, mask_ref,
                      m_ref, acc_ref, *, tq, topk):
    i = pl.program_id(1)
    nblk = i + 1
    row = lax.broadcasted_iota(I32, (tq, tq), 0)
    col = lax.broadcasted_iota(I32, (tq, tq), 1)

    qi = [qi_ref[:, h * IDX_DIM:(h + 1) * IDX_DIM].astype(BF16) for h in range(IDX_HEADS)]
    wi = [kw_ref[:, IDX_DIM + h:IDX_DIM + h + 1] for h in range(IDX_HEADS)]

    def score_block(j, _):
        cols = pl.ds(pl.multiple_of(j * tq, tq), tq)
        kt = kit_ref[0, :, cols]
        s = None
        for h in range(IDX_HEADS):
            sh = jnp.maximum(_dot(qi[h], kt) * IDX_DIM ** -0.5, 0.0) * wi[h]
            s = sh if s is None else s + sh
        causal = (j < i) | (col <= row)
        keys_ref[:, cols] = jnp.where(causal, _sortable_key(s), KEY_EXCLUDED)
        return 0
    lax.fori_loop(0, nblk, score_block, 0)

    _topk_mask(keys_ref, mask_ref, nblk, tq, tq, topk, n_groups=2)

    qh = [q_ref[:, h * HEAD_DIM:(h + 1) * HEAD_DIM] for h in range(N_HEADS)]
    m_ref[...] = jnp.full(m_ref.shape, MASKED, F32)
    acc_ref[...] = jnp.zeros(acc_ref.shape, F32)

    def attend(j, _, near):
        cols = pl.ds(pl.multiple_of(j * tq, tq), tq)
        masked = mask_ref[:, cols]
        for h in range(N_HEADS):
            s = _dot(qh[h], kt_ref[0, h, :, cols]) + masked
            if near:
                s = s + bias_ref[h, j - i + 1]
            m_old = m_ref[h]
            m_new = jnp.maximum(m_old, jnp.max(s, axis=1, keepdims=True))
            p = jnp.exp(s - m_new).astype(BF16)
            acc_ref[h] = jnp.exp(m_old - m_new) * acc_ref[h] + _dot_nt(p, vt_ref[0, h, :, cols])
            m_ref[h] = m_new
        return 0

    n_far = jnp.maximum(i - 1, 0)
    lax.fori_loop(0, n_far, functools.partial(attend, near=False), 0)
    lax.fori_loop(n_far, nblk, functools.partial(attend, near=True), 0)
    o_ref[...] = jnp.concatenate(
        [acc_ref[h, :, :HEAD_DIM] / acc_ref[h, :, HEAD_DIM:HEAD_DIM + 1] for h in range(N_HEADS)], axis=1)


def _prompt_attn(q, qi, kw, kt, vt, kit, bias_tiles, bsz, t, tq, topk):
    nq = t // tq
    row = lambda b, i: (b * nq + i, 0)
    return pl.pallas_call(
        functools.partial(_prompt_attn_body, tq=tq, topk=topk),
        grid=(bsz, nq),
        in_specs=[pl.BlockSpec((tq, ATT_WIDTH), row),
                  pl.BlockSpec((tq, IDX_HEADS * IDX_DIM), row),
                  pl.BlockSpec((tq, LANES), row),
                  pl.BlockSpec((1, N_HEADS, HEAD_DIM, t), lambda b, i: (b, 0, 0, 0), pipeline_mode=pl.Buffered(1)),
                  pl.BlockSpec((1, N_HEADS, V_ROWS, t), lambda b, i: (b, 0, 0, 0), pipeline_mode=pl.Buffered(1)),
                  pl.BlockSpec((1, IDX_DIM, t), lambda b, i: (b, 0, 0), pipeline_mode=pl.Buffered(1)),
                  pl.BlockSpec((N_HEADS, 2, tq, tq), lambda b, i: (0, 0, 0, 0), pipeline_mode=pl.Buffered(1))],
        out_specs=pl.BlockSpec((tq, ATT_WIDTH), row),
        out_shape=jax.ShapeDtypeStruct((bsz * t, ATT_WIDTH), F32),
        scratch_shapes=[pltpu.VMEM((tq, t), I32), pltpu.VMEM((tq, t), F32),
                        pltpu.VMEM((N_HEADS, tq, 1), F32), pltpu.VMEM((N_HEADS, tq, V_ROWS), F32)],
        compiler_params=_params("arbitrary", "arbitrary"),
        name="prompt_attn",
    )(q, qi, kw, kt, vt, kit, bias_tiles)


def _sample_scores_body(pt_ref, qi_ref, w_ref, kn_ref, *rest, n_pages):
    page_refs, key_ref = rest[:n_pages], rest[n_pages]
    qi = qi_ref[...].astype(BF16)
    w = w_ref[...]

    def weighted(s):
        return jnp.sum(jnp.maximum(s * IDX_DIM ** -0.5, 0.0) * w, axis=0, keepdims=True)

    for p in range(n_pages):
        s = _dot_nt(qi, page_refs[p][...].astype(BF16))
        key_ref[:, p * PAGE_SIZE:(p + 1) * PAGE_SIZE] = _sortable_key(weighted(s))
    s_self = jnp.sum(qi.astype(F32) * kn_ref[...].astype(BF16).astype(F32), axis=1, keepdims=True)
    lane = lax.broadcasted_iota(I32, (1, LANES), 1)
    key_ref[:, n_pages * PAGE_SIZE:] = jnp.where(lane == 0, _sortable_key(weighted(s_self)), KEY_EXCLUDED)


def _sample_scores(page_table, qi3, w3, kn3, cache_kidx):
    db, n_pages = page_table.shape
    n_cols = n_pages * PAGE_SIZE + LANES
    per = lambda s, pt: (s, 0, 0)
    page_specs = [pl.BlockSpec((None, PAGE_SIZE, IDX_DIM), functools.partial(lambda s, pt, p: (pt[s, p], 0, 0), p=p))
                  for p in range(n_pages)]
    return pl.pallas_call(
        functools.partial(_sample_scores_body, n_pages=n_pages),
        grid_spec=pltpu.PrefetchScalarGridSpec(
            num_scalar_prefetch=1, grid=(db,),
            in_specs=[pl.BlockSpec((None, IDX_HEADS, IDX_DIM), per),
                      pl.BlockSpec((None, IDX_HEADS, 1), per),
                      pl.BlockSpec((None, 1, IDX_DIM), per)] + page_specs,
            out_specs=pl.BlockSpec((None, 1, n_cols), per)),
        out_shape=jax.ShapeDtypeStruct((db, 1, n_cols), I32),
        compiler_params=_params("arbitrary"),
        name="sample_scores",
    )(page_table, qi3, w3, kn3, *([cache_kidx] * n_pages))


def _sample_select_body(key_ref, mask_ref, keys_scr, *, n_rows, n_cols, topk):
    keys_scr[...] = key_ref[...]
    _topk_mask(keys_scr, mask_ref, n_cols // LANES, LANES, n_rows, topk, n_groups=2)


def _sample_select(keys, topk):
    n_rows, n_cols = keys.shape
    return pl.pallas_call(
        functools.partial(_sample_select_body, n_rows=n_rows, n_cols=n_cols, topk=topk),
        out_shape=jax.ShapeDtypeStruct((n_rows, n_cols), F32),
        scratch_shapes=[pltpu.VMEM((n_rows, n_cols), I32)],
        compiler_params=pltpu.CompilerParams(vmem_limit_bytes=VMEM_LIMIT),
        name="sample_select",
    )(keys)


def _sample_attn_body(pt_ref, q_ref, kn_ref, vn_ref, mask_ref, bias_ref, *rest, n_pages):
    k_refs, v_refs, o_ref = rest[:n_pages], rest[n_pages:2 * n_pages], rest[2 * n_pages]
    past = n_pages * PAGE_SIZE
    head_of_lane = lax.broadcasted_iota(I32, (N_HEADS, ATT_WIDTH), 1) // HEAD_DIM
    own = head_of_lane == lax.broadcasted_iota(I32, (N_HEADS, ATT_WIDTH), 0)
    qbd = jnp.where(own, q_ref[...].astype(F32), 0.0).astype(BF16)
    bias = bias_ref[:, 0, :]
    mask = mask_ref[...]
    s = jnp.concatenate([_dot_nt(qbd, k_refs[p][...].astype(BF16)) for p in range(n_pages)], axis=1)
    s = s + bias[:, :past] + mask[:, :past]
    s_self = jnp.sum(qbd.astype(F32) * kn_ref[...].astype(BF16).astype(F32), axis=1, keepdims=True)
    s_self = s_self + bias[:, past:past + 1] + mask[:, past:past + 1]
    m = jnp.maximum(jnp.max(s, axis=1, keepdims=True), s_self)
    p_past = jnp.exp(s - m)
    p_self = jnp.exp(s_self - m)
    l = jnp.sum(p_past, axis=1, keepdims=True) + p_self
    acc = p_self.astype(BF16).astype(F32) * vn_ref[...].astype(BF16).astype(F32)
    p_past = p_past.astype(BF16)
    for p in range(n_pages):
        acc = acc + _dot(p_past[:, p * PAGE_SIZE:(p + 1) * PAGE_SIZE], v_refs[p][...].astype(BF16))
    o_ref[...] = jnp.sum(jnp.where(own, acc / l, 0.0), axis=0, keepdims=True)


def _sample_attn(page_table, q3, kn3, vn3, mask3, bias_row, cache_k, cache_v):
    db, n_pages = page_table.shape
    n_cols = n_pages * PAGE_SIZE + LANES
    per = lambda s, pt: (s, 0, 0)
    page_specs = [pl.BlockSpec((None, PAGE_SIZE, ATT_WIDTH), functools.partial(lambda s, pt, p: (pt[s, p], 0, 0), p=p))
                  for p in range(n_pages)]
    return pl.pallas_call(
        functools.partial(_sample_attn_body, n_pages=n_pages),
        grid_spec=pltpu.PrefetchScalarGridSpec(
            num_scalar_prefetch=1, grid=(db,),
            in_specs=[pl.BlockSpec((None, 1, ATT_WIDTH), per),
                      pl.BlockSpec((None, 1, ATT_WIDTH), per),
                      pl.BlockSpec((None, 1, ATT_WIDTH), per),
                      pl.BlockSpec((None, 1, n_cols), per),
                      pl.BlockSpec((N_HEADS, 1, n_cols), lambda s, pt: (0, 0, 0))] + page_specs + page_specs,
            out_specs=pl.BlockSpec((None, 1, ATT_WIDTH), per)),
        out_shape=jax.ShapeDtypeStruct((db, 1, ATT_WIDTH), F32),
        compiler_params=_params("arbitrary"),
        name="sample_attn",
    )(page_table, q3, kn3, vn3, mask3, bias_row, *([cache_k] * n_pages), *([cache_v] * n_pages))


def _cmul(ar, ai, br, bi):
    return ar * br - ai * bi, ar * bi + ai * br


def _ssm_prep_body(lre_f, lim_f, ldt_f, lre_r, lim_r, ldt_r, bre_ref, bim_ref, pre_ref, pim_ref, bbr_ref, bbi_ref,
                   *, chunk):
    def lam_bar(lre, lim, ldt):
        dt = jnp.exp(ldt)
        mag = jnp.exp(lre * dt)
        return mag * jnp.cos(lim * dt), mag * jnp.sin(lim * dt)

    lbr, lbi = lam_bar(lre_f[...], lim_f[...], ldt_f[...])
    pr = jnp.broadcast_to(lbr, (chunk, N_STATE))
    pi = jnp.broadcast_to(lbi, (chunk, N_STATE))
    row = lax.broadcasted_iota(I32, (chunk, N_STATE), 0)
    d = 1
    while d < chunk:
        sr = jnp.where(row >= d, pltpu.roll(pr, d, 0), 1.0)
        si = jnp.where(row >= d, pltpu.roll(pi, d, 0), 0.0)
        pr, pi = _cmul(pr, pi, sr, si)
        d *= 2
    pre_ref[...] = pr
    pim_ref[...] = pi

    lre, lim = lre_r[...], lim_r[...]
    lbr, lbi = lam_bar(lre, lim, ldt_r[...])
    den = lre * lre + lim * lim
    nr, ni = lbr - 1.0, lbi
    cr = (nr * lre + ni * lim) / den
    ci = (ni * lre - nr * lim) / den
    bbr, bbi = _cmul(cr, ci, bre_ref[...], bim_ref[...])
    bbr_ref[...] = bbr
    bbi_ref[...] = bbi


def _ssm_prep(lam_re, lam_im, log_dt, b_re, b_im, chunk):
    flat = lambda a: a.reshape(1, N_STATE)
    rep = lambda a: jnp.repeat(a, GROUP, axis=0)
    ldt = jnp.broadcast_to(log_dt[:, None], (N_GROUPS, STATE_DIM))
    bt = lambda b: jnp.swapaxes(b, 1, 2).reshape(SSM_WIDTH, STATE_DIM)
    return pl.pallas_call(
        functools.partial(_ssm_prep_body, chunk=chunk),
        out_shape=[jax.ShapeDtypeStruct((chunk, N_STATE), F32), jax.ShapeDtypeStruct((chunk, N_STATE), F32),
                   jax.ShapeDtypeStruct((SSM_WIDTH, STATE_DIM), F32), jax.ShapeDtypeStruct((SSM_WIDTH, STATE_DIM), F32)],
        compiler_params=pltpu.CompilerParams(vmem_limit_bytes=VMEM_LIMIT),
        name="ssm_prep",
    )(flat(lam_re), flat(lam_im), flat(ldt), rep(lam_re), rep(lam_im), rep(ldt), bt(b_re), bt(b_im))


def _block_diag(blocks):
    g, r, c = blocks.shape
    eye = jnp.eye(g, dtype=blocks.dtype)
    return (blocks[:, :, None, :] * eye[:, None, :, None]).reshape(g * r, g * c)


def _ssm_readout(u, xr, xi, wc_ref, dskip_ref, wglu_ref, bglu_ref):
    y = _dot(jnp.concatenate([xr, xi], axis=1).astype(BF16), wc_ref[...]) + dskip_ref[...] * u
    g = jax.nn.gelu(y)
    return g * jax.nn.sigmoid(_dot(g.astype(BF16), wglu_ref[...]) + bglu_ref[...])


def _ssm_scan_body(u_ref, x0r_ref, x0i_ref, pre_ref, pim_ref, wb_ref, wc_ref, dskip_ref, wglu_ref, bglu_ref,
                   y_ref, sr_ref, si_ref, cr_ref, ci_ref, *, chunk):
    c = pl.program_id(1)

    @pl.when(c == 0)
    def _():
        cr_ref[...] = x0r_ref[0]
        ci_ref[...] = x0i_ref[0]

    u = u_ref[...]
    bu = _dot(u.astype(BF16), wb_ref[...])
    xr, xi = bu[:, :N_STATE], bu[:, N_STATE:]
    row = lax.broadcasted_iota(I32, (chunk, N_STATE), 0)
    d = 1
    while d < chunk:
        sr = jnp.where(row >= d, pltpu.roll(xr, d, 0), 0.0)
        si = jnp.where(row >= d, pltpu.roll(xi, d, 0), 0.0)
        tr, ti = _cmul(pre_ref[d - 1:d, :], pim_ref[d - 1:d, :], sr, si)
        xr, xi = xr + tr, xi + ti
        d *= 2
    tr, ti = _cmul(pre_ref[...], pim_ref[...], cr_ref[...], ci_ref[...])
    xr, xi = xr + tr, xi + ti
    cr_ref[...] = xr[chunk - 1:chunk, :]
    ci_ref[...] = xi[chunk - 1:chunk, :]
    sr_ref[0] = xr[chunk - 1:chunk, :]
    si_ref[0] = xi[chunk - 1:chunk, :]
    y_ref[...] = _ssm_readout(u, xr, xi, wc_ref, dskip_ref, wglu_ref, bglu_ref)


def _ssm_scan(u, x0r, x0i, pre, pim, wb, wc, dskip, wglu, bglu, bsz, t, chunk):
    nc = t // chunk
    row = lambda b, c: (b * nc + c, 0)
    const = lambda b, c: (0, 0)
    state = pl.BlockSpec((1, 1, N_STATE), lambda b, c: (b, 0, 0))
    return pl.pallas_call(
        functools.partial(_ssm_scan_body, chunk=chunk),
        grid=(bsz, nc),
        in_specs=[pl.BlockSpec((chunk, SSM_WIDTH), row), state, state,
                  pl.BlockSpec((chunk, N_STATE), const), pl.BlockSpec((chunk, N_STATE), const),
                  pl.BlockSpec((SSM_WIDTH, 2 * N_STATE), const), pl.BlockSpec((2 * N_STATE, SSM_WIDTH), const),
                  pl.BlockSpec((1, SSM_WIDTH), const), pl.BlockSpec((SSM_WIDTH, SSM_WIDTH), const),
                  pl.BlockSpec((1, SSM_WIDTH), const)],
        out_specs=[pl.BlockSpec((chunk, SSM_WIDTH), row), state, state],
        out_shape=[jax.ShapeDtypeStruct((bsz * t, SSM_WIDTH), F32),
                   jax.ShapeDtypeStruct((bsz, 1, N_STATE), F32), jax.ShapeDtypeStruct((bsz, 1, N_STATE), F32)],
        scratch_shapes=[pltpu.VMEM((1, N_STATE), F32), pltpu.VMEM((1, N_STATE), F32)],
        compiler_params=_params("arbitrary", "arbitrary"),
        name="ssm_scan",
    )(u, x0r, x0i, pre, pim, wb, wc, dskip, wglu, bglu)


def _ssm_step_body(u_ref, x0r_ref, x0i_ref, pre_ref, pim_ref, wb_ref, wc_ref, dskip_ref, wglu_ref, bglu_ref,
                   y_ref, sr_ref, si_ref):
    u = u_ref[...]
    bu = _dot(u.astype(BF16), wb_ref[...])
    tr, ti = _cmul(pre_ref[0:1, :], pim_ref[0:1, :], x0r_ref[...], x0i_ref[...])
    xr, xi = bu[:, :N_STATE] + tr, bu[:, N_STATE:] + ti
    sr_ref[...] = xr
    si_ref[...] = xi
    y_ref[...] = _ssm_readout(u, xr, xi, wc_ref, dskip_ref, wglu_ref, bglu_ref)


def _ssm_step(u, x0r, x0i, pre, pim, wb, wc, dskip, wglu, bglu):
    n = u.shape[0]
    return pl.pallas_call(
        _ssm_step_body,
        out_shape=[jax.ShapeDtypeStruct((n, SSM_WIDTH), F32),
                   jax.ShapeDtypeStruct((n, N_STATE), F32), jax.ShapeDtypeStruct((n, N_STATE), F32)],
        compiler_params=pltpu.CompilerParams(vmem_limit_bytes=VMEM_LIMIT),
        name="ssm_step",
    )(u, x0r, x0i, pre, pim, wb, wc, dskip, wglu, bglu)


def _merge_body(x_ref, ya_ref, ys_ref, sga_ref, sgs_ref, wa_ref, ws_ref, wo_ref, gpost_ref, gpre_ref,
                x1_ref, h2_ref):
    merged = (sga_ref[...] * _dot(ya_ref[...].astype(BF16), wa_ref[...])
              + sgs_ref[...] * _dot(ys_ref[...].astype(BF16), ws_ref[...]))
    x1 = x_ref[...] + _rms(_dot(merged.astype(BF16), wo_ref[...]), gpost_ref[...])
    x1_ref[...] = x1
    h2_ref[...] = _rms(x1, gpre_ref[...]).astype(BF16)


def _merge(x2, ya, ys, sga, sgs, wa, ws, wo, gpost, gpre, tm):
    m = x2.shape[0]
    row = lambda i: (i, 0)
    const = lambda i: (0, 0)
    return pl.pallas_call(
        _merge_body,
        grid=(m // tm,),
        in_specs=[pl.BlockSpec((tm, D_MODEL), row), pl.BlockSpec((tm, ATT_WIDTH), row),
                  pl.BlockSpec((tm, SSM_WIDTH), row), pl.BlockSpec((tm, D_MODEL), row),
                  pl.BlockSpec((tm, D_MODEL), row),
                  pl.BlockSpec((ATT_WIDTH, D_MODEL), const), pl.BlockSpec((SSM_WIDTH, D_MODEL), const),
                  pl.BlockSpec((D_MODEL, D_MODEL), const), pl.BlockSpec((1, D_MODEL), const),
                  pl.BlockSpec((1, D_MODEL), const)],
        out_specs=[pl.BlockSpec((tm, D_MODEL), row), pl.BlockSpec((tm, D_MODEL), row)],
        out_shape=[jax.ShapeDtypeStruct((m, D_MODEL), F32), jax.ShapeDtypeStruct((m, D_MODEL), BF16)],
        compiler_params=_params("arbitrary"),
        name="merge",
    )(x2, ya, ys, sga, sgs, wa, ws, wo, gpost, gpre)


def _ffn_body(h_ref, x_ref, wua_ref, wub_ref, cwa_ref, cwb_ref, cba_ref, cbb_ref, wd_ref, g_ref, pa_ref, pb_ref,
              y_ref, oa_ref, ob_ref, f_ref, *carry, seq, tiles_per_seq):
    i, j = pl.program_id(0), pl.program_id(1)
    nj = pl.num_programs(1)
    h = h_ref[...]
    tm = h.shape[0]

    def conv(up, cw_ref, cb_ref, prev_ref, carry_ref, out_ref):
        if seq:
            @pl.when(i % tiles_per_seq == 0)
            def _():
                carry_ref[j, 0:CONV_W - 1, :] = prev_ref[0]
            row = lax.broadcasted_iota(I32, up.shape, 0)
            before1 = carry_ref[j, 1:2, :]
            before2 = carry_ref[j, 0:1, :]
            m1 = jnp.where(row == 0, before1, pltpu.roll(up, 1, 0))
            m2 = jnp.where(row == 0, before2, jnp.where(row == 1, before1, pltpu.roll(up, 2, 0)))
            tail = up[tm - (CONV_W - 1):, :]
            carry_ref[j, 0:CONV_W - 1, :] = tail
            out_ref[0] = tail
        else:
            m2, m1 = prev_ref[:, 0, :], prev_ref[:, 1, :]
            out_ref[...] = up
        return cb_ref[...] + m2 * cw_ref[0:1, :] + m1 * cw_ref[1:2, :] + up * cw_ref[2:3, :]

    ca, cb = carry if seq else (None, None)
    a = conv(_dot(h, wua_ref[...]), cwa_ref, cba_ref, pa_ref, ca, oa_ref)
    b = conv(_dot(h, wub_ref[...]), cwb_ref, cbb_ref, pb_ref, cb, ob_ref)
    part = _dot((jax.nn.gelu(a) * b).astype(BF16), wd_ref[...])

    @pl.when(j == 0)
    def _():
        f_ref[...] = part

    @pl.when(j > 0)
    def _():
        f_ref[...] += part

    @pl.when(j == nj - 1)
    def _():
        y_ref[...] = x_ref[...] + _rms(f_ref[...], g_ref[...])


def _ffn(h2, x1, w_up, conv_w, conv_b, w_down, g_post, conv_prev, bsz, t, tm, tf, seq):
    m = bsz * t
    nj = D_FF // tf
    row = lambda i, j: (i, 0)
    if seq:
        tiles_per_seq = t // tm
        prev_a = pl.BlockSpec((1, CONV_W - 1, tf), lambda i, j: (i // tiles_per_seq, 0, j))
        prev_b = pl.BlockSpec((1, CONV_W - 1, tf), lambda i, j: (i // tiles_per_seq, 0, j + nj))
        out_tail = pl.BlockSpec((1, CONV_W - 1, tf), lambda i, j: (i, 0, j))
        tail_shape = jax.ShapeDtypeStruct((m // tm, CONV_W - 1, D_FF), F32)
        scratch = [pltpu.VMEM((nj, 8, tf), F32), pltpu.VMEM((nj, 8, tf), F32)]
    else:
        tiles_per_seq = 1
        prev_a = pl.BlockSpec((tm, CONV_W - 1, tf), lambda i, j: (i, 0, j))
        prev_b = pl.BlockSpec((tm, CONV_W - 1, tf), lambda i, j: (i, 0, j + nj))
        out_tail = pl.BlockSpec((tm, tf), lambda i, j: (i, j))
        tail_shape = jax.ShapeDtypeStruct((m, D_FF), F32)
        scratch = []
    return pl.pallas_call(
        functools.partial(_ffn_body, seq=seq, tiles_per_seq=tiles_per_seq),
        grid=(m // tm, nj),
        in_specs=[pl.BlockSpec((tm, D_MODEL), row), pl.BlockSpec((tm, D_MODEL), row),
                  pl.BlockSpec((D_MODEL, tf), lambda i, j: (0, j)),
                  pl.BlockSpec((D_MODEL, tf), lambda i, j: (0, j + nj)),
                  pl.BlockSpec((CONV_W, tf), lambda i, j: (0, j)),
                  pl.BlockSpec((CONV_W, tf), lambda i, j: (0, j + nj)),
                  pl.BlockSpec((1, tf), lambda i, j: (0, j)),
                  pl.BlockSpec((1, tf), lambda i, j: (0, j + nj)),
                  pl.BlockSpec((tf, D_MODEL), lambda i, j: (j, 0)),
                  pl.BlockSpec((1, D_MODEL), lambda i, j: (0, 0)),
                  prev_a, prev_b],
        out_specs=[pl.BlockSpec((tm, D_MODEL), row), out_tail, out_tail],
        out_shape=[jax.ShapeDtypeStruct((m, D_MODEL), F32), tail_shape, tail_shape],
        scratch_shapes=[pltpu.VMEM((tm, D_MODEL), F32)] + scratch,
        compiler_params=_params("arbitrary", "arbitrary"),
        name="conv_ffn",
    )(h2, x1, w_up, w_up, conv_w, conv_w, conv_b, conv_b, w_down, g_post, conv_prev, conv_prev)


def _pack_w_in(w_in):
    points = np.cumsum(SPLITS)[:-1].tolist()
    wq, wk, wv, wqi, wki, wwi, wu, wga, wgs = jnp.split(w_in, points, axis=-1)
    pad = jnp.zeros((D_MODEL, LANES - IDX_DIM - IDX_HEADS), w_in.dtype)
    return jnp.concatenate([wq, wk, wv, wqi, wki, wwi, pad, wu, wga, wgs], axis=-1).astype(BF16)


def _layer_weights(lw, chunk):
    (w_in, g_pre_mix, g_post_mix, lam_re, lam_im, log_dt, b_re, b_im, c_re, c_im, d_skip,
     w_glu, b_glu, w_att_out, w_ssm_out, w_o, g_pre_ffn, g_post_ffn, w_up, conv_w, conv_b, w_down) = lw
    pre, pim, bbr, bbi = _ssm_prep(lam_re, lam_im, log_dt, b_re, b_im, chunk)
    wb = jnp.concatenate([_block_diag(bbr.reshape(N_GROUPS, GROUP, STATE_DIM)),
                          _block_diag(bbi.reshape(N_GROUPS, GROUP, STATE_DIM))], axis=1).astype(BF16)
    wc = jnp.concatenate([_block_diag(jnp.swapaxes(c_re, 1, 2)),
                          _block_diag(-jnp.swapaxes(c_im, 1, 2))], axis=0).astype(BF16)
    vec = lambda a: a.reshape(1, -1)
    return dict(
        w_in=_pack_w_in(w_in), g_pre_mix=vec(g_pre_mix), g_post_mix=vec(g_post_mix),
        pre=pre, pim=pim, wb=wb, wc=wc, d_skip=vec(d_skip), w_glu=w_glu.astype(BF16), b_glu=vec(b_glu),
        w_att_out=w_att_out.astype(BF16), w_ssm_out=w_ssm_out.astype(BF16), w_o=w_o.astype(BF16),
        g_pre_ffn=vec(g_pre_ffn), g_post_ffn=vec(g_post_ffn), w_up=w_up.astype(BF16), conv_w=conv_w,
        conv_b=vec(conv_b), w_down=w_down.astype(BF16))


def _prompt_layer(x, w, bias_tiles, tq, tm, chunk, tf):
    bsz, t, _ = x.shape
    x2 = x.reshape(bsz * t, D_MODEL)
    q, k, v, qi, kw, u, sga, sgs, kt, vt, kit = _in_proj(x2, w["g_pre_mix"], w["w_in"], bsz, t, tm, True)
    y_att = _prompt_attn(q, qi, kw, kt, vt, kit, bias_tiles, bsz, t, tq, min(TOPK_MAX, t // 4))
    zero_state = jnp.zeros((bsz, 1, N_STATE), F32)
    y_ssm, s_re, s_im = _ssm_scan(u, zero_state, zero_state, w["pre"], w["pim"], w["wb"], w["wc"], w["d_skip"],
                                  w["w_glu"], w["b_glu"], bsz, t, chunk)
    x1, h2 = _merge(x2, y_att, y_ssm, sga, sgs, w["w_att_out"], w["w_ssm_out"], w["w_o"],
                    w["g_post_mix"], w["g_pre_ffn"], tm)
    zero_conv = jnp.zeros((bsz, CONV_W - 1, 2 * D_FF), F32)
    y, tail_a, tail_b = _ffn(h2, x1, w["w_up"], w["conv_w"], w["conv_b"], w["w_down"], w["g_post_ffn"],
                             zero_conv, bsz, t, tm, tf, True)
    state = (k.reshape(bsz, t, N_HEADS, HEAD_DIM), v.reshape(bsz, t, N_HEADS, HEAD_DIM),
             kw[:, :IDX_DIM].reshape(bsz, t, IDX_DIM),
             s_re.reshape(bsz, N_GROUPS, STATE_DIM), s_im.reshape(bsz, N_GROUPS, STATE_DIM),
             jnp.concatenate([tail_a, tail_b], axis=-1)[t // tm - 1::t // tm])
    return y.reshape(bsz, t, D_MODEL), state


def _sample_layer(x, w, bias_row, cache_k, cache_v, cache_kidx, page_table, st_re, st_im, st_conv, tf):
    db, tq, _ = x.shape
    assert tq == 1, "the sample group is decoded one token per sequence"
    n_pool = cache_k.shape[0]
    past = page_table.shape[1] * PAGE_SIZE
    x2 = x.reshape(db, D_MODEL)
    q, k, v, qi, kw, u, sga, sgs = _in_proj(x2, w["g_pre_mix"], w["w_in"], 1, db, db, False)
    keys = _sample_scores(page_table, qi.reshape(db, IDX_HEADS, IDX_DIM),
                          kw[:, IDX_DIM:IDX_DIM + IDX_HEADS].reshape(db, IDX_HEADS, 1),
                          kw[:, :IDX_DIM].reshape(db, 1, IDX_DIM), cache_kidx)
    mask = _sample_select(keys.reshape(db, past + LANES), min(TOPK_MAX, (past + tq) // 4))
    y_att = _sample_attn(page_table, q.reshape(db, 1, ATT_WIDTH), k.reshape(db, 1, ATT_WIDTH),
                         v.reshape(db, 1, ATT_WIDTH), mask.reshape(db, 1, past + LANES), bias_row,
                         cache_k.reshape(n_pool, PAGE_SIZE, ATT_WIDTH), cache_v.reshape(n_pool, PAGE_SIZE, ATT_WIDTH))
    y_ssm, s_re, s_im = _ssm_step(u, st_re.reshape(db, N_STATE), st_im.reshape(db, N_STATE), w["pre"], w["pim"],
                                  w["wb"], w["wc"], w["d_skip"], w["w_glu"], w["b_glu"])
    x1, h2 = _merge(x2, y_att.reshape(db, ATT_WIDTH), y_ssm, sga, sgs, w["w_att_out"], w["w_ssm_out"], w["w_o"],
                    w["g_post_mix"], w["g_pre_ffn"], db)
    y, up_a, up_b = _ffn(h2, x1, w["w_up"], w["conv_w"], w["conv_b"], w["w_down"], w["g_post_ffn"],
                         st_conv, db, 1, db, tf, False)
    conv_new = jnp.concatenate([st_conv[:, 1:], jnp.concatenate([up_a, up_b], axis=-1)[:, None, :]], axis=1)
    state = (k.reshape(db, 1, N_HEADS, HEAD_DIM), v.reshape(db, 1, N_HEADS, HEAD_DIM),
             kw[:, :IDX_DIM].reshape(db, 1, IDX_DIM),
             s_re.reshape(db, N_GROUPS, STATE_DIM), s_im.reshape(db, N_GROUPS, STATE_DIM), conv_new)
    return y.reshape(db, 1, D_MODEL), state


def kernel(x_prompt, x_sample, cache_k, cache_v, cache_kidx, state_ssm_re, state_ssm_im, state_conv, page_table,
           rel_bias, w_in, g_pre_mix, g_post_mix, lam_re, lam_im, log_dt, b_re, b_im, c_re, c_im, d_skip, w_glu,
           b_glu, w_att_out, w_ssm_out, w_o, g_pre_ffn, g_post_ffn, w_up, conv_w, conv_b, w_down):
    depth = w_in.shape[0]
    t = x_prompt.shape[1]
    past = page_table.shape[1] * PAGE_SIZE
    tq = min(256, t)
    tm = min(512, t)
    chunk = min(256, t)
    tf = 512
    bias_tiles, bias_row = _bias_tiles(rel_bias, tq, past)
    y_p, y_s = x_prompt, x_sample
    outs_p, outs_s = [], []
    for l in range(depth):
        lw = (w_in[l], g_pre_mix[l], g_post_mix[l], lam_re[l], lam_im[l], log_dt[l], b_re[l], b_im[l],
              c_re[l], c_im[l], d_skip[l], w_glu[l], b_glu[l], w_att_out[l], w_ssm_out[l], w_o[l],
              g_pre_ffn[l], g_post_ffn[l], w_up[l], conv_w[l], conv_b[l], w_down[l])
        w = _layer_weights(lw, chunk)
        y_p, st_p = _prompt_layer(y_p, w, bias_tiles, tq, tm, chunk, tf)
        y_s, st_s = _sample_layer(y_s, w, bias_row, cache_k[l], cache_v[l], cache_kidx[l], page_table,
                                  state_ssm_re[l], state_ssm_im[l], state_conv[l], tf)
        outs_p.append(st_p)
        outs_s.append(st_s)
    k_p, v_p, ki_p, sr_p, si_p, cv_p = [jnp.stack(a) for a in zip(*outs_p)]
    k_s, v_s, ki_s, sr_s, si_s, cv_s = [jnp.stack(a) for a in zip(*outs_s)]
    return (y_p, y_s, k_p, v_p, ki_p, sr_p, si_p, cv_p, k_s, v_s, ki_s, sr_s, si_s, cv_s)
```

```python
import functools
import math

import numpy as np
import jax
import jax.numpy as jnp
from jax import lax
from jax.experimental import pallas as pl
from jax.experimental.pallas import tpu as pltpu

F32 = jnp.float32
BF16 = jnp.bfloat16
I32 = jnp.int32

D_MODEL = 1024
PAGE_SIZE = 128
N_HEADS = 8
HEAD_DIM = 64
ATT_WIDTH = N_HEADS * HEAD_DIM
IDX_HEADS = 4
IDX_DIM = 64
TOPK_MAX = 256
N_BUCKETS = 32
MAX_DISTANCE = 128
SSM_WIDTH = 512
GROUP = 16
N_GROUPS = SSM_WIDTH // GROUP
STATE_DIM = 64
N_STATE = N_GROUPS * STATE_DIM
D_FF = 4 * D_MODEL
CONV_W = 3
EPS = 1e-6
SPLITS = (ATT_WIDTH, ATT_WIDTH, ATT_WIDTH, IDX_HEADS * IDX_DIM, IDX_DIM, IDX_HEADS, SSM_WIDTH, D_MODEL, D_MODEL)

LANES = 128
KEY_EXCLUDED = -(2 ** 31)
CODE_NEVER = 2 ** 30
IDX_BITS = 14
BF16_SUBLANES = 16
V_ROWS = HEAD_DIM + BF16_SUBLANES
MASKED = -1e30
VMEM_LIMIT = 56 * 1024 * 1024

_C_Q, _C_K, _C_V = 0, ATT_WIDTH, 2 * ATT_WIDTH
_C_QI = 3 * ATT_WIDTH
_C_KW = _C_QI + IDX_HEADS * IDX_DIM
_C_U = _C_KW + LANES
_C_GA = _C_U + SSM_WIDTH
_C_GS = _C_GA + D_MODEL
_C_END = _C_GS + D_MODEL


def _params(*sem):
    return pltpu.CompilerParams(dimension_semantics=sem, vmem_limit_bytes=VMEM_LIMIT)


def _rms(x, g):
    inv = lax.rsqrt(jnp.mean(x * x, axis=-1, keepdims=True) + EPS)
    return (x * inv) * g


def _dot(a, b):
    return jnp.dot(a, b, preferred_element_type=F32)


def _dot_nt(a, b):
    return lax.dot_general(a, b, (((1,), (1,)), ((), ())), preferred_element_type=F32)


def _bucket_starts():
    n = np.arange(0, 1 << IDX_BITS, dtype=np.int32)
    max_exact = N_BUCKETS // 2
    nf = np.maximum(n, 1).astype(np.float32)
    large = max_exact + (np.log(nf / np.float32(max_exact)) / np.float32(math.log(MAX_DISTANCE / max_exact))
                         * np.float32(N_BUCKETS - max_exact)).astype(np.int32)
    large = np.minimum(large, N_BUCKETS - 1)
    bucket = np.where(n < max_exact, n, large)
    assert np.all(np.diff(bucket) >= 0)
    starts = [int(np.argmax(bucket >= b)) for b in range(N_BUCKETS)]
    assert all(bucket[s] == b for b, s in enumerate(starts))
    return starts


_BUCKET_START = _bucket_starts()


def _bias_tiles_body(rb_ref, tile_ref, row_ref, *, tq, past):
    h = pl.program_id(0)

    def bias_of(dist):
        val = jnp.full(dist.shape, rb_ref[0, h], F32)
        for b in range(1, N_BUCKETS):
            val = jnp.where(dist >= _BUCKET_START[b], rb_ref[b, h], val)
        return val

    key_off = lax.broadcasted_iota(I32, (tq, tq), 0)
    qry_off = lax.broadcasted_iota(I32, (tq, tq), 1)
    far = rb_ref[N_BUCKETS - 1, h]
    tile_ref[0, 1] = bias_of(qry_off - key_off) - far
    tile_ref[0, 0] = bias_of(qry_off - key_off + tq) - far
    lane = lax.broadcasted_iota(I32, (1, past + LANES), 1)
    row_ref[0] = bias_of(past - lane)


def _bias_tiles(rel_bias, tq, past):
    assert tq + 1 >= _BUCKET_START[N_BUCKETS - 1]
    return pl.pallas_call(
        functools.partial(_bias_tiles_body, tq=tq, past=past),
        grid=(N_HEADS,),
        in_specs=[pl.BlockSpec(memory_space=pltpu.SMEM)],
        out_specs=[pl.BlockSpec((1, 2, tq, tq), lambda h: (h, 0, 0, 0)),
                   pl.BlockSpec((1, 1, past + LANES), lambda h: (h, 0, 0))],
        out_shape=[jax.ShapeDtypeStruct((N_HEADS, 2, tq, tq), F32),
                   jax.ShapeDtypeStruct((N_HEADS, 1, past + LANES), F32)],
        compiler_params=_params("arbitrary"),
        name="bias_tiles",
    )(rel_bias)


def _in_proj_body(x_ref, g_ref, w_ref, k_ref, v_ref, kw_ref, u_ref, sga_ref, sgs_ref, *more_refs, seq):
    h = _rms(x_ref[...], g_ref[...]).astype(BF16)

    def proj(c0, c1):
        return _dot(h, w_ref[:, c0:c1])

    q = proj(_C_Q, _C_K) * HEAD_DIM ** -0.5
    k = proj(_C_K, _C_V)
    v = proj(_C_V, _C_QI)
    k_ref[...] = k
    v_ref[...] = v
    qi = proj(_C_QI, _C_KW)
    kw = proj(_C_KW, _C_U)
    lane = lax.broadcasted_iota(I32, kw.shape, 1)
    kw = jnp.where(lane >= IDX_DIM, kw * IDX_HEADS ** -0.5, kw)
    kw_ref[...] = kw
    u_ref[...] = proj(_C_U, _C_GA)
    sga_ref[...] = jax.nn.sigmoid(proj(_C_GA, _C_GS))
    sgs_ref[...] = jax.nn.sigmoid(proj(_C_GS, _C_END))
    if seq:
        qt_ref, qit_ref, kwt_ref, kh_ref, vt_ref, ki_ref = more_refs
        tm = k.shape[0]
        qt_ref[...] = q.T.reshape(N_HEADS, HEAD_DIM, tm).astype(BF16)
        qit_ref[...] = (qi * IDX_DIM ** -0.5).T.reshape(IDX_HEADS, IDX_DIM, tm).astype(BF16)
        kwt_ref[0] = kw.T
        for hd in range(N_HEADS):
            kh_ref[0, hd] = k[:, hd * HEAD_DIM:(hd + 1) * HEAD_DIM].astype(BF16)
        vt_ref[0, :, :HEAD_DIM, :] = v.T.reshape(N_HEADS, HEAD_DIM, tm).astype(BF16)
        vt_ref[0, :, HEAD_DIM:, :] = jnp.ones((N_HEADS, V_ROWS - HEAD_DIM, tm), BF16)
        ki_ref[0] = kw[:, :IDX_DIM].astype(BF16)
    else:
        q_ref, qi_ref = more_refs
        q_ref[...] = q.astype(BF16)
        qi_ref[...] = qi


def _in_proj(x2, g, w_packed, bsz, t, tm, seq):
    m = bsz * t
    nt = t // tm
    row = lambda b, i: (b * nt + i, 0)
    widths = (ATT_WIDTH, ATT_WIDTH, LANES, SSM_WIDTH, D_MODEL, D_MODEL)
    out_specs = [pl.BlockSpec((tm, w), row) for w in widths]
    out_shape = [jax.ShapeDtypeStruct((m, w), F32) for w in widths]
    if seq:
        col = lambda b, i: (0, 0, b * nt + i)
        out_specs += [pl.BlockSpec((N_HEADS, HEAD_DIM, tm), col),
                      pl.BlockSpec((IDX_HEADS, IDX_DIM, tm), col),
                      pl.BlockSpec((1, LANES, tm), lambda b, i: (b, 0, i)),
                      pl.BlockSpec((1, N_HEADS, tm, HEAD_DIM), lambda b, i: (b, 0, i, 0)),
                      pl.BlockSpec((1, N_HEADS, V_ROWS, tm), lambda b, i: (b, 0, 0, i)),
                      pl.BlockSpec((1, tm, IDX_DIM), lambda b, i: (b, i, 0))]
        out_shape += [jax.ShapeDtypeStruct((N_HEADS, HEAD_DIM, m), BF16),
                      jax.ShapeDtypeStruct((IDX_HEADS, IDX_DIM, m), BF16),
                      jax.ShapeDtypeStruct((bsz, LANES, t), F32),
                      jax.ShapeDtypeStruct((bsz, N_HEADS, t, HEAD_DIM), BF16),
                      jax.ShapeDtypeStruct((bsz, N_HEADS, V_ROWS, t), BF16),
                      jax.ShapeDtypeStruct((bsz, t, IDX_DIM), BF16)]
    else:
        out_specs += [pl.BlockSpec((tm, ATT_WIDTH), row), pl.BlockSpec((tm, IDX_HEADS * IDX_DIM), row)]
        out_shape += [jax.ShapeDtypeStruct((m, ATT_WIDTH), BF16),
                      jax.ShapeDtypeStruct((m, IDX_HEADS * IDX_DIM), F32)]
    return pl.pallas_call(
        functools.partial(_in_proj_body, seq=seq),
        grid=(bsz, nt),
        in_specs=[pl.BlockSpec((tm, D_MODEL), row),
                  pl.BlockSpec((1, D_MODEL), lambda b, i: (0, 0)),
                  pl.BlockSpec((D_MODEL, _C_END), lambda b, i: (0, 0))],
        out_specs=out_specs,
        out_shape=out_shape,
        compiler_params=_params("arbitrary", "arbitrary"),
        name="in_proj",
    )(x2, g, w_packed)


def _sortable_key(score):
    bits = pltpu.bitcast(score, I32)
    return jnp.where(bits < 0, bits ^ 0x7FFFFFFF, bits)


def _topk_mask(keys_ref, nblk, height, topk):
    n_lanes = keys_ref.shape[1]
    n_parts = height // 8
    assert n_parts * 8 == height and n_parts & (n_parts - 1) == 0

    def row_block(j):
        return pl.ds(pl.multiple_of(j * height, height), height)

    def count(pred, bound):
        def body(j, acc):
            hit = jnp.where(pred(keys_ref[row_block(j), :], bound), 1, 0)
            parts = [hit[r:r + 8] for r in range(0, height, 8)]
            while len(parts) > 1:
                parts = [a + b for a, b in zip(parts[::2], parts[1::2])]
            return acc + parts[0]
        acc = lax.fori_loop(0, nblk, body, jnp.zeros((8, n_lanes), I32))
        return jnp.sum(acc, axis=0, keepdims=True)

    zero = jnp.zeros((1, n_lanes), I32)

    prefix = jnp.where(count(lambda k, b: k >= b, zero) >= topk, 0, KEY_EXCLUDED).astype(I32)

    def value_bit(b, prefix):
        cand = prefix | jnp.left_shift(1, 30 - b)
        return jnp.where(count(lambda k, c: k >= c, cand) >= topk, cand, prefix)
    thr = lax.fori_loop(0, 31, value_bit, prefix)

    def encode(j, _):
        k = keys_ref[row_block(j), :]
        pos = j * height + lax.broadcasted_iota(I32, (height, n_lanes), 0)
        code = jnp.where(k > thr, -1, jnp.where(k == thr, pos, CODE_NEVER))
        keys_ref[row_block(j), :] = jnp.where(k == KEY_EXCLUDED, CODE_NEVER, code)
        return 0
    lax.fori_loop(0, nblk, encode, 0)

    def index_bit(b, bound):
        cand = bound | jnp.left_shift(1, IDX_BITS - 1 - b)
        return jnp.where(count(lambda c, x: c < x, cand) <= topk, cand, bound)
    bound = lax.fori_loop(0, IDX_BITS, index_bit, zero)

    def emit(j, _):
        mask = jnp.where(keys_ref[row_block(j), :] < bound, 0.0, MASKED).astype(F32)
        keys_ref[row_block(j), :] = pltpu.bitcast(mask, I32)
        return 0
    lax.fori_loop(0, nblk, emit, 0)


def _prompt_attn_body(qt_ref, qit_ref, wt_ref, kh_ref, vt_ref, ki_ref, bias_ref, o_ref, keys_ref, m_ref, acc_ref,
                      alpha_ref, p_ref, *, tq, topk):
    i = pl.program_id(1)
    nblk = i + 1
    key_off = lax.broadcasted_iota(I32, (tq, tq), 0)
    qry_off = lax.broadcasted_iota(I32, (tq, tq), 1)

    def key_block(j):
        return pl.ds(pl.multiple_of(j * tq, tq), tq)

    qit = [qit_ref[h] for h in range(IDX_HEADS)]
    wt = [wt_ref[0, h:h + 1, :] for h in range(IDX_HEADS)]

    def score_block(j, _, diagonal):
        ki = ki_ref[0, key_block(j), :]
        s = None
        for h in range(IDX_HEADS):
            sh = jnp.maximum(_dot(ki, qit[h]), 0.0) * wt[h]
            s = sh if s is None else s + sh
        key = _sortable_key(s)
        if diagonal:
            key = jnp.where(key_off <= qry_off, key, KEY_EXCLUDED)
        keys_ref[key_block(j), :] = key
        return 0
    lax.fori_loop(0, i, functools.partial(score_block, diagonal=False), 0)
    score_block(i, 0, diagonal=True)

    _topk_mask(keys_ref, nblk, tq, topk)

    m_ref[...] = jnp.full(m_ref.shape, MASKED, F32)
    acc_ref[...] = jnp.zeros(acc_ref.shape, F32)
    p_ref[...] = jnp.zeros(p_ref.shape, BF16)
    alpha_ref[...] = jnp.ones(alpha_ref.shape, F32)

    def accumulate(j):
        for h in range(N_HEADS):
            acc_ref[h] = alpha_ref[h] * acc_ref[h] + _dot(vt_ref[0, h, :, key_block(j)], p_ref[h])

    def attend(j, _, near):
        accumulate(jnp.maximum(j - 1, 0))
        masked = pltpu.bitcast(keys_ref[key_block(j), :], F32)
        for h in range(N_HEADS):
            s = _dot(kh_ref[0, h, key_block(j), :], qt_ref[h]) + masked
            if near:
                s = s + bias_ref[h, j - i + 1]
            m_old = m_ref[h]
            m_new = jnp.maximum(m_old, jnp.max(s, axis=0, keepdims=True))
            p_ref[h] = jnp.exp(s - m_new).astype(BF16)
            alpha_ref[h] = jnp.exp(m_old - m_new)
            m_ref[h] = m_new
        return 0

    n_far = jnp.maximum(i - 1, 0)
    lax.fori_loop(0, n_far, functools.partial(attend, near=False), 0)
    lax.fori_loop(n_far, nblk, functools.partial(attend, near=True), 0)
    accumulate(i)
    out_t = jnp.concatenate(
        [acc_ref[h, :HEAD_DIM, :] / acc_ref[h, HEAD_DIM:HEAD_DIM + 1, :] for h in range(N_HEADS)], axis=0)
    o_ref[...] = out_t.T


def _prompt_attn(qt, qit, kwt, kh, vt, ki, bias_tiles, bsz, t, tq, topk):
    nq = t // tq
    col = lambda b, i: (0, 0, b * nq + i)
    whole = dict(pipeline_mode=pl.Buffered(1))
    return pl.pallas_call(
        functools.partial(_prompt_attn_body, tq=tq, topk=topk),
        grid=(bsz, nq),
        in_specs=[pl.BlockSpec((N_HEADS, HEAD_DIM, tq), col),
                  pl.BlockSpec((IDX_HEADS, IDX_DIM, tq), col),
                  pl.BlockSpec((1, 8, tq), lambda b, i: (b, IDX_DIM // 8, i)),
                  pl.BlockSpec((1, N_HEADS, t, HEAD_DIM), lambda b, i: (b, 0, 0, 0), **whole),
                  pl.BlockSpec((1, N_HEADS, V_ROWS, t), lambda b, i: (b, 0, 0, 0), **whole),
                  pl.BlockSpec((1, t, IDX_DIM), lambda b, i: (b, 0, 0), **whole),
                  pl.BlockSpec((N_HEADS, 2, tq, tq), lambda b, i: (0, 0, 0, 0), **whole)],
        out_specs=pl.BlockSpec((tq, ATT_WIDTH), lambda b, i: (b * nq + i, 0)),
        out_shape=jax.ShapeDtypeStruct((bsz * t, ATT_WIDTH), F32),
        scratch_shapes=[pltpu.VMEM((t, tq), I32),
                        pltpu.VMEM((N_HEADS, 1, tq), F32), pltpu.VMEM((N_HEADS, V_ROWS, tq), F32),
                        pltpu.VMEM((N_HEADS, 1, tq), F32), pltpu.VMEM((N_HEADS, tq, tq), BF16)],
        compiler_params=_params("arbitrary", "arbitrary"),
        name="prompt_attn",
    )(qt, qit, kwt, kh, vt, ki, bias_tiles)


def _sample_scores_body(pt_ref, qi_ref, w_ref, kn_ref, *rest, n_pages):
    page_refs, key_ref = rest[:n_pages], rest[n_pages]
    qi = qi_ref[...].astype(BF16)
    w = w_ref[...]

    def weighted(s):
        return jnp.sum(jnp.maximum(s * IDX_DIM ** -0.5, 0.0) * w, axis=0, keepdims=True)

    for p in range(n_pages):
        s = _dot_nt(qi, page_refs[p][...].astype(BF16))
        key_ref[:, p * PAGE_SIZE:(p + 1) * PAGE_SIZE] = _sortable_key(weighted(s))
    s_self = jnp.sum(qi.astype(F32) * kn_ref[...].astype(BF16).astype(F32), axis=1, keepdims=True)
    lane = lax.broadcasted_iota(I32, (1, LANES), 1)
    key_ref[:, n_pages * PAGE_SIZE:] = jnp.where(lane == 0, _sortable_key(weighted(s_self)), KEY_EXCLUDED)


def _sample_scores(page_table, qi3, w3, kn3, cache_kidx):
    db, n_pages = page_table.shape
    n_cols = n_pages * PAGE_SIZE + LANES
    per = lambda s, pt: (s, 0, 0)
    page_specs = [pl.BlockSpec((None, PAGE_SIZE, IDX_DIM), functools.partial(lambda s, pt, p: (pt[s, p], 0, 0), p=p))
                  for p in range(n_pages)]
    return pl.pallas_call(
        functools.partial(_sample_scores_body, n_pages=n_pages),
        grid_spec=pltpu.PrefetchScalarGridSpec(
            num_scalar_prefetch=1, grid=(db,),
            in_specs=[pl.BlockSpec((None, IDX_HEADS, IDX_DIM), per),
                      pl.BlockSpec((None, IDX_HEADS, 1), per),
                      pl.BlockSpec((None, 1, IDX_DIM), per)] + page_specs,
            out_specs=pl.BlockSpec((None, 1, n_cols), per)),
        out_shape=jax.ShapeDtypeStruct((db, 1, n_cols), I32),
        compiler_params=_params("arbitrary"),
        name="sample_scores",
    )(page_table, qi3, w3, kn3, *([cache_kidx] * n_pages))


def _sample_select_body(key_ref, mask_ref, keys_scr, *, n_cols, topk):
    keys_scr[...] = key_ref[...].T
    _topk_mask(keys_scr, n_cols // LANES, LANES, topk)
    mask_ref[...] = pltpu.bitcast(keys_scr[...], F32).T


def _sample_select(keys, topk):
    n_rows, n_cols = keys.shape
    return pl.pallas_call(
        functools.partial(_sample_select_body, n_cols=n_cols, topk=topk),
        out_shape=jax.ShapeDtypeStruct((n_rows, n_cols), F32),
        scratch_shapes=[pltpu.VMEM((n_cols, n_rows), I32)],
        compiler_params=pltpu.CompilerParams(vmem_limit_bytes=VMEM_LIMIT),
        name="sample_select",
    )(keys)


def _sample_attn_body(pt_ref, q_ref, kn_ref, vn_ref, mask_ref, bias_ref, *rest, n_pages):
    k_refs, v_refs, o_ref = rest[:n_pages], rest[n_pages:2 * n_pages], rest[2 * n_pages]
    past = n_pages * PAGE_SIZE
    head_of_lane = lax.broadcasted_iota(I32, (N_HEADS, ATT_WIDTH), 1) // HEAD_DIM
    own = head_of_lane == lax.broadcasted_iota(I32, (N_HEADS, ATT_WIDTH), 0)
    qbd = jnp.where(own, q_ref[...].astype(F32), 0.0).astype(BF16)
    bias = bias_ref[:, 0, :]
    mask = mask_ref[...]
    s = jnp.concatenate([_dot_nt(qbd, k_refs[p][...].astype(BF16)) for p in range(n_pages)], axis=1)
    s = s + bias[:, :past] + mask[:, :past]
    s_self = jnp.sum(qbd.astype(F32) * kn_ref[...].astype(BF16).astype(F32), axis=1, keepdims=True)
    s_self = s_self + bias[:, past:past + 1] + mask[:, past:past + 1]
    m = jnp.maximum(jnp.max(s, axis=1, keepdims=True), s_self)
    p_past = jnp.exp(s - m)
    p_self = jnp.exp(s_self - m)
    l = jnp.sum(p_past, axis=1, keepdims=True) + p_self
    acc = p_self.astype(BF16).astype(F32) * vn_ref[...].astype(BF16).astype(F32)
    p_past = p_past.astype(BF16)
    for p in range(n_pages):
        acc = acc + _dot(p_past[:, p * PAGE_SIZE:(p + 1) * PAGE_SIZE], v_refs[p][...].astype(BF16))
    o_ref[...] = jnp.sum(jnp.where(own, acc / l, 0.0), axis=0, keepdims=True)


def _sample_attn(page_table, q3, kn3, vn3, mask3, bias_row, cache_k, cache_v):
    db, n_pages = page_table.shape
    n_cols = n_pages * PAGE_SIZE + LANES
    per = lambda s, pt: (s, 0, 0)
    page_specs = [pl.BlockSpec((None, PAGE_SIZE, ATT_WIDTH), functools.partial(lambda s, pt, p: (pt[s, p], 0, 0), p=p))
                  for p in range(n_pages)]
    return pl.pallas_call(
        functools.partial(_sample_attn_body, n_pages=n_pages),
        grid_spec=pltpu.PrefetchScalarGridSpec(
            num_scalar_prefetch=1, grid=(db,),
            in_specs=[pl.BlockSpec((None, 1, ATT_WIDTH), per),
                      pl.BlockSpec((None, 1, ATT_WIDTH), per),
                      pl.BlockSpec((None, 1, ATT_WIDTH), per),
                      pl.BlockSpec((None, 1, n_cols), per),
                      pl.BlockSpec((N_HEADS, 1, n_cols), lambda s, pt: (0, 0, 0))] + page_specs + page_specs,
            out_specs=pl.BlockSpec((None, 1, ATT_WIDTH), per)),
        out_shape=jax.ShapeDtypeStruct((db, 1, ATT_WIDTH), F32),
        compiler_params=_params("arbitrary"),
        name="sample_attn",
    )(page_table, q3, kn3, vn3, mask3, bias_row, *([cache_k] * n_pages), *([cache_v] * n_pages))


def _cmul(ar, ai, br, bi):
    return ar * br - ai * bi, ar * bi + ai * br


def _ssm_prep_body(lre_f, lim_f, ldt_f, lre_r, lim_r, ldt_r, bre_ref, bim_ref, pre_ref, pim_ref, bbr_ref, bbi_ref,
                   *, chunk):
    def lam_bar(lre, lim, ldt):
        dt = jnp.exp(ldt)
        mag = jnp.exp(lre * dt)
        return mag * jnp.cos(lim * dt), mag * jnp.sin(lim * dt)

    lbr, lbi = lam_bar(lre_f[...], lim_f[...], ldt_f[...])
    pr = jnp.broadcast_to(lbr, (chunk, N_STATE))
    pi = jnp.broadcast_to(lbi, (chunk, N_STATE))
    row = lax.broadcasted_iota(I32, (chunk, N_STATE), 0)
    d = 1
    while d < chunk:
        sr = jnp.where(row >= d, pltpu.roll(pr, d, 0), 1.0)
        si = jnp.where(row >= d, pltpu.roll(pi, d, 0), 0.0)
        pr, pi = _cmul(pr, pi, sr, si)
        d *= 2
    pre_ref[...] = pr
    pim_ref[...] = pi

    lre, lim = lre_r[...], lim_r[...]
    lbr, lbi = lam_bar(lre, lim, ldt_r[...])
    den = lre * lre + lim * lim
    nr, ni = lbr - 1.0, lbi
    cr = (nr * lre + ni * lim) / den
    ci = (ni * lre - nr * lim) / den
    bbr, bbi = _cmul(cr, ci, bre_ref[...], bim_ref[...])
    bbr_ref[...] = bbr
    bbi_ref[...] = bbi


def _ssm_prep(lam_re, lam_im, log_dt, b_re, b_im, chunk):
    flat = lambda a: a.reshape(1, N_STATE)
    rep = lambda a: jnp.repeat(a, GROUP, axis=0)
    ldt = jnp.broadcast_to(log_dt[:, None], (N_GROUPS, STATE_DIM))
    bt = lambda b: jnp.swapaxes(b, 1, 2).reshape(SSM_WIDTH, STATE_DIM)
    return pl.pallas_call(
        functools.partial(_ssm_prep_body, chunk=chunk),
        out_shape=[jax.ShapeDtypeStruct((chunk, N_STATE), F32), jax.ShapeDtypeStruct((chunk, N_STATE), F32),
                   jax.ShapeDtypeStruct((SSM_WIDTH, STATE_DIM), F32), jax.ShapeDtypeStruct((SSM_WIDTH, STATE_DIM), F32)],
        compiler_params=pltpu.CompilerParams(vmem_limit_bytes=VMEM_LIMIT),
        name="ssm_prep",
    )(flat(lam_re), flat(lam_im), flat(ldt), rep(lam_re), rep(lam_im), rep(ldt), bt(b_re), bt(b_im))


def _block_diag(blocks):
    g, r, c = blocks.shape
    eye = jnp.eye(g, dtype=blocks.dtype)
    return (blocks[:, :, None, :] * eye[:, None, :, None]).reshape(g * r, g * c)


def _ssm_readout(u, xr, xi, wc_ref, dskip_ref, wglu_ref, bglu_ref):
    y = _dot(jnp.concatenate([xr, xi], axis=1).astype(BF16), wc_ref[...]) + dskip_ref[...] * u
    g = jax.nn.gelu(y)
    return g * jax.nn.sigmoid(_dot(g.astype(BF16), wglu_ref[...]) + bglu_ref[...])


def _ssm_scan_body(u_ref, x0r_ref, x0i_ref, pre_ref, pim_ref, wb_ref, wc_ref, dskip_ref, wglu_ref, bglu_ref,
                   y_ref, sr_ref, si_ref, cr_ref, ci_ref, *, chunk):
    c = pl.program_id(1)

    @pl.when(c == 0)
    def _():
        cr_ref[...] = x0r_ref[0]
        ci_ref[...] = x0i_ref[0]

    u = u_ref[...]
    bu = _dot(u.astype(BF16), wb_ref[...])
    xr, xi = bu[:, :N_STATE], bu[:, N_STATE:]
    row = lax.broadcasted_iota(I32, (chunk, N_STATE), 0)
    d = 1
    while d < chunk:
        sr = jnp.where(row >= d, pltpu.roll(xr, d, 0), 0.0)
        si = jnp.where(row >= d, pltpu.roll(xi, d, 0), 0.0)
        tr, ti = _cmul(pre_ref[d - 1:d, :], pim_ref[d - 1:d, :], sr, si)
        xr, xi = xr + tr, xi + ti
        d *= 2
    tr, ti = _cmul(pre_ref[...], pim_ref[...], cr_ref[...], ci_ref[...])
    xr, xi = xr + tr, xi + ti
    cr_ref[...] = xr[chunk - 1:chunk, :]
    ci_ref[...] = xi[chunk - 1:chunk, :]
    sr_ref[0] = xr[chunk - 1:chunk, :]
    si_ref[0] = xi[chunk - 1:chunk, :]
    y_ref[...] = _ssm_readout(u, xr, xi, wc_ref, dskip_ref, wglu_ref, bglu_ref)


def _ssm_scan(u, x0r, x0i, pre, pim, wb, wc, dskip, wglu, bglu, bsz, t, chunk):
    nc = t // chunk
    row = lambda b, c: (b * nc + c, 0)
    const = lambda b, c: (0, 0)
    state = pl.BlockSpec((1, 1, N_STATE), lambda b, c: (b, 0, 0))
    return pl.pallas_call(
        functools.partial(_ssm_scan_body, chunk=chunk),
        grid=(bsz, nc),
        in_specs=[pl.BlockSpec((chunk, SSM_WIDTH), row), state, state,
                  pl.BlockSpec((chunk, N_STATE), const), pl.BlockSpec((chunk, N_STATE), const),
                  pl.BlockSpec((SSM_WIDTH, 2 * N_STATE), const), pl.BlockSpec((2 * N_STATE, SSM_WIDTH), const),
                  pl.BlockSpec((1, SSM_WIDTH), const), pl.BlockSpec((SSM_WIDTH, SSM_WIDTH), const),
                  pl.BlockSpec((1, SSM_WIDTH), const)],
        out_specs=[pl.BlockSpec((chunk, SSM_WIDTH), row), state, state],
        out_shape=[jax.ShapeDtypeStruct((bsz * t, SSM_WIDTH), F32),
                   jax.ShapeDtypeStruct((bsz, 1, N_STATE), F32), jax.ShapeDtypeStruct((bsz, 1, N_STATE), F32)],
        scratch_shapes=[pltpu.VMEM((1, N_STATE), F32), pltpu.VMEM((1, N_STATE), F32)],
        compiler_params=_params("arbitrary", "arbitrary"),
        name="ssm_scan",
    )(u, x0r, x0i, pre, pim, wb, wc, dskip, wglu, bglu)


def _ssm_step_body(u_ref, x0r_ref, x0i_ref, pre_ref, pim_ref, wb_ref, wc_ref, dskip_ref, wglu_ref, bglu_ref,
                   y_ref, sr_ref, si_ref):
    u = u_ref[...]
    bu = _dot(u.astype(BF16), wb_ref[...])
    tr, ti = _cmul(pre_ref[0:1, :], pim_ref[0:1, :], x0r_ref[...], x0i_ref[...])
    xr, xi = bu[:, :N_STATE] + tr, bu[:, N_STATE:] + ti
    sr_ref[...] = xr
    si_ref[...] = xi
    y_ref[...] = _ssm_readout(u, xr, xi, wc_ref, dskip_ref, wglu_ref, bglu_ref)


def _ssm_step(u, x0r, x0i, pre, pim, wb, wc, dskip, wglu, bglu):
    n = u.shape[0]
    return pl.pallas_call(
        _ssm_step_body,
        out_shape=[jax.ShapeDtypeStruct((n, SSM_WIDTH), F32),
                   jax.ShapeDtypeStruct((n, N_STATE), F32), jax.ShapeDtypeStruct((n, N_STATE), F32)],
        compiler_params=pltpu.CompilerParams(vmem_limit_bytes=VMEM_LIMIT),
        name="ssm_step",
    )(u, x0r, x0i, pre, pim, wb, wc, dskip, wglu, bglu)


def _merge_body(x_ref, ya_ref, ys_ref, sga_ref, sgs_ref, wa_ref, ws_ref, wo_ref, gpost_ref, gpre_ref,
                x1_ref, h2_ref):
    merged = (sga_ref[...] * _dot(ya_ref[...].astype(BF16), wa_ref[...])
              + sgs_ref[...] * _dot(ys_ref[...].astype(BF16), ws_ref[...]))
    x1 = x_ref[...] + _rms(_dot(merged.astype(BF16), wo_ref[...]), gpost_ref[...])
    x1_ref[...] = x1
    h2_ref[...] = _rms(x1, gpre_ref[...]).astype(BF16)


def _merge(x2, ya, ys, sga, sgs, wa, ws, wo, gpost, gpre, tm):
    m = x2.shape[0]
    row = lambda i: (i, 0)
    const = lambda i: (0, 0)
    return pl.pallas_call(
        _merge_body,
        grid=(m // tm,),
        in_specs=[pl.BlockSpec((tm, D_MODEL), row), pl.BlockSpec((tm, ATT_WIDTH), row),
                  pl.BlockSpec((tm, SSM_WIDTH), row), pl.BlockSpec((tm, D_MODEL), row),
                  pl.BlockSpec((tm, D_MODEL), row),
                  pl.BlockSpec((ATT_WIDTH, D_MODEL), const), pl.BlockSpec((SSM_WIDTH, D_MODEL), const),
                  pl.BlockSpec((D_MODEL, D_MODEL), const), pl.BlockSpec((1, D_MODEL), const),
                  pl.BlockSpec((1, D_MODEL), const)],
        out_specs=[pl.BlockSpec((tm, D_MODEL), row), pl.BlockSpec((tm, D_MODEL), row)],
        out_shape=[jax.ShapeDtypeStruct((m, D_MODEL), F32), jax.ShapeDtypeStruct((m, D_MODEL), BF16)],
        compiler_params=_params("arbitrary"),
        name="merge",
    )(x2, ya, ys, sga, sgs, wa, ws, wo, gpost, gpre)


def _ffn_body(h_ref, x_ref, wua_ref, wub_ref, cwa_ref, cwb_ref, cba_ref, cbb_ref, wd_ref, g_ref, pa_ref, pb_ref,
              y_ref, oa_ref, ob_ref, f_ref, *carry, seq, tiles_per_seq):
    i, j = pl.program_id(0), pl.program_id(1)
    nj = pl.num_programs(1)
    h = h_ref[...]
    tm = h.shape[0]

    def conv(up, cw_ref, cb_ref, prev_ref, carry_ref, out_ref):
        if seq:
            @pl.when(i % tiles_per_seq == 0)
            def _():
                carry_ref[j, 0:CONV_W - 1, :] = prev_ref[0]
            row = lax.broadcasted_iota(I32, up.shape, 0)
            before1 = carry_ref[j, 1:2, :]
            before2 = carry_ref[j, 0:1, :]
            m1 = jnp.where(row == 0, before1, pltpu.roll(up, 1, 0))
            m2 = jnp.where(row == 0, before2, jnp.where(row == 1, before1, pltpu.roll(up, 2, 0)))
            tail = up[tm - (CONV_W - 1):, :]
            carry_ref[j, 0:CONV_W - 1, :] = tail
            out_ref[0] = tail
        else:
            m2, m1 = prev_ref[:, 0, :], prev_ref[:, 1, :]
            out_ref[...] = up
        return cb_ref[...] + m2 * cw_ref[0:1, :] + m1 * cw_ref[1:2, :] + up * cw_ref[2:3, :]

    ca, cb = carry if seq else (None, None)
    a = conv(_dot(h, wua_ref[...]), cwa_ref, cba_ref, pa_ref, ca, oa_ref)
    b = conv(_dot(h, wub_ref[...]), cwb_ref, cbb_ref, pb_ref, cb, ob_ref)
    part = _dot((jax.nn.gelu(a) * b).astype(BF16), wd_ref[...])

    @pl.when(j == 0)
    def _():
        f_ref[...] = part

    @pl.when(j > 0)
    def _():
        f_ref[...] += part

    @pl.when(j == nj - 1)
    def _():
        y_ref[...] = x_ref[...] + _rms(f_ref[...], g_ref[...])


def _ffn(h2, x1, w_up, conv_w, conv_b, w_down, g_post, conv_prev, bsz, t, tm, tf, seq):
    m = bsz * t
    nj = D_FF // tf
    row = lambda i, j: (i, 0)
    if seq:
        tiles_per_seq = t // tm
        prev_a = pl.BlockSpec((1, CONV_W - 1, tf), lambda i, j: (i // tiles_per_seq, 0, j))
        prev_b = pl.BlockSpec((1, CONV_W - 1, tf), lambda i, j: (i // tiles_per_seq, 0, j + nj))
        out_tail = pl.BlockSpec((1, CONV_W - 1, tf), lambda i, j: (i, 0, j))
        tail_shape = jax.ShapeDtypeStruct((m // tm, CONV_W - 1, D_FF), F32)
        scratch = [pltpu.VMEM((nj, 8, tf), F32), pltpu.VMEM((nj, 8, tf), F32)]
    else:
        tiles_per_seq = 1
        prev_a = pl.BlockSpec((tm, CONV_W - 1, tf), lambda i, j: (i, 0, j))
        prev_b = pl.BlockSpec((tm, CONV_W - 1, tf), lambda i, j: (i, 0, j + nj))
        out_tail = pl.BlockSpec((tm, tf), lambda i, j: (i, j))
        tail_shape = jax.ShapeDtypeStruct((m, D_FF), F32)
        scratch = []
    return pl.pallas_call(
        functools.partial(_ffn_body, seq=seq, tiles_per_seq=tiles_per_seq),
        grid=(m // tm, nj),
        in_specs=[pl.BlockSpec((tm, D_MODEL), row), pl.BlockSpec((tm, D_MODEL), row),
                  pl.BlockSpec((D_MODEL, tf), lambda i, j: (0, j)),
                  pl.BlockSpec((D_MODEL, tf), lambda i, j: (0, j + nj)),
                  pl.BlockSpec((CONV_W, tf), lambda i, j: (0, j)),
                  pl.BlockSpec((CONV_W, tf), lambda i, j: (0, j + nj)),
                  pl.BlockSpec((1, tf), lambda i, j: (0, j)),
                  pl.BlockSpec((1, tf), lambda i, j: (0, j + nj)),
                  pl.BlockSpec((tf, D_MODEL), lambda i, j: (j, 0)),
                  pl.BlockSpec((1, D_MODEL), lambda i, j: (0, 0)),
                  prev_a, prev_b],
        out_specs=[pl.BlockSpec((tm, D_MODEL), row), out_tail, out_tail],
        out_shape=[jax.ShapeDtypeStruct((m, D_MODEL), F32), tail_shape, tail_shape],
        scratch_shapes=[pltpu.VMEM((tm, D_MODEL), F32)] + scratch,
        compiler_params=_params("arbitrary", "arbitrary"),
        name="conv_ffn",
    )(h2, x1, w_up, w_up, conv_w, conv_w, conv_b, conv_b, w_down, g_post, conv_prev, conv_prev)


def _pack_w_in(w_in):
    points = np.cumsum(SPLITS)[:-1].tolist()
    wq, wk, wv, wqi, wki, wwi, wu, wga, wgs = jnp.split(w_in, points, axis=-1)
    pad = jnp.zeros((D_MODEL, LANES - IDX_DIM - IDX_HEADS), w_in.dtype)
    return jnp.concatenate([wq, wk, wv, wqi, wki, wwi, pad, wu, wga, wgs], axis=-1).astype(BF16)


def _layer_weights(lw, chunk):
    (w_in, g_pre_mix, g_post_mix, lam_re, lam_im, log_dt, b_re, b_im, c_re, c_im, d_skip,
     w_glu, b_glu, w_att_out, w_ssm_out, w_o, g_pre_ffn, g_post_ffn, w_up, conv_w, conv_b, w_down) = lw
    pre, pim, bbr, bbi = _ssm_prep(lam_re, lam_im, log_dt, b_re, b_im, chunk)
    wb = jnp.concatenate([_block_diag(bbr.reshape(N_GROUPS, GROUP, STATE_DIM)),
                          _block_diag(bbi.reshape(N_GROUPS, GROUP, STATE_DIM))], axis=1).astype(BF16)
    wc = jnp.concatenate([_block_diag(jnp.swapaxes(c_re, 1, 2)),
                          _block_diag(-jnp.swapaxes(c_im, 1, 2))], axis=0).astype(BF16)
    vec = lambda a: a.reshape(1, -1)
    return dict(
        w_in=_pack_w_in(w_in), g_pre_mix=vec(g_pre_mix), g_post_mix=vec(g_post_mix),
        pre=pre, pim=pim, wb=wb, wc=wc, d_skip=vec(d_skip), w_glu=w_glu.astype(BF16), b_glu=vec(b_glu),
        w_att_out=w_att_out.astype(BF16), w_ssm_out=w_ssm_out.astype(BF16), w_o=w_o.astype(BF16),
        g_pre_ffn=vec(g_pre_ffn), g_post_ffn=vec(g_post_ffn), w_up=w_up.astype(BF16), conv_w=conv_w,
        conv_b=vec(conv_b), w_down=w_down.astype(BF16))


def _prompt_layer(x, w, bias_tiles, tq, tm, chunk, tf):
    bsz, t, _ = x.shape
    x2 = x.reshape(bsz * t, D_MODEL)
    k, v, kw, u, sga, sgs, qt, qit, kwt, kh, vt, ki = _in_proj(x2, w["g_pre_mix"], w["w_in"], bsz, t, tm, True)
    y_att = _prompt_attn(qt, qit, kwt, kh, vt, ki, bias_tiles, bsz, t, tq, min(TOPK_MAX, t // 4))
    zero_state = jnp.zeros((bsz, 1, N_STATE), F32)
    y_ssm, s_re, s_im = _ssm_scan(u, zero_state, zero_state, w["pre"], w["pim"], w["wb"], w["wc"], w["d_skip"],
                                  w["w_glu"], w["b_glu"], bsz, t, chunk)
    x1, h2 = _merge(x2, y_att, y_ssm, sga, sgs, w["w_att_out"], w["w_ssm_out"], w["w_o"],
                    w["g_post_mix"], w["g_pre_ffn"], tm)
    zero_conv = jnp.zeros((bsz, CONV_W - 1, 2 * D_FF), F32)
    y, tail_a, tail_b = _ffn(h2, x1, w["w_up"], w["conv_w"], w["conv_b"], w["w_down"], w["g_post_ffn"],
                             zero_conv, bsz, t, tm, tf, True)
    state = (k.reshape(bsz, t, N_HEADS, HEAD_DIM), v.reshape(bsz, t, N_HEADS, HEAD_DIM),
             kw[:, :IDX_DIM].reshape(bsz, t, IDX_DIM),
             s_re.reshape(bsz, N_GROUPS, STATE_DIM), s_im.reshape(bsz, N_GROUPS, STATE_DIM),
             jnp.concatenate([tail_a, tail_b], axis=-1)[t // tm - 1::t // tm])
    return y.reshape(bsz, t, D_MODEL), state


def _sample_layer(x, w, bias_row, cache_k, cache_v, cache_kidx, page_table, st_re, st_im, st_conv, tf):
    db, tq, _ = x.shape
    assert tq == 1, "the sample group is decoded one token per sequence"
    n_pool = cache_k.shape[0]
    past = page_table.shape[1] * PAGE_SIZE
    x2 = x.reshape(db, D_MODEL)
    k, v, kw, u, sga, sgs, q, qi = _in_proj(x2, w["g_pre_mix"], w["w_in"], 1, db, db, False)
    keys = _sample_scores(page_table, qi.reshape(db, IDX_HEADS, IDX_DIM),
                          kw[:, IDX_DIM:IDX_DIM + IDX_HEADS].reshape(db, IDX_HEADS, 1),
                          kw[:, :IDX_DIM].reshape(db, 1, IDX_DIM), cache_kidx)
    mask = _sample_select(keys.reshape(db, past + LANES), min(TOPK_MAX, (past + tq) // 4))
    y_att = _sample_attn(page_table, q.reshape(db, 1, ATT_WIDTH), k.reshape(db, 1, ATT_WIDTH),
                         v.reshape(db, 1, ATT_WIDTH), mask.reshape(db, 1, past + LANES), bias_row,
                         cache_k.reshape(n_pool, PAGE_SIZE, ATT_WIDTH), cache_v.reshape(n_pool, PAGE_SIZE, ATT_WIDTH))
    y_ssm, s_re, s_im = _ssm_step(u, st_re.reshape(db, N_STATE), st_im.reshape(db, N_STATE), w["pre"], w["pim"],
                                  w["wb"], w["wc"], w["d_skip"], w["w_glu"], w["b_glu"])
    x1, h2 = _merge(x2, y_att.reshape(db, ATT_WIDTH), y_ssm, sga, sgs, w["w_att_out"], w["w_ssm_out"], w["w_o"],
                    w["g_post_mix"], w["g_pre_ffn"], db)
    y, up_a, up_b = _ffn(h2, x1, w["w_up"], w["conv_w"], w["conv_b"], w["w_down"], w["g_post_ffn"],
                         st_conv, db, 1, db, tf, False)
    conv_new = jnp.concatenate([st_conv[:, 1:], jnp.concatenate([up_a, up_b], axis=-1)[:, None, :]], axis=1)
    state = (k.reshape(db, 1, N_HEADS, HEAD_DIM), v.reshape(db, 1, N_HEADS, HEAD_DIM),
             kw[:, :IDX_DIM].reshape(db, 1, IDX_DIM),
             s_re.reshape(db, N_GROUPS, STATE_DIM), s_im.reshape(db, N_GROUPS, STATE_DIM), conv_new)
    return y.reshape(db, 1, D_MODEL), state


def kernel(x_prompt, x_sample, cache_k, cache_v, cache_kidx, state_ssm_re, state_ssm_im, state_conv, page_table,
           rel_bias, w_in, g_pre_mix, g_post_mix, lam_re, lam_im, log_dt, b_re, b_im, c_re, c_im, d_skip, w_glu,
           b_glu, w_att_out, w_ssm_out, w_o, g_pre_ffn, g_post_ffn, w_up, conv_w, conv_b, w_down):
    depth = w_in.shape[0]
    t = x_prompt.shape[1]
    past = page_table.shape[1] * PAGE_SIZE
    tq = min(256, t)
    tm = min(512, t)
    chunk = min(256, t)
    tf = 512
    bias_tiles, bias_row = _bias_tiles(rel_bias, tq, past)
    y_p, y_s = x_prompt, x_sample
    outs_p, outs_s = [], []
    for l in range(depth):
        lw = (w_in[l], g_pre_mix[l], g_post_mix[l], lam_re[l], lam_im[l], log_dt[l], b_re[l], b_im[l],
              c_re[l], c_im[l], d_skip[l], w_glu[l], b_glu[l], w_att_out[l], w_ssm_out[l], w_o[l],
              g_pre_ffn[l], g_post_ffn[l], w_up[l], conv_w[l], conv_b[l], w_down[l])
        w = _layer_weights(lw, chunk)
        y_p, st_p = _prompt_layer(y_p, w, bias_tiles, tq, tm, chunk, tf)
        y_s, st_s = _sample_layer(y_s, w, bias_row, cache_k[l], cache_v[l], cache_kidx[l], page_table,
                                  state_ssm_re[l], state_ssm_im[l], state_conv[l], tf)
        outs_p.append(st_p)
        outs_s.append(st_s)
    k_p, v_p, ki_p, sr_p, si_p, cv_p = [jnp.stack(a) for a in zip(*outs_p)]
    k_s, v_s, ki_s, sr_s, si_s, cv_s = [jnp.stack(a) for a in zip(*outs_s)]
    return (y_p, y_s, k_p, v_p, ki_p, sr_p, si_p, cv_p, k_s, v_s, ki_s, sr_s, si_s, cv_s)
```

```python
import functools
import math

import numpy as np
import jax
import jax.numpy as jnp
from jax import lax
from jax.experimental import pallas as pl
from jax.experimental.pallas import tpu as pltpu

F32 = jnp.float32
BF16 = jnp.bfloat16
I32 = jnp.int32

D_MODEL = 1024
PAGE_SIZE = 128
N_HEADS = 8
HEAD_DIM = 64
ATT_WIDTH = N_HEADS * HEAD_DIM
IDX_HEADS = 4
IDX_DIM = 64
TOPK_MAX = 256
N_BUCKETS = 32
MAX_DISTANCE = 128
SSM_WIDTH = 512
GROUP = 16
N_GROUPS = SSM_WIDTH // GROUP
STATE_DIM = 64
N_STATE = N_GROUPS * STATE_DIM
D_FF = 4 * D_MODEL
CONV_W = 3
EPS = 1e-6
SPLITS = (ATT_WIDTH, ATT_WIDTH, ATT_WIDTH, IDX_HEADS * IDX_DIM, IDX_DIM, IDX_HEADS, SSM_WIDTH, D_MODEL, D_MODEL)

LANES = 128
KEY_EXCLUDED = -(2 ** 31)
CODE_NEVER = 2 ** 30
IDX_BITS = 14
BF16_SUBLANES = 16
V_ROWS = HEAD_DIM + BF16_SUBLANES
SCAN_GROUP = 8
MASKED = -1e30
VMEM_LIMIT = 56 * 1024 * 1024

_C_Q, _C_K, _C_V = 0, ATT_WIDTH, 2 * ATT_WIDTH
_C_QI = 3 * ATT_WIDTH
_C_KW = _C_QI + IDX_HEADS * IDX_DIM
_C_U = _C_KW + LANES
_C_GA = _C_U + SSM_WIDTH
_C_GS = _C_GA + D_MODEL
_C_END = _C_GS + D_MODEL


def _params(*sem):
    return pltpu.CompilerParams(dimension_semantics=sem, vmem_limit_bytes=VMEM_LIMIT)


def _rms(x, g):
    inv = lax.rsqrt(jnp.mean(x * x, axis=-1, keepdims=True) + EPS)
    return (x * inv) * g


def _dot(a, b):
    return jnp.dot(a, b, preferred_element_type=F32)


def _dot_nt(a, b):
    return lax.dot_general(a, b, (((1,), (1,)), ((), ())), preferred_element_type=F32)


def _bucket_starts():
    n = np.arange(0, 1 << IDX_BITS, dtype=np.int32)
    max_exact = N_BUCKETS // 2
    nf = np.maximum(n, 1).astype(np.float32)
    large = max_exact + (np.log(nf / np.float32(max_exact)) / np.float32(math.log(MAX_DISTANCE / max_exact))
                         * np.float32(N_BUCKETS - max_exact)).astype(np.int32)
    large = np.minimum(large, N_BUCKETS - 1)
    bucket = np.where(n < max_exact, n, large)
    assert np.all(np.diff(bucket) >= 0)
    starts = [int(np.argmax(bucket >= b)) for b in range(N_BUCKETS)]
    assert all(bucket[s] == b for b, s in enumerate(starts))
    return starts


_BUCKET_START = _bucket_starts()


def _bias_tiles_body(rb_ref, tile_ref, row_ref, *, tq, past):
    h = pl.program_id(0)

    def bias_of(dist):
        val = jnp.full(dist.shape, rb_ref[0, h], F32)
        for b in range(1, N_BUCKETS):
            val = jnp.where(dist >= _BUCKET_START[b], rb_ref[b, h], val)
        return val

    key_off = lax.broadcasted_iota(I32, (tq, tq), 0)
    qry_off = lax.broadcasted_iota(I32, (tq, tq), 1)
    far = rb_ref[N_BUCKETS - 1, h]
    tile_ref[0, 1] = bias_of(qry_off - key_off) - far
    tile_ref[0, 0] = bias_of(qry_off - key_off + tq) - far
    lane = lax.broadcasted_iota(I32, (1, past * N_HEADS + LANES), 1)
    key = lane // N_HEADS
    own_head = (lane % N_HEADS == h) | (key >= past)
    row_ref[0] = jnp.where(own_head, bias_of(past - jnp.minimum(key, past)), MASKED)


def _bias_tiles(rel_bias, tq, past):
    assert tq + 1 >= _BUCKET_START[N_BUCKETS - 1]
    return pl.pallas_call(
        functools.partial(_bias_tiles_body, tq=tq, past=past),
        grid=(N_HEADS,),
        in_specs=[pl.BlockSpec(memory_space=pltpu.SMEM)],
        out_specs=[pl.BlockSpec((1, 2, tq, tq), lambda h: (h, 0, 0, 0)),
                   pl.BlockSpec((1, 1, past * N_HEADS + LANES), lambda h: (h, 0, 0))],
        out_shape=[jax.ShapeDtypeStruct((N_HEADS, 2, tq, tq), F32),
                   jax.ShapeDtypeStruct((N_HEADS, 1, past * N_HEADS + LANES), F32)],
        compiler_params=_params("arbitrary"),
        name="bias_tiles",
    )(rel_bias)


def _in_proj_body(x_ref, g_ref, w_ref, k_ref, v_ref, kw_ref, u_ref, sga_ref, sgs_ref, *more_refs, seq):
    h = _rms(x_ref[...], g_ref[...]).astype(BF16)

    def proj(c0, c1):
        return _dot(h, w_ref[:, c0:c1])

    q = proj(_C_Q, _C_K) * HEAD_DIM ** -0.5
    k = proj(_C_K, _C_V)
    v = proj(_C_V, _C_QI)
    k_ref[...] = k
    v_ref[...] = v
    qi = proj(_C_QI, _C_KW)
    kw = proj(_C_KW, _C_U)
    lane = lax.broadcasted_iota(I32, kw.shape, 1)
    kw = jnp.where(lane >= IDX_DIM, kw * IDX_HEADS ** -0.5, kw)
    kw_ref[...] = kw
    u_ref[...] = proj(_C_U, _C_GA)
    sga_ref[...] = jax.nn.sigmoid(proj(_C_GA, _C_GS))
    sgs_ref[...] = jax.nn.sigmoid(proj(_C_GS, _C_END))
    if seq:
        qt_ref, qit_ref, kwt_ref, kh_ref, vt_ref, ki_ref = more_refs
        tm = k.shape[0]
        qt_ref[...] = q.T.reshape(N_HEADS, HEAD_DIM, tm).astype(BF16)
        qit_ref[...] = (qi * IDX_DIM ** -0.5).T.reshape(IDX_HEADS, IDX_DIM, tm).astype(BF16)
        kwt_ref[0] = kw.T
        for hd in range(N_HEADS):
            kh_ref[0, hd] = k[:, hd * HEAD_DIM:(hd + 1) * HEAD_DIM].astype(BF16)
        vt_ref[0, :, :HEAD_DIM, :] = v.T.reshape(N_HEADS, HEAD_DIM, tm).astype(BF16)
        vt_ref[0, :, HEAD_DIM:, :] = jnp.ones((N_HEADS, V_ROWS - HEAD_DIM, tm), BF16)
        ki_ref[0] = kw[:, :IDX_DIM].astype(BF16)
    else:
        q_ref, qi_ref = more_refs
        q_ref[...] = q.astype(BF16)
        qi_ref[...] = qi


def _in_proj(x2, g, w_packed, bsz, t, tm, seq):
    m = bsz * t
    nt = t // tm
    row = lambda b, i: (b * nt + i, 0)
    widths = (ATT_WIDTH, ATT_WIDTH, LANES, SSM_WIDTH, D_MODEL, D_MODEL)
    out_specs = [pl.BlockSpec((tm, w), row) for w in widths]
    out_shape = [jax.ShapeDtypeStruct((m, w), F32) for w in widths]
    if seq:
        col = lambda b, i: (0, 0, b * nt + i)
        out_specs += [pl.BlockSpec((N_HEADS, HEAD_DIM, tm), col),
                      pl.BlockSpec((IDX_HEADS, IDX_DIM, tm), col),
                      pl.BlockSpec((1, LANES, tm), lambda b, i: (b, 0, i)),
                      pl.BlockSpec((1, N_HEADS, tm, HEAD_DIM), lambda b, i: (b, 0, i, 0)),
                      pl.BlockSpec((1, N_HEADS, V_ROWS, tm), lambda b, i: (b, 0, 0, i)),
                      pl.BlockSpec((1, tm, IDX_DIM), lambda b, i: (b, i, 0))]
        out_shape += [jax.ShapeDtypeStruct((N_HEADS, HEAD_DIM, m), BF16),
                      jax.ShapeDtypeStruct((IDX_HEADS, IDX_DIM, m), BF16),
                      jax.ShapeDtypeStruct((bsz, LANES, t), F32),
                      jax.ShapeDtypeStruct((bsz, N_HEADS, t, HEAD_DIM), BF16),
                      jax.ShapeDtypeStruct((bsz, N_HEADS, V_ROWS, t), BF16),
                      jax.ShapeDtypeStruct((bsz, t, IDX_DIM), BF16)]
    else:
        out_specs += [pl.BlockSpec((tm, ATT_WIDTH), row), pl.BlockSpec((tm, IDX_HEADS * IDX_DIM), row)]
        out_shape += [jax.ShapeDtypeStruct((m, ATT_WIDTH), BF16),
                      jax.ShapeDtypeStruct((m, IDX_HEADS * IDX_DIM), F32)]
    return pl.pallas_call(
        functools.partial(_in_proj_body, seq=seq),
        grid=(bsz, nt),
        in_specs=[pl.BlockSpec((tm, D_MODEL), row),
                  pl.BlockSpec((1, D_MODEL), lambda b, i: (0, 0)),
                  pl.BlockSpec((D_MODEL, _C_END), lambda b, i: (0, 0))],
        out_specs=out_specs,
        out_shape=out_shape,
        compiler_params=_params("arbitrary", "arbitrary"),
        name="in_proj",
    )(x2, g, w_packed)


def _sortable_key(score):
    bits = pltpu.bitcast(score, I32)
    return jnp.where(bits < 0, bits ^ 0x7FFFFFFF, bits)


def _topk_mask(keys_ref, nblk, height, topk):
    n_lanes = keys_ref.shape[1]
    n_parts = height // 8
    assert n_parts * 8 == height and n_parts & (n_parts - 1) == 0

    def row_block(j):
        return pl.ds(pl.multiple_of(j * height, height), height)

    def count(pred, bound):
        def body(j, acc):
            hit = jnp.where(pred(keys_ref[row_block(j), :], bound), 1, 0)
            parts = [hit[r:r + 8] for r in range(0, height, 8)]
            while len(parts) > 1:
                parts = [a + b for a, b in zip(parts[::2], parts[1::2])]
            return acc + parts[0]
        acc = lax.fori_loop(0, nblk, body, jnp.zeros((8, n_lanes), I32))
        return jnp.sum(acc, axis=0, keepdims=True)

    zero = jnp.zeros((1, n_lanes), I32)

    prefix = jnp.where(count(lambda k, b: k >= b, zero) >= topk, 0, KEY_EXCLUDED).astype(I32)

    def value_bit(b, prefix):
        cand = prefix | jnp.left_shift(1, 30 - b)
        return jnp.where(count(lambda k, c: k >= c, cand) >= topk, cand, prefix)
    thr = lax.fori_loop(0, 31, value_bit, prefix)

    def encode(j, _):
        k = keys_ref[row_block(j), :]
        pos = j * height + lax.broadcasted_iota(I32, (height, n_lanes), 0)
        code = jnp.where(k > thr, -1, jnp.where(k == thr, pos, CODE_NEVER))
        keys_ref[row_block(j), :] = jnp.where(k == KEY_EXCLUDED, CODE_NEVER, code)
        return 0
    lax.fori_loop(0, nblk, encode, 0)

    def index_bit(b, bound):
        cand = bound | jnp.left_shift(1, IDX_BITS - 1 - b)
        return jnp.where(count(lambda c, x: c < x, cand) <= topk, cand, bound)
    bound = lax.fori_loop(0, IDX_BITS, index_bit, zero)

    def emit(j, _):
        mask = jnp.where(keys_ref[row_block(j), :] < bound, 0.0, MASKED).astype(F32)
        keys_ref[row_block(j), :] = pltpu.bitcast(mask, I32)
        return 0
    lax.fori_loop(0, nblk, emit, 0)


def _prompt_attn_body(qt_ref, qit_ref, wt_ref, kh_ref, vt_ref, ki_ref, bias_ref, o_ref, keys_ref, m_ref, acc_ref,
                      alpha_ref, p_ref, *, tq, topk):
    i = pl.program_id(1)
    nblk = i + 1
    key_off = lax.broadcasted_iota(I32, (tq, tq), 0)
    qry_off = lax.broadcasted_iota(I32, (tq, tq), 1)

    def key_block(j):
        return pl.ds(pl.multiple_of(j * tq, tq), tq)

    qit = [qit_ref[h] for h in range(IDX_HEADS)]
    wt = [wt_ref[0, h:h + 1, :] for h in range(IDX_HEADS)]

    def score_block(j, _, diagonal):
        ki = ki_ref[0, key_block(j), :]
        s = None
        for h in range(IDX_HEADS):
            sh = jnp.maximum(_dot(ki, qit[h]), 0.0) * wt[h]
            s = sh if s is None else s + sh
        key = _sortable_key(s)
        if diagonal:
            key = jnp.where(key_off <= qry_off, key, KEY_EXCLUDED)
        keys_ref[key_block(j), :] = key
        return 0
    lax.fori_loop(0, i, functools.partial(score_block, diagonal=False), 0)
    score_block(i, 0, diagonal=True)

    _topk_mask(keys_ref, nblk, tq, topk)

    m_ref[...] = jnp.full(m_ref.shape, MASKED, F32)
    acc_ref[...] = jnp.zeros(acc_ref.shape, F32)
    p_ref[...] = jnp.zeros(p_ref.shape, BF16)
    alpha_ref[...] = jnp.ones(alpha_ref.shape, F32)

    def accumulate(j):
        for h in range(N_HEADS):
            acc_ref[h] = alpha_ref[h] * acc_ref[h] + _dot(vt_ref[0, h, :, key_block(j)], p_ref[h])

    def attend(j, _, near):
        accumulate(jnp.maximum(j - 1, 0))
        masked = pltpu.bitcast(keys_ref[key_block(j), :], F32)
        for h in range(N_HEADS):
            s = _dot(kh_ref[0, h, key_block(j), :], qt_ref[h]) + masked
            if near:
                s = s + bias_ref[h, j - i + 1]
            m_old = m_ref[h]
            m_new = jnp.maximum(m_old, jnp.max(s, axis=0, keepdims=True))
            p_ref[h] = jnp.exp(s - m_new).astype(BF16)
            alpha_ref[h] = jnp.exp(m_old - m_new)
            m_ref[h] = m_new
        return 0

    n_far = jnp.maximum(i - 1, 0)
    lax.fori_loop(0, n_far, functools.partial(attend, near=False), 0)
    lax.fori_loop(n_far, nblk, functools.partial(attend, near=True), 0)
    accumulate(i)
    out_t = jnp.concatenate(
        [acc_ref[h, :HEAD_DIM, :] / acc_ref[h, HEAD_DIM:HEAD_DIM + 1, :] for h in range(N_HEADS)], axis=0)
    o_ref[...] = out_t.T


def _prompt_attn(qt, qit, kwt, kh, vt, ki, bias_tiles, bsz, t, tq, topk):
    nq = t // tq
    col = lambda b, i: (0, 0, b * nq + i)
    whole = dict(pipeline_mode=pl.Buffered(1))
    return pl.pallas_call(
        functools.partial(_prompt_attn_body, tq=tq, topk=topk),
        grid=(bsz, nq),
        in_specs=[pl.BlockSpec((N_HEADS, HEAD_DIM, tq), col),
                  pl.BlockSpec((IDX_HEADS, IDX_DIM, tq), col),
                  pl.BlockSpec((1, 8, tq), lambda b, i: (b, IDX_DIM // 8, i)),
                  pl.BlockSpec((1, N_HEADS, t, HEAD_DIM), lambda b, i: (b, 0, 0, 0), **whole),
                  pl.BlockSpec((1, N_HEADS, V_ROWS, t), lambda b, i: (b, 0, 0, 0), **whole),
                  pl.BlockSpec((1, t, IDX_DIM), lambda b, i: (b, 0, 0), **whole),
                  pl.BlockSpec((N_HEADS, 2, tq, tq), lambda b, i: (0, 0, 0, 0), **whole)],
        out_specs=pl.BlockSpec((tq, ATT_WIDTH), lambda b, i: (b * nq + i, 0)),
        out_shape=jax.ShapeDtypeStruct((bsz * t, ATT_WIDTH), F32),
        scratch_shapes=[pltpu.VMEM((t, tq), I32),
                        pltpu.VMEM((N_HEADS, 1, tq), F32), pltpu.VMEM((N_HEADS, V_ROWS, tq), F32),
                        pltpu.VMEM((N_HEADS, 1, tq), F32), pltpu.VMEM((N_HEADS, tq, tq), BF16)],
        compiler_params=_params("arbitrary", "arbitrary"),
        name="prompt_attn",
    )(qt, qit, kwt, kh, vt, ki, bias_tiles)


def _sample_scores_body(pt_ref, qi_ref, w_ref, kn_ref, *rest, n_pages):
    page_refs, key_ref = rest[:n_pages], rest[n_pages]
    qi = qi_ref[...].astype(BF16)
    w = w_ref[...]

    def weighted(s):
        return jnp.sum(jnp.maximum(s * IDX_DIM ** -0.5, 0.0) * w, axis=0, keepdims=True)

    for p in range(n_pages):
        s = _dot_nt(qi, page_refs[p][...].astype(BF16))
        key_ref[:, p * PAGE_SIZE:(p + 1) * PAGE_SIZE] = _sortable_key(weighted(s))
    s_self = jnp.sum(qi.astype(F32) * kn_ref[...].astype(BF16).astype(F32), axis=1, keepdims=True)
    lane = lax.broadcasted_iota(I32, (1, LANES), 1)
    key_ref[:, n_pages * PAGE_SIZE:] = jnp.where(lane == 0, _sortable_key(weighted(s_self)), KEY_EXCLUDED)


def _sample_scores(page_table, qi3, w3, kn3, cache_kidx):
    db, n_pages = page_table.shape
    n_cols = n_pages * PAGE_SIZE + LANES
    per = lambda s, pt: (s, 0, 0)
    page_specs = [pl.BlockSpec((None, PAGE_SIZE, IDX_DIM), functools.partial(lambda s, pt, p: (pt[s, p], 0, 0), p=p))
                  for p in range(n_pages)]
    return pl.pallas_call(
        functools.partial(_sample_scores_body, n_pages=n_pages),
        grid_spec=pltpu.PrefetchScalarGridSpec(
            num_scalar_prefetch=1, grid=(db,),
            in_specs=[pl.BlockSpec((None, IDX_HEADS, IDX_DIM), per),
                      pl.BlockSpec((None, IDX_HEADS, 1), per),
                      pl.BlockSpec((None, 1, IDX_DIM), per)] + page_specs,
            out_specs=pl.BlockSpec((None, 1, n_cols), per)),
        out_shape=jax.ShapeDtypeStruct((db, 1, n_cols), I32),
        compiler_params=_params("arbitrary"),
        name="sample_scores",
    )(page_table, qi3, w3, kn3, *([cache_kidx] * n_pages))


def _sample_select_body(key_ref, mask_ref, keys_scr, *, n_cols, topk):
    keys_scr[...] = key_ref[...].T
    _topk_mask(keys_scr, n_cols // LANES, LANES, topk)
    mask_ref[...] = pltpu.bitcast(keys_scr[...], F32).T


def _sample_select(keys, topk):
    n_rows, n_cols = keys.shape
    return pl.pallas_call(
        functools.partial(_sample_select_body, n_cols=n_cols, topk=topk),
        out_shape=jax.ShapeDtypeStruct((n_rows, n_cols), F32),
        scratch_shapes=[pltpu.VMEM((n_cols, n_rows), I32)],
        compiler_params=pltpu.CompilerParams(vmem_limit_bytes=VMEM_LIMIT),
        name="sample_select",
    )(keys)


def _sample_attn_body(pt_ref, q_ref, kn_ref, vn_ref, mask_ref, bias_ref, *rest, n_pages):
    k_refs, v_refs, o_ref = rest[:n_pages], rest[n_pages:2 * n_pages], rest[2 * n_pages]
    rows = PAGE_SIZE * N_HEADS
    n_past = n_pages * rows
    q = q_ref[...]
    bias = bias_ref[:, 0, :]
    mask = mask_ref[...]
    s = jnp.concatenate([_dot_nt(q, k_refs[p][...].astype(BF16)) for p in range(n_pages)], axis=1)
    s = s + bias[:, :n_past] + mask[:, :n_past]
    s_self = jnp.sum(q.astype(F32) * kn_ref[...].astype(BF16).astype(F32), axis=1, keepdims=True)
    s_self = s_self + bias[:, n_past:n_past + 1] + mask[:, n_past:n_past + 1]
    m = jnp.maximum(jnp.max(s, axis=1, keepdims=True), s_self)
    p_past = jnp.exp(s - m)
    p_self = jnp.exp(s_self - m)
    l = jnp.sum(p_past, axis=1, keepdims=True) + p_self
    acc = p_self.astype(BF16).astype(F32) * vn_ref[...].astype(BF16).astype(F32)
    p_past = p_past.astype(BF16)
    for p in range(n_pages):
        acc = acc + _dot(p_past[:, p * rows:(p + 1) * rows], v_refs[p][...].astype(BF16))
    o_ref[...] = acc / l


def _sample_attn(page_table, q3, kn3, vn3, mask3, bias_row, cache_k, cache_v):
    db, n_pages = page_table.shape
    n_cols = n_pages * PAGE_SIZE * N_HEADS + LANES
    per = lambda s, pt: (s, 0, 0)
    page_specs = [pl.BlockSpec((None, PAGE_SIZE * N_HEADS, HEAD_DIM),
                               functools.partial(lambda s, pt, p: (pt[s, p], 0, 0), p=p)) for p in range(n_pages)]
    return pl.pallas_call(
        functools.partial(_sample_attn_body, n_pages=n_pages),
        grid_spec=pltpu.PrefetchScalarGridSpec(
            num_scalar_prefetch=1, grid=(db,),
            in_specs=[pl.BlockSpec((None, N_HEADS, HEAD_DIM), per),
                      pl.BlockSpec((None, N_HEADS, HEAD_DIM), per),
                      pl.BlockSpec((None, N_HEADS, HEAD_DIM), per),
                      pl.BlockSpec((None, 1, n_cols), per),
                      pl.BlockSpec((N_HEADS, 1, n_cols), lambda s, pt: (0, 0, 0))] + page_specs + page_specs,
            out_specs=pl.BlockSpec((None, N_HEADS, HEAD_DIM), per)),
        out_shape=jax.ShapeDtypeStruct((db, N_HEADS, HEAD_DIM), F32),
        compiler_params=_params("arbitrary"),
        name="sample_attn",
    )(page_table, q3, kn3, vn3, mask3, bias_row, *([cache_k] * n_pages), *([cache_v] * n_pages))


def _cmul(ar, ai, br, bi):
    return ar * br - ai * bi, ar * bi + ai * br


def _ssm_prep_body(lre_f, lim_f, ldt_f, lre_r, lim_r, ldt_r, bre_ref, bim_ref, pre_ref, pim_ref, bbr_ref, bbi_ref,
                   *, chunk):
    def lam_bar(lre, lim, ldt):
        dt = jnp.exp(ldt)
        mag = jnp.exp(lre * dt)
        return mag * jnp.cos(lim * dt), mag * jnp.sin(lim * dt)

    lbr, lbi = lam_bar(lre_f[...], lim_f[...], ldt_f[...])
    pr = jnp.broadcast_to(lbr, (chunk, N_STATE))
    pi = jnp.broadcast_to(lbi, (chunk, N_STATE))
    row = lax.broadcasted_iota(I32, (chunk, N_STATE), 0)
    d = 1
    while d < chunk:
        sr = jnp.where(row >= d, pltpu.roll(pr, d, 0), 1.0)
        si = jnp.where(row >= d, pltpu.roll(pi, d, 0), 0.0)
        pr, pi = _cmul(pr, pi, sr, si)
        d *= 2
    pre_ref[...] = pr
    pim_ref[...] = pi

    lre, lim = lre_r[...], lim_r[...]
    lbr, lbi = lam_bar(lre, lim, ldt_r[...])
    den = lre * lre + lim * lim
    nr, ni = lbr - 1.0, lbi
    cr = (nr * lre + ni * lim) / den
    ci = (ni * lre - nr * lim) / den
    bbr, bbi = _cmul(cr, ci, bre_ref[...], bim_ref[...])
    bbr_ref[...] = bbr
    bbi_ref[...] = bbi


def _ssm_prep(lam_re, lam_im, log_dt, b_re, b_im, chunk):
    flat = lambda a: a.reshape(1, N_STATE)
    rep = lambda a: jnp.repeat(a, GROUP, axis=0)
    ldt = jnp.broadcast_to(log_dt[:, None], (N_GROUPS, STATE_DIM))
    bt = lambda b: jnp.swapaxes(b, 1, 2).reshape(SSM_WIDTH, STATE_DIM)
    return pl.pallas_call(
        functools.partial(_ssm_prep_body, chunk=chunk),
        out_shape=[jax.ShapeDtypeStruct((chunk, N_STATE), F32), jax.ShapeDtypeStruct((chunk, N_STATE), F32),
                   jax.ShapeDtypeStruct((SSM_WIDTH, STATE_DIM), F32), jax.ShapeDtypeStruct((SSM_WIDTH, STATE_DIM), F32)],
        compiler_params=pltpu.CompilerParams(vmem_limit_bytes=VMEM_LIMIT),
        name="ssm_prep",
    )(flat(lam_re), flat(lam_im), flat(ldt), rep(lam_re), rep(lam_im), rep(ldt), bt(b_re), bt(b_im))


def _block_diag(blocks):
    g, r, c = blocks.shape
    eye = jnp.eye(g, dtype=blocks.dtype)
    return (blocks[:, :, None, :] * eye[:, None, :, None]).reshape(g * r, g * c)


def _ssm_readout(u, xr, xi, wc_ref, dskip_ref, wglu_ref, bglu_ref):
    y = _dot(jnp.concatenate([xr, xi], axis=1).astype(BF16), wc_ref[...]) + dskip_ref[...] * u
    g = jax.nn.gelu(y)
    return g * jax.nn.sigmoid(_dot(g.astype(BF16), wglu_ref[...]) + bglu_ref[...])


def _ssm_scan_body(u_ref, x0r_ref, x0i_ref, pre_ref, pim_ref, wb_ref, wc_ref, dskip_ref, wglu_ref, bglu_ref,
                   y_ref, sr_ref, si_ref, cr_ref, ci_ref, xr_ref, xi_ref, *, chunk):
    c = pl.program_id(1)

    @pl.when(c == 0)
    def _():
        cr_ref[...] = x0r_ref[0]
        ci_ref[...] = x0i_ref[0]

    u = u_ref[...]
    bu = _dot(u.astype(BF16), wb_ref[...])
    n_groups = chunk // SCAN_GROUP
    xr = bu[:, :N_STATE].reshape(n_groups, SCAN_GROUP, N_STATE)
    xi = bu[:, N_STATE:].reshape(n_groups, SCAN_GROUP, N_STATE)
    pos = lax.broadcasted_iota(I32, (SCAN_GROUP, N_STATE), 0)
    d = 1
    while d < SCAN_GROUP:
        ar = jnp.where(pos >= d, pre_ref[d - 1:d, :], 0.0)
        ai = jnp.where(pos >= d, pim_ref[d - 1:d, :], 0.0)
        tr, ti = _cmul(ar, ai, pltpu.roll(xr, d, 1), pltpu.roll(xi, d, 1))
        xr, xi = xr + tr, xi + ti
        d *= 2
    pr, pi = pre_ref[...], pim_ref[...]
    cr, ci = cr_ref[...], ci_ref[...]
    for g in range(n_groups):
        tr, ti = _cmul(pr, pi, cr, ci)
        gr, gi = xr[g] + tr, xi[g] + ti
        xr_ref[g * SCAN_GROUP:(g + 1) * SCAN_GROUP, :] = gr
        xi_ref[g * SCAN_GROUP:(g + 1) * SCAN_GROUP, :] = gi
        cr, ci = gr[SCAN_GROUP - 1:, :], gi[SCAN_GROUP - 1:, :]
    cr_ref[...] = cr
    ci_ref[...] = ci
    sr_ref[0] = cr
    si_ref[0] = ci
    y_ref[...] = _ssm_readout(u, xr_ref[...], xi_ref[...], wc_ref, dskip_ref, wglu_ref, bglu_ref)


def _ssm_scan(u, x0r, x0i, pre, pim, wb, wc, dskip, wglu, bglu, bsz, t, chunk):
    nc = t // chunk
    row = lambda b, c: (b * nc + c, 0)
    const = lambda b, c: (0, 0)
    state = pl.BlockSpec((1, 1, N_STATE), lambda b, c: (b, 0, 0))
    return pl.pallas_call(
        functools.partial(_ssm_scan_body, chunk=chunk),
        grid=(bsz, nc),
        in_specs=[pl.BlockSpec((chunk, SSM_WIDTH), row), state, state,
                  pl.BlockSpec((SCAN_GROUP, N_STATE), const), pl.BlockSpec((SCAN_GROUP, N_STATE), const),
                  pl.BlockSpec((SSM_WIDTH, 2 * N_STATE), const), pl.BlockSpec((2 * N_STATE, SSM_WIDTH), const),
                  pl.BlockSpec((1, SSM_WIDTH), const), pl.BlockSpec((SSM_WIDTH, SSM_WIDTH), const),
                  pl.BlockSpec((1, SSM_WIDTH), const)],
        out_specs=[pl.BlockSpec((chunk, SSM_WIDTH), row), state, state],
        out_shape=[jax.ShapeDtypeStruct((bsz * t, SSM_WIDTH), F32),
                   jax.ShapeDtypeStruct((bsz, 1, N_STATE), F32), jax.ShapeDtypeStruct((bsz, 1, N_STATE), F32)],
        scratch_shapes=[pltpu.VMEM((1, N_STATE), F32), pltpu.VMEM((1, N_STATE), F32),
                        pltpu.VMEM((chunk, N_STATE), F32), pltpu.VMEM((chunk, N_STATE), F32)],
        compiler_params=_params("arbitrary", "arbitrary"),
        name="ssm_scan",
    )(u, x0r, x0i, pre, pim, wb, wc, dskip, wglu, bglu)


def _ssm_step_body(u_ref, x0r_ref, x0i_ref, pre_ref, pim_ref, wb_ref, wc_ref, dskip_ref, wglu_ref, bglu_ref,
                   y_ref, sr_ref, si_ref):
    u = u_ref[...]
    bu = _dot(u.astype(BF16), wb_ref[...])
    tr, ti = _cmul(pre_ref[0:1, :], pim_ref[0:1, :], x0r_ref[...], x0i_ref[...])
    xr, xi = bu[:, :N_STATE] + tr, bu[:, N_STATE:] + ti
    sr_ref[...] = xr
    si_ref[...] = xi
    y_ref[...] = _ssm_readout(u, xr, xi, wc_ref, dskip_ref, wglu_ref, bglu_ref)


def _ssm_step(u, x0r, x0i, pre, pim, wb, wc, dskip, wglu, bglu):
    n = u.shape[0]
    return pl.pallas_call(
        _ssm_step_body,
        out_shape=[jax.ShapeDtypeStruct((n, SSM_WIDTH), F32),
                   jax.ShapeDtypeStruct((n, N_STATE), F32), jax.ShapeDtypeStruct((n, N_STATE), F32)],
        compiler_params=pltpu.CompilerParams(vmem_limit_bytes=VMEM_LIMIT),
        name="ssm_step",
    )(u, x0r, x0i, pre, pim, wb, wc, dskip, wglu, bglu)


def _merge_body(x_ref, ya_ref, ys_ref, sga_ref, sgs_ref, wa_ref, ws_ref, wo_ref, gpost_ref, gpre_ref,
                x1_ref, h2_ref):
    merged = (sga_ref[...] * _dot(ya_ref[...].astype(BF16), wa_ref[...])
              + sgs_ref[...] * _dot(ys_ref[...].astype(BF16), ws_ref[...]))
    x1 = x_ref[...] + _rms(_dot(merged.astype(BF16), wo_ref[...]), gpost_ref[...])
    x1_ref[...] = x1
    h2_ref[...] = _rms(x1, gpre_ref[...]).astype(BF16)


def _merge(x2, ya, ys, sga, sgs, wa, ws, wo, gpost, gpre, tm):
    m = x2.shape[0]
    row = lambda i: (i, 0)
    const = lambda i: (0, 0)
    return pl.pallas_call(
        _merge_body,
        grid=(m // tm,),
        in_specs=[pl.BlockSpec((tm, D_MODEL), row), pl.BlockSpec((tm, ATT_WIDTH), row),
                  pl.BlockSpec((tm, SSM_WIDTH), row), pl.BlockSpec((tm, D_MODEL), row),
                  pl.BlockSpec((tm, D_MODEL), row),
                  pl.BlockSpec((ATT_WIDTH, D_MODEL), const), pl.BlockSpec((SSM_WIDTH, D_MODEL), const),
                  pl.BlockSpec((D_MODEL, D_MODEL), const), pl.BlockSpec((1, D_MODEL), const),
                  pl.BlockSpec((1, D_MODEL), const)],
        out_specs=[pl.BlockSpec((tm, D_MODEL), row), pl.BlockSpec((tm, D_MODEL), row)],
        out_shape=[jax.ShapeDtypeStruct((m, D_MODEL), F32), jax.ShapeDtypeStruct((m, D_MODEL), BF16)],
        compiler_params=_params("arbitrary"),
        name="merge",
    )(x2, ya, ys, sga, sgs, wa, ws, wo, gpost, gpre)


def _ffn_body(h_ref, x_ref, wua_ref, wub_ref, cwa_ref, cwb_ref, cba_ref, cbb_ref, wd_ref, g_ref, pa_ref, pb_ref,
              y_ref, oa_ref, ob_ref, f_ref, *carry, seq, tiles_per_seq):
    i, j = pl.program_id(0), pl.program_id(1)
    nj = pl.num_programs(1)
    h = h_ref[...]
    tm = h.shape[0]

    def conv(up, cw_ref, cb_ref, prev_ref, carry_ref, out_ref):
        if seq:
            @pl.when(i % tiles_per_seq == 0)
            def _():
                carry_ref[j, 0:CONV_W - 1, :] = prev_ref[0]
            row = lax.broadcasted_iota(I32, up.shape, 0)
            before1 = carry_ref[j, 1:2, :]
            before2 = carry_ref[j, 0:1, :]
            m1 = jnp.where(row == 0, before1, pltpu.roll(up, 1, 0))
            m2 = jnp.where(row == 0, before2, jnp.where(row == 1, before1, pltpu.roll(up, 2, 0)))
            tail = up[tm - (CONV_W - 1):, :]
            carry_ref[j, 0:CONV_W - 1, :] = tail
            out_ref[0] = tail
        else:
            m2, m1 = prev_ref[:, 0, :], prev_ref[:, 1, :]
            out_ref[...] = up
        return cb_ref[...] + m2 * cw_ref[0:1, :] + m1 * cw_ref[1:2, :] + up * cw_ref[2:3, :]

    ca, cb = carry if seq else (None, None)
    a = conv(_dot(h, wua_ref[...]), cwa_ref, cba_ref, pa_ref, ca, oa_ref)
    b = conv(_dot(h, wub_ref[...]), cwb_ref, cbb_ref, pb_ref, cb, ob_ref)
    part = _dot((jax.nn.gelu(a) * b).astype(BF16), wd_ref[...])

    @pl.when(j == 0)
    def _():
        f_ref[...] = part

    @pl.when(j > 0)
    def _():
        f_ref[...] += part

    @pl.when(j == nj - 1)
    def _():
        y_ref[...] = x_ref[...] + _rms(f_ref[...], g_ref[...])


def _ffn(h2, x1, w_up, conv_w, conv_b, w_down, g_post, conv_prev, bsz, t, tm, tf, seq):
    m = bsz * t
    nj = D_FF // tf
    row = lambda i, j: (i, 0)
    if seq:
        tiles_per_seq = t // tm
        prev_a = pl.BlockSpec((1, CONV_W - 1, tf), lambda i, j: (i // tiles_per_seq, 0, j))
        prev_b = pl.BlockSpec((1, CONV_W - 1, tf), lambda i, j: (i // tiles_per_seq, 0, j + nj))
        out_tail = pl.BlockSpec((1, CONV_W - 1, tf), lambda i, j: (i, 0, j))
        tail_shape = jax.ShapeDtypeStruct((m // tm, CONV_W - 1, D_FF), F32)
        scratch = [pltpu.VMEM((nj, 8, tf), F32), pltpu.VMEM((nj, 8, tf), F32)]
    else:
        tiles_per_seq = 1
        prev_a = pl.BlockSpec((tm, CONV_W - 1, tf), lambda i, j: (i, 0, j))
        prev_b = pl.BlockSpec((tm, CONV_W - 1, tf), lambda i, j: (i, 0, j + nj))
        out_tail = pl.BlockSpec((tm, tf), lambda i, j: (i, j))
        tail_shape = jax.ShapeDtypeStruct((m, D_FF), F32)
        scratch = []
    return pl.pallas_call(
        functools.partial(_ffn_body, seq=seq, tiles_per_seq=tiles_per_seq),
        grid=(m // tm, nj),
        in_specs=[pl.BlockSpec((tm, D_MODEL), row), pl.BlockSpec((tm, D_MODEL), row),
                  pl.BlockSpec((D_MODEL, tf), lambda i, j: (0, j)),
                  pl.BlockSpec((D_MODEL, tf), lambda i, j: (0, j + nj)),
                  pl.BlockSpec((CONV_W, tf), lambda i, j: (0, j)),
                  pl.BlockSpec((CONV_W, tf), lambda i, j: (0, j + nj)),
                  pl.BlockSpec((1, tf), lambda i, j: (0, j)),
                  pl.BlockSpec((1, tf), lambda i, j: (0, j + nj)),
                  pl.BlockSpec((tf, D_MODEL), lambda i, j: (j, 0)),
                  pl.BlockSpec((1, D_MODEL), lambda i, j: (0, 0)),
                  prev_a, prev_b],
        out_specs=[pl.BlockSpec((tm, D_MODEL), row), out_tail, out_tail],
        out_shape=[jax.ShapeDtypeStruct((m, D_MODEL), F32), tail_shape, tail_shape],
        scratch_shapes=[pltpu.VMEM((tm, D_MODEL), F32)] + scratch,
        compiler_params=_params("arbitrary", "arbitrary"),
        name="conv_ffn",
    )(h2, x1, w_up, w_up, conv_w, conv_w, conv_b, conv_b, w_down, g_post, conv_prev, conv_prev)


def _pack_w_in(w_in):
    points = np.cumsum(SPLITS)[:-1].tolist()
    wq, wk, wv, wqi, wki, wwi, wu, wga, wgs = jnp.split(w_in, points, axis=-1)
    pad = jnp.zeros((D_MODEL, LANES - IDX_DIM - IDX_HEADS), w_in.dtype)
    return jnp.concatenate([wq, wk, wv, wqi, wki, wwi, pad, wu, wga, wgs], axis=-1).astype(BF16)


def _layer_weights(lw):
    (w_in, g_pre_mix, g_post_mix, lam_re, lam_im, log_dt, b_re, b_im, c_re, c_im, d_skip,
     w_glu, b_glu, w_att_out, w_ssm_out, w_o, g_pre_ffn, g_post_ffn, w_up, conv_w, conv_b, w_down) = lw
    pre, pim, bbr, bbi = _ssm_prep(lam_re, lam_im, log_dt, b_re, b_im, SCAN_GROUP)
    wb = jnp.concatenate([_block_diag(bbr.reshape(N_GROUPS, GROUP, STATE_DIM)),
                          _block_diag(bbi.reshape(N_GROUPS, GROUP, STATE_DIM))], axis=1).astype(BF16)
    wc = jnp.concatenate([_block_diag(jnp.swapaxes(c_re, 1, 2)),
                          _block_diag(-jnp.swapaxes(c_im, 1, 2))], axis=0).astype(BF16)
    vec = lambda a: a.reshape(1, -1)
    return dict(
        w_in=_pack_w_in(w_in), g_pre_mix=vec(g_pre_mix), g_post_mix=vec(g_post_mix),
        pre=pre, pim=pim, wb=wb, wc=wc, d_skip=vec(d_skip), w_glu=w_glu.astype(BF16), b_glu=vec(b_glu),
        w_att_out=w_att_out.astype(BF16), w_ssm_out=w_ssm_out.astype(BF16), w_o=w_o.astype(BF16),
        g_pre_ffn=vec(g_pre_ffn), g_post_ffn=vec(g_post_ffn), w_up=w_up.astype(BF16), conv_w=conv_w,
        conv_b=vec(conv_b), w_down=w_down.astype(BF16))


def _prompt_layer(x, w, bias_tiles, tq, tm, chunk, tf):
    bsz, t, _ = x.shape
    x2 = x.reshape(bsz * t, D_MODEL)
    k, v, kw, u, sga, sgs, qt, qit, kwt, kh, vt, ki = _in_proj(x2, w["g_pre_mix"], w["w_in"], bsz, t, tm, True)
    y_att = _prompt_attn(qt, qit, kwt, kh, vt, ki, bias_tiles, bsz, t, tq, min(TOPK_MAX, t // 4))
    zero_state = jnp.zeros((bsz, 1, N_STATE), F32)
    y_ssm, s_re, s_im = _ssm_scan(u, zero_state, zero_state, w["pre"], w["pim"], w["wb"], w["wc"], w["d_skip"],
                                  w["w_glu"], w["b_glu"], bsz, t, chunk)
    x1, h2 = _merge(x2, y_att, y_ssm, sga, sgs, w["w_att_out"], w["w_ssm_out"], w["w_o"],
                    w["g_post_mix"], w["g_pre_ffn"], tm)
    zero_conv = jnp.zeros((bsz, CONV_W - 1, 2 * D_FF), F32)
    y, tail_a, tail_b = _ffn(h2, x1, w["w_up"], w["conv_w"], w["conv_b"], w["w_down"], w["g_post_ffn"],
                             zero_conv, bsz, t, tm, tf, True)
    state = (k.reshape(bsz, t, N_HEADS, HEAD_DIM), v.reshape(bsz, t, N_HEADS, HEAD_DIM),
             kw[:, :IDX_DIM].reshape(bsz, t, IDX_DIM),
             s_re.reshape(bsz, N_GROUPS, STATE_DIM), s_im.reshape(bsz, N_GROUPS, STATE_DIM),
             jnp.concatenate([tail_a, tail_b], axis=-1)[t // tm - 1::t // tm])
    return y.reshape(bsz, t, D_MODEL), state


def _sample_layer(x, w, bias_row, cache_k, cache_v, cache_kidx, page_table, st_re, st_im, st_conv, tf):
    db, tq, _ = x.shape
    assert tq == 1, "the sample group is decoded one token per sequence"
    n_pool = cache_k.shape[0]
    past = page_table.shape[1] * PAGE_SIZE
    x2 = x.reshape(db, D_MODEL)
    k, v, kw, u, sga, sgs, q, qi = _in_proj(x2, w["g_pre_mix"], w["w_in"], 1, db, db, False)
    keys = _sample_scores(page_table, qi.reshape(db, IDX_HEADS, IDX_DIM),
                          kw[:, IDX_DIM:IDX_DIM + IDX_HEADS].reshape(db, IDX_HEADS, 1),
                          kw[:, :IDX_DIM].reshape(db, 1, IDX_DIM), cache_kidx)
    mask = _sample_select(keys.reshape(db, past + LANES), min(TOPK_MAX, (past + tq) // 4))
    heads = lambda a: a.reshape(db, N_HEADS, HEAD_DIM)
    pages = lambda c: c.reshape(n_pool, PAGE_SIZE * N_HEADS, HEAD_DIM)
    mask_rows = jnp.concatenate([jnp.repeat(mask[:, :past], N_HEADS, axis=1), mask[:, past:]], axis=1)
    y_att = _sample_attn(page_table, heads(q), heads(k), heads(v), mask_rows[:, None, :], bias_row,
                         pages(cache_k), pages(cache_v))
    y_ssm, s_re, s_im = _ssm_step(u, st_re.reshape(db, N_STATE), st_im.reshape(db, N_STATE), w["pre"], w["pim"],
                                  w["wb"], w["wc"], w["d_skip"], w["w_glu"], w["b_glu"])
    x1, h2 = _merge(x2, y_att.reshape(db, ATT_WIDTH), y_ssm, sga, sgs, w["w_att_out"], w["w_ssm_out"], w["w_o"],
                    w["g_post_mix"], w["g_pre_ffn"], db)
    y, up_a, up_b = _ffn(h2, x1, w["w_up"], w["conv_w"], w["conv_b"], w["w_down"], w["g_post_ffn"],
                         st_conv, db, 1, db, tf, False)
    conv_new = jnp.concatenate([st_conv[:, 1:], jnp.concatenate([up_a, up_b], axis=-1)[:, None, :]], axis=1)
    state = (k.reshape(db, 1, N_HEADS, HEAD_DIM), v.reshape(db, 1, N_HEADS, HEAD_DIM),
             kw[:, :IDX_DIM].reshape(db, 1, IDX_DIM),
             s_re.reshape(db, N_GROUPS, STATE_DIM), s_im.reshape(db, N_GROUPS, STATE_DIM), conv_new)
    return y.reshape(db, 1, D_MODEL), state


def kernel(x_prompt, x_sample, cache_k, cache_v, cache_kidx, state_ssm_re, state_ssm_im, state_conv, page_table,
           rel_bias, w_in, g_pre_mix, g_post_mix, lam_re, lam_im, log_dt, b_re, b_im, c_re, c_im, d_skip, w_glu,
           b_glu, w_att_out, w_ssm_out, w_o, g_pre_ffn, g_post_ffn, w_up, conv_w, conv_b, w_down):
    depth = w_in.shape[0]
    t = x_prompt.shape[1]
    past = page_table.shape[1] * PAGE_SIZE
    tq = min(256, t)
    tm = min(512, t)
    chunk = min(256, t)
    tf = 512
    bias_tiles, bias_row = _bias_tiles(rel_bias, tq, past)
    y_p, y_s = x_prompt, x_sample
    outs_p, outs_s = [], []
    for l in range(depth):
        lw = (w_in[l], g_pre_mix[l], g_post_mix[l], lam_re[l], lam_im[l], log_dt[l], b_re[l], b_im[l],
              c_re[l], c_im[l], d_skip[l], w_glu[l], b_glu[l], w_att_out[l], w_ssm_out[l], w_o[l],
              g_pre_ffn[l], g_post_ffn[l], w_up[l], conv_w[l], conv_b[l], w_down[l])
        w = _layer_weights(lw)
        y_p, st_p = _prompt_layer(y_p, w, bias_tiles, tq, tm, chunk, tf)
        y_s, st_s = _sample_layer(y_s, w, bias_row, cache_k[l], cache_v[l], cache_kidx[l], page_table,
                                  state_ssm_re[l], state_ssm_im[l], state_conv[l], tf)
        outs_p.append(st_p)
        outs_s.append(st_s)
    k_p, v_p, ki_p, sr_p, si_p, cv_p = [jnp.stack(a) for a in zip(*outs_p)]
    k_s, v_s, ki_s, sr_s, si_s, cv_s = [jnp.stack(a) for a in zip(*outs_s)]
    return (y_p, y_s, k_p, v_p, ki_p, sr_p, si_p, cv_p, k_s, v_s, ki_s, sr_s, si_s, cv_s)
```

```python
import functools
import math

import numpy as np
import jax
import jax.numpy as jnp
from jax import lax
from jax.experimental import pallas as pl
from jax.experimental.pallas import tpu as pltpu

F32 = jnp.float32
BF16 = jnp.bfloat16
I32 = jnp.int32

D_MODEL = 1024
PAGE_SIZE = 128
N_HEADS = 8
HEAD_DIM = 64
ATT_WIDTH = N_HEADS * HEAD_DIM
IDX_HEADS = 4
IDX_DIM = 64
TOPK_MAX = 256
N_BUCKETS = 32
MAX_DISTANCE = 128
SSM_WIDTH = 512
GROUP = 16
N_GROUPS = SSM_WIDTH // GROUP
STATE_DIM = 64
N_STATE = N_GROUPS * STATE_DIM
D_FF = 4 * D_MODEL
CONV_W = 3
EPS = 1e-6
SPLITS = (ATT_WIDTH, ATT_WIDTH, ATT_WIDTH, IDX_HEADS * IDX_DIM, IDX_DIM, IDX_HEADS, SSM_WIDTH, D_MODEL, D_MODEL)

LANES = 128
KEY_EXCLUDED = -(2 ** 31)
CODE_NEVER = 2 ** 30
IDX_BITS = 14
BF16_SUBLANES = 16
V_ROWS = HEAD_DIM + BF16_SUBLANES
SCAN_GROUP = 8
MASKED = -1e30
VMEM_LIMIT = 56 * 1024 * 1024

_C_Q, _C_K, _C_V = 0, ATT_WIDTH, 2 * ATT_WIDTH
_C_QI = 3 * ATT_WIDTH
_C_KW = _C_QI + IDX_HEADS * IDX_DIM
_C_U = _C_KW + LANES
_C_GA = _C_U + SSM_WIDTH
_C_GS = _C_GA + D_MODEL
_C_END = _C_GS + D_MODEL


def _params(*sem):
    return pltpu.CompilerParams(dimension_semantics=sem, vmem_limit_bytes=VMEM_LIMIT)


def _rms(x, g):
    inv = lax.rsqrt(jnp.mean(x * x, axis=-1, keepdims=True) + EPS)
    return (x * inv) * g


def _dot(a, b):
    return jnp.dot(a, b, preferred_element_type=F32)


def _dot_nt(a, b):
    return lax.dot_general(a, b, (((1,), (1,)), ((), ())), preferred_element_type=F32)


def _bucket_starts():
    n = np.arange(0, 1 << IDX_BITS, dtype=np.int32)
    max_exact = N_BUCKETS // 2
    nf = np.maximum(n, 1).astype(np.float32)
    large = max_exact + (np.log(nf / np.float32(max_exact)) / np.float32(math.log(MAX_DISTANCE / max_exact))
                         * np.float32(N_BUCKETS - max_exact)).astype(np.int32)
    large = np.minimum(large, N_BUCKETS - 1)
    bucket = np.where(n < max_exact, n, large)
    assert np.all(np.diff(bucket) >= 0)
    starts = [int(np.argmax(bucket >= b)) for b in range(N_BUCKETS)]
    assert all(bucket[s] == b for b, s in enumerate(starts))
    return starts


_BUCKET_START = _bucket_starts()


def _bias_tiles_body(rb_ref, tile_ref, row_ref, *, tq, past):
    h = pl.program_id(0)

    def bias_of(dist):
        val = jnp.full(dist.shape, rb_ref[0, h], F32)
        for b in range(1, N_BUCKETS):
            val = jnp.where(dist >= _BUCKET_START[b], rb_ref[b, h], val)
        return val

    key_off = lax.broadcasted_iota(I32, (tq, tq), 0)
    qry_off = lax.broadcasted_iota(I32, (tq, tq), 1)
    far = rb_ref[N_BUCKETS - 1, h]
    tile_ref[0, 1] = bias_of(qry_off - key_off) - far
    tile_ref[0, 0] = bias_of(qry_off - key_off + tq) - far
    lane = lax.broadcasted_iota(I32, (1, past + LANES), 1)
    row_ref[0] = bias_of(past - lane)


def _bias_tiles(rel_bias, tq, past):
    assert tq + 1 >= _BUCKET_START[N_BUCKETS - 1]
    return pl.pallas_call(
        functools.partial(_bias_tiles_body, tq=tq, past=past),
        grid=(N_HEADS,),
        in_specs=[pl.BlockSpec(memory_space=pltpu.SMEM)],
        out_specs=[pl.BlockSpec((1, 2, tq, tq), lambda h: (h, 0, 0, 0)),
                   pl.BlockSpec((1, 1, past + LANES), lambda h: (h, 0, 0))],
        out_shape=[jax.ShapeDtypeStruct((N_HEADS, 2, tq, tq), F32),
                   jax.ShapeDtypeStruct((N_HEADS, 1, past + LANES), F32)],
        compiler_params=_params("arbitrary"),
        name="bias_tiles",
    )(rel_bias)


def _in_proj_body(x_ref, g_ref, w_ref, k_ref, v_ref, kw_ref, u_ref, sga_ref, sgs_ref, *more_refs, seq):
    h = _rms(x_ref[...], g_ref[...]).astype(BF16)

    def proj(c0, c1):
        return _dot(h, w_ref[:, c0:c1])

    q = proj(_C_Q, _C_K) * HEAD_DIM ** -0.5
    k = proj(_C_K, _C_V)
    v = proj(_C_V, _C_QI)
    k_ref[...] = k
    v_ref[...] = v
    qi = proj(_C_QI, _C_KW)
    kw = proj(_C_KW, _C_U)
    lane = lax.broadcasted_iota(I32, kw.shape, 1)
    kw = jnp.where(lane >= IDX_DIM, kw * IDX_HEADS ** -0.5, kw)
    kw_ref[...] = kw
    u_ref[...] = proj(_C_U, _C_GA)
    sga_ref[...] = jax.nn.sigmoid(proj(_C_GA, _C_GS))
    sgs_ref[...] = jax.nn.sigmoid(proj(_C_GS, _C_END))
    if seq:
        qt_ref, qit_ref, kwt_ref, kh_ref, vt_ref, ki_ref = more_refs
        tm = k.shape[0]
        qt_ref[...] = q.T.reshape(N_HEADS, HEAD_DIM, tm).astype(BF16)
        qit_ref[...] = (qi * IDX_DIM ** -0.5).T.reshape(IDX_HEADS, IDX_DIM, tm).astype(BF16)
        kwt_ref[0] = kw.T
        for hd in range(N_HEADS):
            kh_ref[0, hd] = k[:, hd * HEAD_DIM:(hd + 1) * HEAD_DIM].astype(BF16)
        vt_ref[0, :, :HEAD_DIM, :] = v.T.reshape(N_HEADS, HEAD_DIM, tm).astype(BF16)
        vt_ref[0, :, HEAD_DIM:, :] = jnp.ones((N_HEADS, V_ROWS - HEAD_DIM, tm), BF16)
        ki_ref[0] = kw[:, :IDX_DIM].astype(BF16)
    else:
        q_ref, qi_ref = more_refs
        q_ref[...] = q.astype(BF16)
        qi_ref[...] = qi


def _in_proj(x2, g, w_packed, bsz, t, tm, seq):
    m = bsz * t
    nt = t // tm
    row = lambda b, i: (b * nt + i, 0)
    widths = (ATT_WIDTH, ATT_WIDTH, LANES, SSM_WIDTH, D_MODEL, D_MODEL)
    out_specs = [pl.BlockSpec((tm, w), row) for w in widths]
    out_shape = [jax.ShapeDtypeStruct((m, w), F32) for w in widths]
    if seq:
        col = lambda b, i: (0, 0, b * nt + i)
        out_specs += [pl.BlockSpec((N_HEADS, HEAD_DIM, tm), col),
                      pl.BlockSpec((IDX_HEADS, IDX_DIM, tm), col),
                      pl.BlockSpec((1, LANES, tm), lambda b, i: (b, 0, i)),
                      pl.BlockSpec((1, N_HEADS, tm, HEAD_DIM), lambda b, i: (b, 0, i, 0)),
                      pl.BlockSpec((1, N_HEADS, V_ROWS, tm), lambda b, i: (b, 0, 0, i)),
                      pl.BlockSpec((1, tm, IDX_DIM), lambda b, i: (b, i, 0))]
        out_shape += [jax.ShapeDtypeStruct((N_HEADS, HEAD_DIM, m), BF16),
                      jax.ShapeDtypeStruct((IDX_HEADS, IDX_DIM, m), BF16),
                      jax.ShapeDtypeStruct((bsz, LANES, t), F32),
                      jax.ShapeDtypeStruct((bsz, N_HEADS, t, HEAD_DIM), BF16),
                      jax.ShapeDtypeStruct((bsz, N_HEADS, V_ROWS, t), BF16),
                      jax.ShapeDtypeStruct((bsz, t, IDX_DIM), BF16)]
    else:
        out_specs += [pl.BlockSpec((tm, ATT_WIDTH), row), pl.BlockSpec((tm, IDX_HEADS * IDX_DIM), row)]
        out_shape += [jax.ShapeDtypeStruct((m, ATT_WIDTH), BF16),
                      jax.ShapeDtypeStruct((m, IDX_HEADS * IDX_DIM), F32)]
    return pl.pallas_call(
        functools.partial(_in_proj_body, seq=seq),
        grid=(bsz, nt),
        in_specs=[pl.BlockSpec((tm, D_MODEL), row),
                  pl.BlockSpec((1, D_MODEL), lambda b, i: (0, 0)),
                  pl.BlockSpec((D_MODEL, _C_END), lambda b, i: (0, 0))],
        out_specs=out_specs,
        out_shape=out_shape,
        compiler_params=_params("arbitrary", "arbitrary"),
        name="in_proj",
    )(x2, g, w_packed)


def _sortable_key(score):
    bits = pltpu.bitcast(score, I32)
    return jnp.where(bits < 0, bits ^ 0x7FFFFFFF, bits)


def _topk_mask(keys_ref, nblk, height, topk):
    n_lanes = keys_ref.shape[1]
    n_parts = height // 8
    assert n_parts * 8 == height and n_parts & (n_parts - 1) == 0

    def row_block(j):
        return pl.ds(pl.multiple_of(j * height, height), height)

    def count(pred, bound):
        def body(j, acc):
            hit = jnp.where(pred(keys_ref[row_block(j), :], bound), 1, 0)
            parts = [hit[r:r + 8] for r in range(0, height, 8)]
            while len(parts) > 1:
                parts = [a + b for a, b in zip(parts[::2], parts[1::2])]
            return acc + parts[0]
        acc = lax.fori_loop(0, nblk, body, jnp.zeros((8, n_lanes), I32))
        return jnp.sum(acc, axis=0, keepdims=True)

    zero = jnp.zeros((1, n_lanes), I32)

    prefix = jnp.where(count(lambda k, b: k >= b, zero) >= topk, 0, KEY_EXCLUDED).astype(I32)

    def value_bit(b, prefix):
        cand = prefix | jnp.left_shift(1, 30 - b)
        return jnp.where(count(lambda k, c: k >= c, cand) >= topk, cand, prefix)
    thr = lax.fori_loop(0, 31, value_bit, prefix)

    def encode(j, _):
        k = keys_ref[row_block(j), :]
        pos = j * height + lax.broadcasted_iota(I32, (height, n_lanes), 0)
        code = jnp.where(k > thr, -1, jnp.where(k == thr, pos, CODE_NEVER))
        keys_ref[row_block(j), :] = jnp.where(k == KEY_EXCLUDED, CODE_NEVER, code)
        return 0
    lax.fori_loop(0, nblk, encode, 0)

    def index_bit(b, bound):
        cand = bound | jnp.left_shift(1, IDX_BITS - 1 - b)
        return jnp.where(count(lambda c, x: c < x, cand) <= topk, cand, bound)
    bound = lax.fori_loop(0, IDX_BITS, index_bit, zero)

    def emit(j, _):
        mask = jnp.where(keys_ref[row_block(j), :] < bound, 0.0, MASKED).astype(F32)
        keys_ref[row_block(j), :] = pltpu.bitcast(mask, I32)
        return 0
    lax.fori_loop(0, nblk, emit, 0)


def _prompt_attn_body(qt_ref, qit_ref, wt_ref, kh_ref, vt_ref, ki_ref, bias_ref, o_ref, keys_ref, m_ref, acc_ref,
                      alpha_ref, p_ref, *, tq, topk):
    i = pl.program_id(1)
    nblk = i + 1
    key_off = lax.broadcasted_iota(I32, (tq, tq), 0)
    qry_off = lax.broadcasted_iota(I32, (tq, tq), 1)

    def key_block(j):
        return pl.ds(pl.multiple_of(j * tq, tq), tq)

    qit = [qit_ref[h] for h in range(IDX_HEADS)]
    wt = [wt_ref[0, h:h + 1, :] for h in range(IDX_HEADS)]

    def score_block(j, _, diagonal):
        ki = ki_ref[0, key_block(j), :]
        s = None
        for h in range(IDX_HEADS):
            sh = jnp.maximum(_dot(ki, qit[h]), 0.0) * wt[h]
            s = sh if s is None else s + sh
        key = _sortable_key(s)
        if diagonal:
            key = jnp.where(key_off <= qry_off, key, KEY_EXCLUDED)
        keys_ref[key_block(j), :] = key
        return 0
    lax.fori_loop(0, i, functools.partial(score_block, diagonal=False), 0)
    score_block(i, 0, diagonal=True)

    _topk_mask(keys_ref, nblk, tq, topk)

    m_ref[...] = jnp.full(m_ref.shape, MASKED, F32)
    acc_ref[...] = jnp.zeros(acc_ref.shape, F32)
    p_ref[...] = jnp.zeros(p_ref.shape, BF16)
    alpha_ref[...] = jnp.ones(alpha_ref.shape, F32)

    def accumulate(j):
        for h in range(N_HEADS):
            acc_ref[h] = alpha_ref[h] * acc_ref[h] + _dot(vt_ref[0, h, :, key_block(j)], p_ref[h])

    def attend(j, _, near):
        accumulate(jnp.maximum(j - 1, 0))
        masked = pltpu.bitcast(keys_ref[key_block(j), :], F32)
        for h in range(N_HEADS):
            s = _dot(kh_ref[0, h, key_block(j), :], qt_ref[h]) + masked
            if near:
                s = s + bias_ref[h, j - i + 1]
            m_old = m_ref[h]
            m_new = jnp.maximum(m_old, jnp.max(s, axis=0, keepdims=True))
            p_ref[h] = jnp.exp(s - m_new).astype(BF16)
            alpha_ref[h] = jnp.exp(m_old - m_new)
            m_ref[h] = m_new
        return 0

    n_far = jnp.maximum(i - 1, 0)
    lax.fori_loop(0, n_far, functools.partial(attend, near=False), 0)
    lax.fori_loop(n_far, nblk, functools.partial(attend, near=True), 0)
    accumulate(i)
    out_t = jnp.concatenate(
        [acc_ref[h, :HEAD_DIM, :] / acc_ref[h, HEAD_DIM:HEAD_DIM + 1, :] for h in range(N_HEADS)], axis=0)
    o_ref[...] = out_t.T


def _prompt_attn(qt, qit, kwt, kh, vt, ki, bias_tiles, bsz, t, tq, topk):
    nq = t // tq
    col = lambda b, i: (0, 0, b * nq + i)
    whole = dict(pipeline_mode=pl.Buffered(1))
    return pl.pallas_call(
        functools.partial(_prompt_attn_body, tq=tq, topk=topk),
        grid=(bsz, nq),
        in_specs=[pl.BlockSpec((N_HEADS, HEAD_DIM, tq), col),
                  pl.BlockSpec((IDX_HEADS, IDX_DIM, tq), col),
                  pl.BlockSpec((1, 8, tq), lambda b, i: (b, IDX_DIM // 8, i)),
                  pl.BlockSpec((1, N_HEADS, t, HEAD_DIM), lambda b, i: (b, 0, 0, 0), **whole),
                  pl.BlockSpec((1, N_HEADS, V_ROWS, t), lambda b, i: (b, 0, 0, 0), **whole),
                  pl.BlockSpec((1, t, IDX_DIM), lambda b, i: (b, 0, 0), **whole),
                  pl.BlockSpec((N_HEADS, 2, tq, tq), lambda b, i: (0, 0, 0, 0), **whole)],
        out_specs=pl.BlockSpec((tq, ATT_WIDTH), lambda b, i: (b * nq + i, 0)),
        out_shape=jax.ShapeDtypeStruct((bsz * t, ATT_WIDTH), F32),
        scratch_shapes=[pltpu.VMEM((t, tq), I32),
                        pltpu.VMEM((N_HEADS, 1, tq), F32), pltpu.VMEM((N_HEADS, V_ROWS, tq), F32),
                        pltpu.VMEM((N_HEADS, 1, tq), F32), pltpu.VMEM((N_HEADS, tq, tq), BF16)],
        compiler_params=_params("arbitrary", "arbitrary"),
        name="prompt_attn",
    )(qt, qit, kwt, kh, vt, ki, bias_tiles)


def _sample_scores_body(pt_ref, qi_ref, w_ref, kn_ref, *rest, n_pages):
    page_refs, key_ref = rest[:n_pages], rest[n_pages]
    qi = qi_ref[...].astype(BF16)
    w = w_ref[...]

    def weighted(s):
        return jnp.sum(jnp.maximum(s * IDX_DIM ** -0.5, 0.0) * w, axis=0, keepdims=True)

    for p in range(n_pages):
        s = _dot(qi, page_refs[p][...].astype(BF16))
        key_ref[:, p * PAGE_SIZE:(p + 1) * PAGE_SIZE] = _sortable_key(weighted(s))
    s_self = jnp.sum(qi.astype(F32) * kn_ref[...].astype(BF16).astype(F32), axis=1, keepdims=True)
    lane = lax.broadcasted_iota(I32, (1, LANES), 1)
    key_ref[:, n_pages * PAGE_SIZE:] = jnp.where(lane == 0, _sortable_key(weighted(s_self)), KEY_EXCLUDED)


def _sample_scores(page_table, qi3, w3, kn3, cache_kidx):
    db, n_pages = page_table.shape
    n_cols = n_pages * PAGE_SIZE + LANES
    per = lambda s, pt: (s, 0, 0)
    page_specs = [pl.BlockSpec((None, IDX_DIM, PAGE_SIZE), functools.partial(lambda s, pt, p: (pt[s, p], 0, 0), p=p))
                  for p in range(n_pages)]
    return pl.pallas_call(
        functools.partial(_sample_scores_body, n_pages=n_pages),
        grid_spec=pltpu.PrefetchScalarGridSpec(
            num_scalar_prefetch=1, grid=(db,),
            in_specs=[pl.BlockSpec((None, IDX_HEADS, IDX_DIM), per),
                      pl.BlockSpec((None, IDX_HEADS, 1), per),
                      pl.BlockSpec((None, 1, IDX_DIM), per)] + page_specs,
            out_specs=pl.BlockSpec((None, 1, n_cols), per)),
        out_shape=jax.ShapeDtypeStruct((db, 1, n_cols), I32),
        compiler_params=_params("arbitrary"),
        name="sample_scores",
    )(page_table, qi3, w3, kn3, *([cache_kidx] * n_pages))


def _sample_select_body(key_ref, mask_ref, keys_scr, *, n_cols, topk):
    keys_scr[...] = key_ref[...].T
    _topk_mask(keys_scr, n_cols // LANES, LANES, topk)
    mask_ref[...] = pltpu.bitcast(keys_scr[...], F32).T


def _sample_select(keys, topk):
    n_rows, n_cols = keys.shape
    return pl.pallas_call(
        functools.partial(_sample_select_body, n_cols=n_cols, topk=topk),
        out_shape=jax.ShapeDtypeStruct((n_rows, n_cols), F32),
        scratch_shapes=[pltpu.VMEM((n_cols, n_rows), I32)],
        compiler_params=pltpu.CompilerParams(vmem_limit_bytes=VMEM_LIMIT),
        name="sample_select",
    )(keys)


def _sample_attn_body(pt_ref, q_ref, qt_ref, kn_ref, vnt_ref, mask_ref, bias_ref, *rest, n_pages):
    kt_refs, vt_refs, o_ref = rest[:n_pages], rest[n_pages:2 * n_pages], rest[2 * n_pages]
    past = n_pages * PAGE_SIZE
    mask = mask_ref[...]
    s_self_all = jnp.sum(q_ref[...] * kn_ref[...], axis=1, keepdims=True)
    for h in range(N_HEADS):
        q_col = qt_ref[:, h:h + 1]
        bias = bias_ref[h]
        s = jnp.concatenate([jnp.sum(kt_refs[p][h] * q_col, axis=0, keepdims=True) for p in range(n_pages)], axis=1)
        s = s + bias[:, :past] + mask[:, :past]
        s_self = s_self_all[h:h + 1, :] + bias[:, past:past + 1] + mask[:, past:past + 1]
        m = jnp.maximum(jnp.max(s, axis=1, keepdims=True), s_self)
        p_past = jnp.exp(s - m)
        p_self = jnp.exp(s_self - m)
        l = jnp.sum(p_past, axis=1, keepdims=True) + p_self
        acc = vt_refs[0][h] * p_past[:, :PAGE_SIZE]
        for p in range(1, n_pages):
            acc = acc + vt_refs[p][h] * p_past[:, p * PAGE_SIZE:(p + 1) * PAGE_SIZE]
        out = jnp.sum(acc, axis=1, keepdims=True) + p_self * vnt_ref[:, h:h + 1]
        o_ref[:, h:h + 1] = out / l


def _sample_attn(page_table, q3, qt3, kn3, vnt3, mask3, bias_row, cache_kt, cache_vt):
    db, n_pages = page_table.shape
    n_cols = n_pages * PAGE_SIZE + LANES
    per = lambda s, pt: (s, 0, 0)
    page_specs = [pl.BlockSpec((None, N_HEADS, HEAD_DIM, PAGE_SIZE),
                               functools.partial(lambda s, pt, p: (pt[s, p], 0, 0, 0), p=p)) for p in range(n_pages)]
    return pl.pallas_call(
        functools.partial(_sample_attn_body, n_pages=n_pages),
        grid_spec=pltpu.PrefetchScalarGridSpec(
            num_scalar_prefetch=1, grid=(db,),
            in_specs=[pl.BlockSpec((None, N_HEADS, HEAD_DIM), per),
                      pl.BlockSpec((None, HEAD_DIM, N_HEADS), per),
                      pl.BlockSpec((None, N_HEADS, HEAD_DIM), per),
                      pl.BlockSpec((None, HEAD_DIM, N_HEADS), per),
                      pl.BlockSpec((None, 1, n_cols), per),
                      pl.BlockSpec((N_HEADS, 1, n_cols), lambda s, pt: (0, 0, 0))] + page_specs + page_specs,
            out_specs=pl.BlockSpec((None, HEAD_DIM, N_HEADS), per)),
        out_shape=jax.ShapeDtypeStruct((db, HEAD_DIM, N_HEADS), F32),
        compiler_params=_params("arbitrary"),
        name="sample_attn",
    )(page_table, q3, qt3, kn3, vnt3, mask3, bias_row, *([cache_kt] * n_pages), *([cache_vt] * n_pages))


def _cmul(ar, ai, br, bi):
    return ar * br - ai * bi, ar * bi + ai * br


def _ssm_prep_body(lre_f, lim_f, ldt_f, lre_r, lim_r, ldt_r, bre_ref, bim_ref, pre_ref, pim_ref, bbr_ref, bbi_ref,
                   *, chunk):
    def lam_bar(lre, lim, ldt):
        dt = jnp.exp(ldt)
        mag = jnp.exp(lre * dt)
        return mag * jnp.cos(lim * dt), mag * jnp.sin(lim * dt)

    lbr, lbi = lam_bar(lre_f[...], lim_f[...], ldt_f[...])
    pr = jnp.broadcast_to(lbr, (chunk, N_STATE))
    pi = jnp.broadcast_to(lbi, (chunk, N_STATE))
    row = lax.broadcasted_iota(I32, (chunk, N_STATE), 0)
    d = 1
    while d < chunk:
        sr = jnp.where(row >= d, pltpu.roll(pr, d, 0), 1.0)
        si = jnp.where(row >= d, pltpu.roll(pi, d, 0), 0.0)
        pr, pi = _cmul(pr, pi, sr, si)
        d *= 2
    pre_ref[...] = pr
    pim_ref[...] = pi

    lre, lim = lre_r[...], lim_r[...]
    lbr, lbi = lam_bar(lre, lim, ldt_r[...])
    den = lre * lre + lim * lim
    nr, ni = lbr - 1.0, lbi
    cr = (nr * lre + ni * lim) / den
    ci = (ni * lre - nr * lim) / den
    bbr, bbi = _cmul(cr, ci, bre_ref[...], bim_ref[...])
    bbr_ref[...] = bbr
    bbi_ref[...] = bbi


def _ssm_prep(lam_re, lam_im, log_dt, b_re, b_im, chunk):
    flat = lambda a: a.reshape(1, N_STATE)
    rep = lambda a: jnp.repeat(a, GROUP, axis=0)
    ldt = jnp.broadcast_to(log_dt[:, None], (N_GROUPS, STATE_DIM))
    bt = lambda b: jnp.swapaxes(b, 1, 2).reshape(SSM_WIDTH, STATE_DIM)
    return pl.pallas_call(
        functools.partial(_ssm_prep_body, chunk=chunk),
        out_shape=[jax.ShapeDtypeStruct((chunk, N_STATE), F32), jax.ShapeDtypeStruct((chunk, N_STATE), F32),
                   jax.ShapeDtypeStruct((SSM_WIDTH, STATE_DIM), F32), jax.ShapeDtypeStruct((SSM_WIDTH, STATE_DIM), F32)],
        compiler_params=pltpu.CompilerParams(vmem_limit_bytes=VMEM_LIMIT),
        name="ssm_prep",
    )(flat(lam_re), flat(lam_im), flat(ldt), rep(lam_re), rep(lam_im), rep(ldt), bt(b_re), bt(b_im))


def _block_diag(blocks):
    g, r, c = blocks.shape
    eye = jnp.eye(g, dtype=blocks.dtype)
    return (blocks[:, :, None, :] * eye[:, None, :, None]).reshape(g * r, g * c)


def _ssm_readout(u, xr, xi, wc_ref, dskip_ref, wglu_ref, bglu_ref):
    y = _dot(jnp.concatenate([xr, xi], axis=1).astype(BF16), wc_ref[...]) + dskip_ref[...] * u
    g = jax.nn.gelu(y)
    return g * jax.nn.sigmoid(_dot(g.astype(BF16), wglu_ref[...]) + bglu_ref[...])


def _ssm_scan_body(u_ref, x0r_ref, x0i_ref, pre_ref, pim_ref, wb_ref, wc_ref, dskip_ref, wglu_ref, bglu_ref,
                   y_ref, sr_ref, si_ref, cr_ref, ci_ref, xr_ref, xi_ref, *, chunk):
    c = pl.program_id(1)

    @pl.when(c == 0)
    def _():
        cr_ref[...] = x0r_ref[0]
        ci_ref[...] = x0i_ref[0]

    u = u_ref[...]
    bu = _dot(u.astype(BF16), wb_ref[...])
    n_groups = chunk // SCAN_GROUP
    xr = bu[:, :N_STATE].reshape(n_groups, SCAN_GROUP, N_STATE)
    xi = bu[:, N_STATE:].reshape(n_groups, SCAN_GROUP, N_STATE)
    pos = lax.broadcasted_iota(I32, (SCAN_GROUP, N_STATE), 0)
    d = 1
    while d < SCAN_GROUP:
        ar = jnp.where(pos >= d, pre_ref[d - 1:d, :], 0.0)
        ai = jnp.where(pos >= d, pim_ref[d - 1:d, :], 0.0)
        tr, ti = _cmul(ar, ai, pltpu.roll(xr, d, 1), pltpu.roll(xi, d, 1))
        xr, xi = xr + tr, xi + ti
        d *= 2
    pr, pi = pre_ref[...], pim_ref[...]
    cr, ci = cr_ref[...], ci_ref[...]
    for g in range(n_groups):
        tr, ti = _cmul(pr, pi, cr, ci)
        gr, gi = xr[g] + tr, xi[g] + ti
        xr_ref[g * SCAN_GROUP:(g + 1) * SCAN_GROUP, :] = gr
        xi_ref[g * SCAN_GROUP:(g + 1) * SCAN_GROUP, :] = gi
        cr, ci = gr[SCAN_GROUP - 1:, :], gi[SCAN_GROUP - 1:, :]
    cr_ref[...] = cr
    ci_ref[...] = ci
    sr_ref[0] = cr
    si_ref[0] = ci
    y_ref[...] = _ssm_readout(u, xr_ref[...], xi_ref[...], wc_ref, dskip_ref, wglu_ref, bglu_ref)


def _ssm_scan(u, x0r, x0i, pre, pim, wb, wc, dskip, wglu, bglu, bsz, t, chunk):
    nc = t // chunk
    row = lambda b, c: (b * nc + c, 0)
    const = lambda b, c: (0, 0)
    state = pl.BlockSpec((1, 1, N_STATE), lambda b, c: (b, 0, 0))
    return pl.pallas_call(
        functools.partial(_ssm_scan_body, chunk=chunk),
        grid=(bsz, nc),
        in_specs=[pl.BlockSpec((chunk, SSM_WIDTH), row), state, state,
                  pl.BlockSpec((SCAN_GROUP, N_STATE), const), pl.BlockSpec((SCAN_GROUP, N_STATE), const),
                  pl.BlockSpec((SSM_WIDTH, 2 * N_STATE), const), pl.BlockSpec((2 * N_STATE, SSM_WIDTH), const),
                  pl.BlockSpec((1, SSM_WIDTH), const), pl.BlockSpec((SSM_WIDTH, SSM_WIDTH), const),
                  pl.BlockSpec((1, SSM_WIDTH), const)],
        out_specs=[pl.BlockSpec((chunk, SSM_WIDTH), row), state, state],
        out_shape=[jax.ShapeDtypeStruct((bsz * t, SSM_WIDTH), F32),
                   jax.ShapeDtypeStruct((bsz, 1, N_STATE), F32), jax.ShapeDtypeStruct((bsz, 1, N_STATE), F32)],
        scratch_shapes=[pltpu.VMEM((1, N_STATE), F32), pltpu.VMEM((1, N_STATE), F32),
                        pltpu.VMEM((chunk, N_STATE), F32), pltpu.VMEM((chunk, N_STATE), F32)],
        compiler_params=_params("arbitrary", "arbitrary"),
        name="ssm_scan",
    )(u, x0r, x0i, pre, pim, wb, wc, dskip, wglu, bglu)


def _ssm_step_body(u_ref, x0r_ref, x0i_ref, pre_ref, pim_ref, wb_ref, wc_ref, dskip_ref, wglu_ref, bglu_ref,
                   y_ref, sr_ref, si_ref):
    u = u_ref[...]
    bu = _dot(u.astype(BF16), wb_ref[...])
    tr, ti = _cmul(pre_ref[0:1, :], pim_ref[0:1, :], x0r_ref[...], x0i_ref[...])
    xr, xi = bu[:, :N_STATE] + tr, bu[:, N_STATE:] + ti
    sr_ref[...] = xr
    si_ref[...] = xi
    y_ref[...] = _ssm_readout(u, xr, xi, wc_ref, dskip_ref, wglu_ref, bglu_ref)


def _ssm_step(u, x0r, x0i, pre, pim, wb, wc, dskip, wglu, bglu):
    n = u.shape[0]
    return pl.pallas_call(
        _ssm_step_body,
        out_shape=[jax.ShapeDtypeStruct((n, SSM_WIDTH), F32),
                   jax.ShapeDtypeStruct((n, N_STATE), F32), jax.ShapeDtypeStruct((n, N_STATE), F32)],
        compiler_params=pltpu.CompilerParams(vmem_limit_bytes=VMEM_LIMIT),
        name="ssm_step",
    )(u, x0r, x0i, pre, pim, wb, wc, dskip, wglu, bglu)


def _merge_body(x_ref, ya_ref, ys_ref, sga_ref, sgs_ref, wa_ref, ws_ref, wo_ref, gpost_ref, gpre_ref,
                x1_ref, h2_ref):
    merged = (sga_ref[...] * _dot(ya_ref[...].astype(BF16), wa_ref[...])
              + sgs_ref[...] * _dot(ys_ref[...].astype(BF16), ws_ref[...]))
    x1 = x_ref[...] + _rms(_dot(merged.astype(BF16), wo_ref[...]), gpost_ref[...])
    x1_ref[...] = x1
    h2_ref[...] = _rms(x1, gpre_ref[...]).astype(BF16)


def _merge(x2, ya, ys, sga, sgs, wa, ws, wo, gpost, gpre, tm):
    m = x2.shape[0]
    row = lambda i: (i, 0)
    const = lambda i: (0, 0)
    return pl.pallas_call(
        _merge_body,
        grid=(m // tm,),
        in_specs=[pl.BlockSpec((tm, D_MODEL), row), pl.BlockSpec((tm, ATT_WIDTH), row),
                  pl.BlockSpec((tm, SSM_WIDTH), row), pl.BlockSpec((tm, D_MODEL), row),
                  pl.BlockSpec((tm, D_MODEL), row),
                  pl.BlockSpec((ATT_WIDTH, D_MODEL), const), pl.BlockSpec((SSM_WIDTH, D_MODEL), const),
                  pl.BlockSpec((D_MODEL, D_MODEL), const), pl.BlockSpec((1, D_MODEL), const),
                  pl.BlockSpec((1, D_MODEL), const)],
        out_specs=[pl.BlockSpec((tm, D_MODEL), row), pl.BlockSpec((tm, D_MODEL), row)],
        out_shape=[jax.ShapeDtypeStruct((m, D_MODEL), F32), jax.ShapeDtypeStruct((m, D_MODEL), BF16)],
        compiler_params=_params("arbitrary"),
        name="merge",
    )(x2, ya, ys, sga, sgs, wa, ws, wo, gpost, gpre)


def _ffn_body(h_ref, x_ref, wua_ref, wub_ref, cwa_ref, cwb_ref, cba_ref, cbb_ref, wd_ref, g_ref, pa_ref, pb_ref,
              y_ref, oa_ref, ob_ref, f_ref, *carry, seq, tiles_per_seq):
    i, j = pl.program_id(0), pl.program_id(1)
    nj = pl.num_programs(1)
    h = h_ref[...]
    tm = h.shape[0]

    def conv(up, cw_ref, cb_ref, prev_ref, carry_ref, out_ref):
        if seq:
            @pl.when(i % tiles_per_seq == 0)
            def _():
                carry_ref[j, 0:CONV_W - 1, :] = prev_ref[0]
            row = lax.broadcasted_iota(I32, up.shape, 0)
            before1 = carry_ref[j, 1:2, :]
            before2 = carry_ref[j, 0:1, :]
            m1 = jnp.where(row == 0, before1, pltpu.roll(up, 1, 0))
            m2 = jnp.where(row == 0, before2, jnp.where(row == 1, before1, pltpu.roll(up, 2, 0)))
            tail = up[tm - (CONV_W - 1):, :]
            carry_ref[j, 0:CONV_W - 1, :] = tail
            out_ref[0] = tail
        else:
            m2, m1 = prev_ref[:, 0, :], prev_ref[:, 1, :]
            out_ref[...] = up
        return cb_ref[...] + m2 * cw_ref[0:1, :] + m1 * cw_ref[1:2, :] + up * cw_ref[2:3, :]

    ca, cb = carry if seq else (None, None)
    a = conv(_dot(h, wua_ref[...]), cwa_ref, cba_ref, pa_ref, ca, oa_ref)
    b = conv(_dot(h, wub_ref[...]), cwb_ref, cbb_ref, pb_ref, cb, ob_ref)
    part = _dot((jax.nn.gelu(a) * b).astype(BF16), wd_ref[...])

    @pl.when(j == 0)
    def _():
        f_ref[...] = part

    @pl.when(j > 0)
    def _():
        f_ref[...] += part

    @pl.when(j == nj - 1)
    def _():
        y_ref[...] = x_ref[...] + _rms(f_ref[...], g_ref[...])


def _ffn(h2, x1, w_up, conv_w, conv_b, w_down, g_post, conv_prev, bsz, t, tm, tf, seq):
    m = bsz * t
    nj = D_FF // tf
    row = lambda i, j: (i, 0)
    if seq:
        tiles_per_seq = t // tm
        prev_a = pl.BlockSpec((1, CONV_W - 1, tf), lambda i, j: (i // tiles_per_seq, 0, j))
        prev_b = pl.BlockSpec((1, CONV_W - 1, tf), lambda i, j: (i // tiles_per_seq, 0, j + nj))
        out_tail = pl.BlockSpec((1, CONV_W - 1, tf), lambda i, j: (i, 0, j))
        tail_shape = jax.ShapeDtypeStruct((m // tm, CONV_W - 1, D_FF), F32)
        scratch = [pltpu.VMEM((nj, 8, tf), F32), pltpu.VMEM((nj, 8, tf), F32)]
    else:
        tiles_per_seq = 1
        prev_a = pl.BlockSpec((tm, CONV_W - 1, tf), lambda i, j: (i, 0, j))
        prev_b = pl.BlockSpec((tm, CONV_W - 1, tf), lambda i, j: (i, 0, j + nj))
        out_tail = pl.BlockSpec((tm, tf), lambda i, j: (i, j))
        tail_shape = jax.ShapeDtypeStruct((m, D_FF), F32)
        scratch = []
    return pl.pallas_call(
        functools.partial(_ffn_body, seq=seq, tiles_per_seq=tiles_per_seq),
        grid=(m // tm, nj),
        in_specs=[pl.BlockSpec((tm, D_MODEL), row), pl.BlockSpec((tm, D_MODEL), row),
                  pl.BlockSpec((D_MODEL, tf), lambda i, j: (0, j)),
                  pl.BlockSpec((D_MODEL, tf), lambda i, j: (0, j + nj)),
                  pl.BlockSpec((CONV_W, tf), lambda i, j: (0, j)),
                  pl.BlockSpec((CONV_W, tf), lambda i, j: (0, j + nj)),
                  pl.BlockSpec((1, tf), lambda i, j: (0, j)),
                  pl.BlockSpec((1, tf), lambda i, j: (0, j + nj)),
                  pl.BlockSpec((tf, D_MODEL), lambda i, j: (j, 0)),
                  pl.BlockSpec((1, D_MODEL), lambda i, j: (0, 0)),
                  prev_a, prev_b],
        out_specs=[pl.BlockSpec((tm, D_MODEL), row), out_tail, out_tail],
        out_shape=[jax.ShapeDtypeStruct((m, D_MODEL), F32), tail_shape, tail_shape],
        scratch_shapes=[pltpu.VMEM((tm, D_MODEL), F32)] + scratch,
        compiler_params=_params("arbitrary", "arbitrary"),
        name="conv_ffn",
    )(h2, x1, w_up, w_up, conv_w, conv_w, conv_b, conv_b, w_down, g_post, conv_prev, conv_prev)


def _pack_w_in(w_in):
    points = np.cumsum(SPLITS)[:-1].tolist()
    wq, wk, wv, wqi, wki, wwi, wu, wga, wgs = jnp.split(w_in, points, axis=-1)
    pad = jnp.zeros((D_MODEL, LANES - IDX_DIM - IDX_HEADS), w_in.dtype)
    return jnp.concatenate([wq, wk, wv, wqi, wki, wwi, pad, wu, wga, wgs], axis=-1).astype(BF16)


def _layer_weights(lw):
    (w_in, g_pre_mix, g_post_mix, lam_re, lam_im, log_dt, b_re, b_im, c_re, c_im, d_skip,
     w_glu, b_glu, w_att_out, w_ssm_out, w_o, g_pre_ffn, g_post_ffn, w_up, conv_w, conv_b, w_down) = lw
    pre, pim, bbr, bbi = _ssm_prep(lam_re, lam_im, log_dt, b_re, b_im, SCAN_GROUP)
    wb = jnp.concatenate([_block_diag(bbr.reshape(N_GROUPS, GROUP, STATE_DIM)),
                          _block_diag(bbi.reshape(N_GROUPS, GROUP, STATE_DIM))], axis=1).astype(BF16)
    wc = jnp.concatenate([_block_diag(jnp.swapaxes(c_re, 1, 2)),
                          _block_diag(-jnp.swapaxes(c_im, 1, 2))], axis=0).astype(BF16)
    vec = lambda a: a.reshape(1, -1)
    return dict(
        w_in=_pack_w_in(w_in), g_pre_mix=vec(g_pre_mix), g_post_mix=vec(g_post_mix),
        pre=pre, pim=pim, wb=wb, wc=wc, d_skip=vec(d_skip), w_glu=w_glu.astype(BF16), b_glu=vec(b_glu),
        w_att_out=w_att_out.astype(BF16), w_ssm_out=w_ssm_out.astype(BF16), w_o=w_o.astype(BF16),
        g_pre_ffn=vec(g_pre_ffn), g_post_ffn=vec(g_post_ffn), w_up=w_up.astype(BF16), conv_w=conv_w,
        conv_b=vec(conv_b), w_down=w_down.astype(BF16))


def _prompt_layer(x, w, bias_tiles, tq, tm, chunk, tf):
    bsz, t, _ = x.shape
    x2 = x.reshape(bsz * t, D_MODEL)
    k, v, kw, u, sga, sgs, qt, qit, kwt, kh, vt, ki = _in_proj(x2, w["g_pre_mix"], w["w_in"], bsz, t, tm, True)
    y_att = _prompt_attn(qt, qit, kwt, kh, vt, ki, bias_tiles, bsz, t, tq, min(TOPK_MAX, t // 4))
    zero_state = jnp.zeros((bsz, 1, N_STATE), F32)
    y_ssm, s_re, s_im = _ssm_scan(u, zero_state, zero_state, w["pre"], w["pim"], w["wb"], w["wc"], w["d_skip"],
                                  w["w_glu"], w["b_glu"], bsz, t, chunk)
    x1, h2 = _merge(x2, y_att, y_ssm, sga, sgs, w["w_att_out"], w["w_ssm_out"], w["w_o"],
                    w["g_post_mix"], w["g_pre_ffn"], tm)
    zero_conv = jnp.zeros((bsz, CONV_W - 1, 2 * D_FF), F32)
    y, tail_a, tail_b = _ffn(h2, x1, w["w_up"], w["conv_w"], w["conv_b"], w["w_down"], w["g_post_ffn"],
                             zero_conv, bsz, t, tm, tf, True)
    state = (k.reshape(bsz, t, N_HEADS, HEAD_DIM), v.reshape(bsz, t, N_HEADS, HEAD_DIM),
             kw[:, :IDX_DIM].reshape(bsz, t, IDX_DIM),
             s_re.reshape(bsz, N_GROUPS, STATE_DIM), s_im.reshape(bsz, N_GROUPS, STATE_DIM),
             jnp.concatenate([tail_a, tail_b], axis=-1)[t // tm - 1::t // tm])
    return y.reshape(bsz, t, D_MODEL), state


def _sample_layer(x, w, bias_row, cache_k, cache_v, cache_kidx, page_table, st_re, st_im, st_conv, tf):
    db, tq, _ = x.shape
    assert tq == 1, "the sample group is decoded one token per sequence"
    n_pool = cache_k.shape[0]
    past = page_table.shape[1] * PAGE_SIZE
    x2 = x.reshape(db, D_MODEL)
    k, v, kw, u, sga, sgs, q, qi = _in_proj(x2, w["g_pre_mix"], w["w_in"], 1, db, db, False)
    keys = _sample_scores(page_table, qi.reshape(db, IDX_HEADS, IDX_DIM),
                          kw[:, IDX_DIM:IDX_DIM + IDX_HEADS].reshape(db, IDX_HEADS, 1),
                          kw[:, :IDX_DIM].reshape(db, 1, IDX_DIM), jnp.swapaxes(cache_kidx, 1, 2))
    mask = _sample_select(keys.reshape(db, past + LANES), min(TOPK_MAX, (past + tq) // 4))
    heads = lambda a: a.astype(F32).reshape(db, N_HEADS, HEAD_DIM)
    heads_t = lambda a: jnp.swapaxes(heads(a), 1, 2)
    pages_t = lambda c: jnp.transpose(c, (0, 2, 3, 1))
    y_att_t = _sample_attn(page_table, heads(q), heads_t(q), heads(k), heads_t(v), mask[:, None, :], bias_row,
                           pages_t(cache_k), pages_t(cache_v))
    y_att = jnp.swapaxes(y_att_t, 1, 2).reshape(db, ATT_WIDTH)
    y_ssm, s_re, s_im = _ssm_step(u, st_re.reshape(db, N_STATE), st_im.reshape(db, N_STATE), w["pre"], w["pim"],
                                  w["wb"], w["wc"], w["d_skip"], w["w_glu"], w["b_glu"])
    x1, h2 = _merge(x2, y_att, y_ssm, sga, sgs, w["w_att_out"], w["w_ssm_out"], w["w_o"],
                    w["g_post_mix"], w["g_pre_ffn"], db)
    y, up_a, up_b = _ffn(h2, x1, w["w_up"], w["conv_w"], w["conv_b"], w["w_down"], w["g_post_ffn"],
                         st_conv, db, 1, db, tf, False)
    conv_new = jnp.concatenate([st_conv[:, 1:], jnp.concatenate([up_a, up_b], axis=-1)[:, None, :]], axis=1)
    state = (k.reshape(db, 1, N_HEADS, HEAD_DIM), v.reshape(db, 1, N_HEADS, HEAD_DIM),
             kw[:, :IDX_DIM].reshape(db, 1, IDX_DIM),
             s_re.reshape(db, N_GROUPS, STATE_DIM), s_im.reshape(db, N_GROUPS, STATE_DIM), conv_new)
    return y.reshape(db, 1, D_MODEL), state


def kernel(x_prompt, x_sample, cache_k, cache_v, cache_kidx, state_ssm_re, state_ssm_im, state_conv, page_table,
           rel_bias, w_in, g_pre_mix, g_post_mix, lam_re, lam_im, log_dt, b_re, b_im, c_re, c_im, d_skip, w_glu,
           b_glu, w_att_out, w_ssm_out, w_o, g_pre_ffn, g_post_ffn, w_up, conv_w, conv_b, w_down):
    depth = w_in.shape[0]
    t = x_prompt.shape[1]
    past = page_table.shape[1] * PAGE_SIZE
    tq = min(256, t)
    tm = min(512, t)
    chunk = min(256, t)
    tf = 512
    bias_tiles, bias_row = _bias_tiles(rel_bias, tq, past)
    y_p, y_s = x_prompt, x_sample
    outs_p, outs_s = [], []
    for l in range(depth):
        lw = (w_in[l], g_pre_mix[l], g_post_mix[l], lam_re[l], lam_im[l], log_dt[l], b_re[l], b_im[l],
              c_re[l], c_im[l], d_skip[l], w_glu[l], b_glu[l], w_att_out[l], w_ssm_out[l], w_o[l],
              g_pre_ffn[l], g_post_ffn[l], w_up[l], conv_w[l], conv_b[l], w_down[l])
        w = _layer_weights(lw)
        y_p, st_p = _prompt_layer(y_p, w, bias_tiles, tq, tm, chunk, tf)
        y_s, st_s = _sample_layer(y_s, w, bias_row, cache_k[l], cache_v[l], cache_kidx[l], page_table,
                                  state_ssm_re[l], state_ssm_im[l], state_conv[l], tf)
        outs_p.append(st_p)
        outs_s.append(st_s)
    k_p, v_p, ki_p, sr_p, si_p, cv_p = [jnp.stack(a) for a in zip(*outs_p)]
    k_s, v_s, ki_s, sr_s, si_s, cv_s = [jnp.stack(a) for a in zip(*outs_s)]
    return (y_p, y_s, k_p, v_p, ki_p, sr_p, si_p, cv_p, k_s, v_s, ki_s, sr_s, si_s, cv_s)
```

```python
import functools
import math

import numpy as np
import jax
import jax.numpy as jnp
from jax import lax
from jax.experimental import pallas as pl
from jax.experimental.pallas import tpu as pltpu

F32 = jnp.float32
BF16 = jnp.bfloat16
I32 = jnp.int32

D_MODEL = 1024
PAGE_SIZE = 128
N_HEADS = 8
HEAD_DIM = 64
ATT_WIDTH = N_HEADS * HEAD_DIM
IDX_HEADS = 4
IDX_DIM = 64
TOPK_MAX = 256
N_BUCKETS = 32
MAX_DISTANCE = 128
SSM_WIDTH = 512
GROUP = 16
N_GROUPS = SSM_WIDTH // GROUP
STATE_DIM = 64
N_STATE = N_GROUPS * STATE_DIM
D_FF = 4 * D_MODEL
CONV_W = 3
EPS = 1e-6
SPLITS = (ATT_WIDTH, ATT_WIDTH, ATT_WIDTH, IDX_HEADS * IDX_DIM, IDX_DIM, IDX_HEADS, SSM_WIDTH, D_MODEL, D_MODEL)

LANES = 128
KEY_EXCLUDED = -(2 ** 31)
CODE_NEVER = 2 ** 30
IDX_BITS = 14
BF16_SUBLANES = 16
V_ROWS = HEAD_DIM + BF16_SUBLANES
SCAN_GROUP = 8
MASKED = -1e30
VMEM_LIMIT = 56 * 1024 * 1024

_C_Q, _C_K, _C_V = 0, ATT_WIDTH, 2 * ATT_WIDTH
_C_QI = 3 * ATT_WIDTH
_C_KW = _C_QI + IDX_HEADS * IDX_DIM
_C_U = _C_KW + LANES
_C_GA = _C_U + SSM_WIDTH
_C_GS = _C_GA + D_MODEL
_C_END = _C_GS + D_MODEL


def _params(*sem):
    return pltpu.CompilerParams(dimension_semantics=sem, vmem_limit_bytes=VMEM_LIMIT)


def _rms(x, g):
    inv = lax.rsqrt(jnp.mean(x * x, axis=-1, keepdims=True) + EPS)
    return (x * inv) * g


def _dot(a, b):
    return jnp.dot(a, b, preferred_element_type=F32)


def _dot_nt(a, b):
    return lax.dot_general(a, b, (((1,), (1,)), ((), ())), preferred_element_type=F32)


def _bucket_starts():
    n = np.arange(0, 1 << IDX_BITS, dtype=np.int32)
    max_exact = N_BUCKETS // 2
    nf = np.maximum(n, 1).astype(np.float32)
    large = max_exact + (np.log(nf / np.float32(max_exact)) / np.float32(math.log(MAX_DISTANCE / max_exact))
                         * np.float32(N_BUCKETS - max_exact)).astype(np.int32)
    large = np.minimum(large, N_BUCKETS - 1)
    bucket = np.where(n < max_exact, n, large)
    assert np.all(np.diff(bucket) >= 0)
    starts = [int(np.argmax(bucket >= b)) for b in range(N_BUCKETS)]
    assert all(bucket[s] == b for b, s in enumerate(starts))
    return starts


_BUCKET_START = _bucket_starts()


def _bias_tiles_body(rb_ref, tile_ref, row_ref, *, tq, past):
    h = pl.program_id(0)

    def bias_of(dist):
        val = jnp.full(dist.shape, rb_ref[0, h], F32)
        for b in range(1, N_BUCKETS):
            val = jnp.where(dist >= _BUCKET_START[b], rb_ref[b, h], val)
        return val

    key_off = lax.broadcasted_iota(I32, (tq, tq), 0)
    qry_off = lax.broadcasted_iota(I32, (tq, tq), 1)
    far = rb_ref[N_BUCKETS - 1, h]
    tile_ref[0, 1] = bias_of(qry_off - key_off) - far
    tile_ref[0, 0] = bias_of(qry_off - key_off + tq) - far
    lane = lax.broadcasted_iota(I32, (1, past + LANES), 1)
    row_ref[0] = bias_of(past - lane)


def _bias_tiles(rel_bias, tq, past):
    assert tq + 1 >= _BUCKET_START[N_BUCKETS - 1]
    return pl.pallas_call(
        functools.partial(_bias_tiles_body, tq=tq, past=past),
        grid=(N_HEADS,),
        in_specs=[pl.BlockSpec(memory_space=pltpu.SMEM)],
        out_specs=[pl.BlockSpec((1, 2, tq, tq), lambda h: (h, 0, 0, 0)),
                   pl.BlockSpec((1, 1, past + LANES), lambda h: (h, 0, 0))],
        out_shape=[jax.ShapeDtypeStruct((N_HEADS, 2, tq, tq), F32),
                   jax.ShapeDtypeStruct((N_HEADS, 1, past + LANES), F32)],
        compiler_params=_params("arbitrary"),
        name="bias_tiles",
    )(rel_bias)


def _in_proj_body(x_ref, g_ref, w_ref, k_ref, v_ref, kw_ref, u_ref, sga_ref, sgs_ref, *more_refs, seq):
    h = _rms(x_ref[...], g_ref[...]).astype(BF16)

    def proj(c0, c1):
        return _dot(h, w_ref[:, c0:c1])

    q = proj(_C_Q, _C_K) * HEAD_DIM ** -0.5
    k = proj(_C_K, _C_V)
    v = proj(_C_V, _C_QI)
    qi = proj(_C_QI, _C_KW)
    kw = proj(_C_KW, _C_U)
    lane = lax.broadcasted_iota(I32, kw.shape, 1)
    kw = jnp.where(lane >= IDX_DIM, kw * IDX_HEADS ** -0.5, kw)
    kw_ref[...] = kw
    u_ref[...] = proj(_C_U, _C_GA)
    sga_ref[...] = jax.nn.sigmoid(proj(_C_GA, _C_GS))
    sgs_ref[...] = jax.nn.sigmoid(proj(_C_GS, _C_END))
    if seq:
        qt_ref, qit_ref, kwt_ref, kh_ref, vt_ref, ki_ref = more_refs
        tm = k.shape[0]
        vt = v.T.reshape(N_HEADS, HEAD_DIM, tm)
        k_ref[0] = k.T.reshape(N_HEADS, HEAD_DIM, tm)
        v_ref[0] = vt
        qt_ref[...] = q.T.reshape(N_HEADS, HEAD_DIM, tm).astype(BF16)
        qit_ref[...] = (qi * IDX_DIM ** -0.5).T.reshape(IDX_HEADS, IDX_DIM, tm).astype(BF16)
        kwt_ref[0] = kw.T
        for hd in range(N_HEADS):
            kh_ref[0, hd] = k[:, hd * HEAD_DIM:(hd + 1) * HEAD_DIM].astype(BF16)
        vt_ref[0, :, :HEAD_DIM, :] = vt.astype(BF16)
        vt_ref[0, :, HEAD_DIM:, :] = jnp.ones((N_HEADS, V_ROWS - HEAD_DIM, tm), BF16)
        ki_ref[0] = kw[:, :IDX_DIM].astype(BF16)
    else:
        q_ref, qi_ref = more_refs
        k_ref[...] = k
        v_ref[...] = v
        q_ref[...] = q.astype(BF16)
        qi_ref[...] = qi


def _in_proj(x2, g, w_packed, bsz, t, tm, seq):
    m = bsz * t
    nt = t // tm
    row = lambda b, i: (b * nt + i, 0)
    widths = (LANES, SSM_WIDTH, D_MODEL, D_MODEL)
    if seq:
        kv_spec = pl.BlockSpec((1, N_HEADS, HEAD_DIM, tm), lambda b, i: (b, 0, 0, i))
        kv_shape = jax.ShapeDtypeStruct((bsz, N_HEADS, HEAD_DIM, t), F32)
    else:
        kv_spec = pl.BlockSpec((tm, ATT_WIDTH), row)
        kv_shape = jax.ShapeDtypeStruct((m, ATT_WIDTH), F32)
    out_specs = [kv_spec, kv_spec] + [pl.BlockSpec((tm, w), row) for w in widths]
    out_shape = [kv_shape, kv_shape] + [jax.ShapeDtypeStruct((m, w), F32) for w in widths]
    if seq:
        col = lambda b, i: (0, 0, b * nt + i)
        out_specs += [pl.BlockSpec((N_HEADS, HEAD_DIM, tm), col),
                      pl.BlockSpec((IDX_HEADS, IDX_DIM, tm), col),
                      pl.BlockSpec((1, LANES, tm), lambda b, i: (b, 0, i)),
                      pl.BlockSpec((1, N_HEADS, tm, HEAD_DIM), lambda b, i: (b, 0, i, 0)),
                      pl.BlockSpec((1, N_HEADS, V_ROWS, tm), lambda b, i: (b, 0, 0, i)),
                      pl.BlockSpec((1, tm, IDX_DIM), lambda b, i: (b, i, 0))]
        out_shape += [jax.ShapeDtypeStruct((N_HEADS, HEAD_DIM, m), BF16),
                      jax.ShapeDtypeStruct((IDX_HEADS, IDX_DIM, m), BF16),
                      jax.ShapeDtypeStruct((bsz, LANES, t), F32),
                      jax.ShapeDtypeStruct((bsz, N_HEADS, t, HEAD_DIM), BF16),
                      jax.ShapeDtypeStruct((bsz, N_HEADS, V_ROWS, t), BF16),
                      jax.ShapeDtypeStruct((bsz, t, IDX_DIM), BF16)]
    else:
        out_specs += [pl.BlockSpec((tm, ATT_WIDTH), row), pl.BlockSpec((tm, IDX_HEADS * IDX_DIM), row)]
        out_shape += [jax.ShapeDtypeStruct((m, ATT_WIDTH), BF16),
                      jax.ShapeDtypeStruct((m, IDX_HEADS * IDX_DIM), F32)]
    return pl.pallas_call(
        functools.partial(_in_proj_body, seq=seq),
        grid=(bsz, nt),
        in_specs=[pl.BlockSpec((tm, D_MODEL), row),
                  pl.BlockSpec((1, D_MODEL), lambda b, i: (0, 0)),
                  pl.BlockSpec((D_MODEL, _C_END), lambda b, i: (0, 0))],
        out_specs=out_specs,
        out_shape=out_shape,
        compiler_params=_params("arbitrary", "arbitrary"),
        name="in_proj",
    )(x2, g, w_packed)


def _sortable_key(score):
    bits = pltpu.bitcast(score, I32)
    return jnp.where(bits < 0, bits ^ 0x7FFFFFFF, bits)


def _topk_mask(keys_ref, nblk, height, topk):
    n_lanes = keys_ref.shape[1]
    n_parts = height // 8
    assert n_parts * 8 == height and n_parts & (n_parts - 1) == 0

    def row_block(j):
        return pl.ds(pl.multiple_of(j * height, height), height)

    def count(pred, bound):
        def body(j, acc):
            hit = jnp.where(pred(keys_ref[row_block(j), :], bound), 1, 0)
            parts = [hit[r:r + 8] for r in range(0, height, 8)]
            while len(parts) > 1:
                parts = [a + b for a, b in zip(parts[::2], parts[1::2])]
            return acc + parts[0]
        acc = lax.fori_loop(0, nblk, body, jnp.zeros((8, n_lanes), I32))
        return jnp.sum(acc, axis=0, keepdims=True)

    zero = jnp.zeros((1, n_lanes), I32)

    prefix = jnp.where(count(lambda k, b: k >= b, zero) >= topk, 0, KEY_EXCLUDED).astype(I32)

    def value_bit(b, prefix):
        cand = prefix | jnp.left_shift(1, 30 - b)
        return jnp.where(count(lambda k, c: k >= c, cand) >= topk, cand, prefix)
    thr = lax.fori_loop(0, 31, value_bit, prefix)

    def encode(j, _):
        k = keys_ref[row_block(j), :]
        pos = j * height + lax.broadcasted_iota(I32, (height, n_lanes), 0)
        code = jnp.where(k > thr, -1, jnp.where(k == thr, pos, CODE_NEVER))
        keys_ref[row_block(j), :] = jnp.where(k == KEY_EXCLUDED, CODE_NEVER, code)
        return 0
    lax.fori_loop(0, nblk, encode, 0)

    def index_bit(b, bound):
        cand = bound | jnp.left_shift(1, IDX_BITS - 1 - b)
        return jnp.where(count(lambda c, x: c < x, cand) <= topk, cand, bound)
    bound = lax.fori_loop(0, IDX_BITS, index_bit, zero)

    def emit(j, _):
        mask = jnp.where(keys_ref[row_block(j), :] < bound, 0.0, MASKED).astype(F32)
        keys_ref[row_block(j), :] = pltpu.bitcast(mask, I32)
        return 0
    lax.fori_loop(0, nblk, emit, 0)


def _prompt_attn_body(qt_ref, qit_ref, wt_ref, kh_ref, vt_ref, ki_ref, bias_ref, o_ref, keys_ref, m_ref, acc_ref,
                      alpha_ref, p_ref, *, tq, topk):
    i = pl.program_id(1)
    nblk = i + 1
    key_off = lax.broadcasted_iota(I32, (tq, tq), 0)
    qry_off = lax.broadcasted_iota(I32, (tq, tq), 1)

    def key_block(j):
        return pl.ds(pl.multiple_of(j * tq, tq), tq)

    qit = [qit_ref[h] for h in range(IDX_HEADS)]
    wt = [wt_ref[0, h:h + 1, :] for h in range(IDX_HEADS)]

    def score_block(j, _, diagonal):
        ki = ki_ref[0, key_block(j), :]
        s = None
        for h in range(IDX_HEADS):
            sh = jnp.maximum(_dot(ki, qit[h]), 0.0) * wt[h]
            s = sh if s is None else s + sh
        key = _sortable_key(s)
        if diagonal:
            key = jnp.where(key_off <= qry_off, key, KEY_EXCLUDED)
        keys_ref[key_block(j), :] = key
        return 0
    lax.fori_loop(0, i, functools.partial(score_block, diagonal=False), 0)
    score_block(i, 0, diagonal=True)

    _topk_mask(keys_ref, nblk, tq, topk)

    m_ref[...] = jnp.full(m_ref.shape, MASKED, F32)
    acc_ref[...] = jnp.zeros(acc_ref.shape, F32)
    p_ref[...] = jnp.zeros(p_ref.shape, BF16)
    alpha_ref[...] = jnp.ones(alpha_ref.shape, F32)

    def accumulate(j):
        for h in range(N_HEADS):
            acc_ref[h] = alpha_ref[h] * acc_ref[h] + _dot(vt_ref[0, h, :, key_block(j)], p_ref[h])

    def attend(j, _, near):
        accumulate(jnp.maximum(j - 1, 0))
        masked = pltpu.bitcast(keys_ref[key_block(j), :], F32)
        for h in range(N_HEADS):
            s = _dot(kh_ref[0, h, key_block(j), :], qt_ref[h]) + masked
            if near:
                s = s + bias_ref[h, j - i + 1]
            m_old = m_ref[h]
            m_new = jnp.maximum(m_old, jnp.max(s, axis=0, keepdims=True))
            p_ref[h] = jnp.exp(s - m_new).astype(BF16)
            alpha_ref[h] = jnp.exp(m_old - m_new)
            m_ref[h] = m_new
        return 0

    n_far = jnp.maximum(i - 1, 0)
    lax.fori_loop(0, n_far, functools.partial(attend, near=False), 0)
    lax.fori_loop(n_far, nblk, functools.partial(attend, near=True), 0)
    accumulate(i)
    out_t = jnp.concatenate(
        [acc_ref[h, :HEAD_DIM, :] / acc_ref[h, HEAD_DIM:HEAD_DIM + 1, :] for h in range(N_HEADS)], axis=0)
    o_ref[...] = out_t.T


def _prompt_attn(qt, qit, kwt, kh, vt, ki, bias_tiles, bsz, t, tq, topk):
    nq = t // tq
    col = lambda b, i: (0, 0, b * nq + i)
    whole = dict(pipeline_mode=pl.Buffered(1))
    return pl.pallas_call(
        functools.partial(_prompt_attn_body, tq=tq, topk=topk),
        grid=(bsz, nq),
        in_specs=[pl.BlockSpec((N_HEADS, HEAD_DIM, tq), col),
                  pl.BlockSpec((IDX_HEADS, IDX_DIM, tq), col),
                  pl.BlockSpec((1, 8, tq), lambda b, i: (b, IDX_DIM // 8, i)),
                  pl.BlockSpec((1, N_HEADS, t, HEAD_DIM), lambda b, i: (b, 0, 0, 0), **whole),
                  pl.BlockSpec((1, N_HEADS, V_ROWS, t), lambda b, i: (b, 0, 0, 0), **whole),
                  pl.BlockSpec((1, t, IDX_DIM), lambda b, i: (b, 0, 0), **whole),
                  pl.BlockSpec((N_HEADS, 2, tq, tq), lambda b, i: (0, 0, 0, 0), **whole)],
        out_specs=pl.BlockSpec((tq, ATT_WIDTH), lambda b, i: (b * nq + i, 0)),
        out_shape=jax.ShapeDtypeStruct((bsz * t, ATT_WIDTH), F32),
        scratch_shapes=[pltpu.VMEM((t, tq), I32),
                        pltpu.VMEM((N_HEADS, 1, tq), F32), pltpu.VMEM((N_HEADS, V_ROWS, tq), F32),
                        pltpu.VMEM((N_HEADS, 1, tq), F32), pltpu.VMEM((N_HEADS, tq, tq), BF16)],
        compiler_params=_params("arbitrary", "arbitrary"),
        name="prompt_attn",
    )(qt, qit, kwt, kh, vt, ki, bias_tiles)


def _sample_scores_body(pt_ref, qi_ref, w_ref, kn_ref, *rest, n_pages):
    page_refs, key_ref = rest[:n_pages], rest[n_pages]
    qi = qi_ref[...].astype(BF16)
    w = w_ref[...]

    def weighted(s):
        return jnp.sum(jnp.maximum(s * IDX_DIM ** -0.5, 0.0) * w, axis=0, keepdims=True)

    for p in range(n_pages):
        s = _dot(qi, page_refs[p][...].astype(BF16))
        key_ref[:, p * PAGE_SIZE:(p + 1) * PAGE_SIZE] = _sortable_key(weighted(s))
    s_self = jnp.sum(qi.astype(F32) * kn_ref[...].astype(BF16).astype(F32), axis=1, keepdims=True)
    lane = lax.broadcasted_iota(I32, (1, LANES), 1)
    key_ref[:, n_pages * PAGE_SIZE:] = jnp.where(lane == 0, _sortable_key(weighted(s_self)), KEY_EXCLUDED)


def _sample_scores(page_table, qi3, w3, kn3, cache_kidx):
    db, n_pages = page_table.shape
    n_cols = n_pages * PAGE_SIZE + LANES
    per = lambda s, pt: (s, 0, 0)
    page_specs = [pl.BlockSpec((None, IDX_DIM, PAGE_SIZE), functools.partial(lambda s, pt, p: (pt[s, p], 0, 0), p=p))
                  for p in range(n_pages)]
    return pl.pallas_call(
        functools.partial(_sample_scores_body, n_pages=n_pages),
        grid_spec=pltpu.PrefetchScalarGridSpec(
            num_scalar_prefetch=1, grid=(db,),
            in_specs=[pl.BlockSpec((None, IDX_HEADS, IDX_DIM), per),
                      pl.BlockSpec((None, IDX_HEADS, 1), per),
                      pl.BlockSpec((None, 1, IDX_DIM), per)] + page_specs,
            out_specs=pl.BlockSpec((None, 1, n_cols), per)),
        out_shape=jax.ShapeDtypeStruct((db, 1, n_cols), I32),
        compiler_params=_params("arbitrary"),
        name="sample_scores",
    )(page_table, qi3, w3, kn3, *([cache_kidx] * n_pages))


def _sample_select_body(key_ref, mask_ref, keys_scr, *, n_cols, topk):
    keys_scr[...] = key_ref[...].T
    _topk_mask(keys_scr, n_cols // LANES, LANES, topk)
    mask_ref[...] = pltpu.bitcast(keys_scr[...], F32).T


def _sample_select(keys, topk):
    n_rows, n_cols = keys.shape
    return pl.pallas_call(
        functools.partial(_sample_select_body, n_cols=n_cols, topk=topk),
        out_shape=jax.ShapeDtypeStruct((n_rows, n_cols), F32),
        scratch_shapes=[pltpu.VMEM((n_cols, n_rows), I32)],
        compiler_params=pltpu.CompilerParams(vmem_limit_bytes=VMEM_LIMIT),
        name="sample_select",
    )(keys)


def _sample_attn_body(pt_ref, q_ref, qt_ref, kn_ref, vnt_ref, mask_ref, bias_ref, *rest, n_pages):
    kt_refs, vt_refs, o_ref = rest[:n_pages], rest[n_pages:2 * n_pages], rest[2 * n_pages]
    past = n_pages * PAGE_SIZE
    mask = mask_ref[...]
    s_self_all = jnp.sum(q_ref[...] * kn_ref[...], axis=1, keepdims=True)
    for h in range(N_HEADS):
        q_col = qt_ref[:, h:h + 1]
        bias = bias_ref[h]
        s = jnp.concatenate([jnp.sum(kt_refs[p][h] * q_col, axis=0, keepdims=True) for p in range(n_pages)], axis=1)
        s = s + bias[:, :past] + mask[:, :past]
        s_self = s_self_all[h:h + 1, :] + bias[:, past:past + 1] + mask[:, past:past + 1]
        m = jnp.maximum(jnp.max(s, axis=1, keepdims=True), s_self)
        p_past = jnp.exp(s - m)
        p_self = jnp.exp(s_self - m)
        l = jnp.sum(p_past, axis=1, keepdims=True) + p_self
        acc = vt_refs[0][h] * p_past[:, :PAGE_SIZE]
        for p in range(1, n_pages):
            acc = acc + vt_refs[p][h] * p_past[:, p * PAGE_SIZE:(p + 1) * PAGE_SIZE]
        out = jnp.sum(acc, axis=1, keepdims=True) + p_self * vnt_ref[:, h:h + 1]
        o_ref[:, h:h + 1] = out / l


def _sample_attn(page_table, q3, qt3, kn3, vnt3, mask3, bias_row, cache_kt, cache_vt):
    db, n_pages = page_table.shape
    n_cols = n_pages * PAGE_SIZE + LANES
    per = lambda s, pt: (s, 0, 0)
    page_specs = [pl.BlockSpec((None, N_HEADS, HEAD_DIM, PAGE_SIZE),
                               functools.partial(lambda s, pt, p: (pt[s, p], 0, 0, 0), p=p)) for p in range(n_pages)]
    return pl.pallas_call(
        functools.partial(_sample_attn_body, n_pages=n_pages),
        grid_spec=pltpu.PrefetchScalarGridSpec(
            num_scalar_prefetch=1, grid=(db,),
            in_specs=[pl.BlockSpec((None, N_HEADS, HEAD_DIM), per),
                      pl.BlockSpec((None, HEAD_DIM, N_HEADS), per),
                      pl.BlockSpec((None, N_HEADS, HEAD_DIM), per),
                      pl.BlockSpec((None, HEAD_DIM, N_HEADS), per),
                      pl.BlockSpec((None, 1, n_cols), per),
                      pl.BlockSpec((N_HEADS, 1, n_cols), lambda s, pt: (0, 0, 0))] + page_specs + page_specs,
            out_specs=pl.BlockSpec((None, HEAD_DIM, N_HEADS), per)),
        out_shape=jax.ShapeDtypeStruct((db, HEAD_DIM, N_HEADS), F32),
        compiler_params=_params("arbitrary"),
        name="sample_attn",
    )(page_table, q3, qt3, kn3, vnt3, mask3, bias_row, *([cache_kt] * n_pages), *([cache_vt] * n_pages))


def _cmul(ar, ai, br, bi):
    return ar * br - ai * bi, ar * bi + ai * br


def _ssm_prep_body(lre_f, lim_f, ldt_f, lre_r, lim_r, ldt_r, bre_ref, bim_ref, pre_ref, pim_ref, bbr_ref, bbi_ref,
                   *, chunk):
    def lam_bar(lre, lim, ldt):
        dt = jnp.exp(ldt)
        mag = jnp.exp(lre * dt)
        return mag * jnp.cos(lim * dt), mag * jnp.sin(lim * dt)

    lbr, lbi = lam_bar(lre_f[...], lim_f[...], ldt_f[...])
    pr = jnp.broadcast_to(lbr, (chunk, N_STATE))
    pi = jnp.broadcast_to(lbi, (chunk, N_STATE))
    row = lax.broadcasted_iota(I32, (chunk, N_STATE), 0)
    d = 1
    while d < chunk:
        sr = jnp.where(row >= d, pltpu.roll(pr, d, 0), 1.0)
        si = jnp.where(row >= d, pltpu.roll(pi, d, 0), 0.0)
        pr, pi = _cmul(pr, pi, sr, si)
        d *= 2
    pre_ref[...] = pr
    pim_ref[...] = pi

    lre, lim = lre_r[...], lim_r[...]
    lbr, lbi = lam_bar(lre, lim, ldt_r[...])
    den = lre * lre + lim * lim
    nr, ni = lbr - 1.0, lbi
    cr = (nr * lre + ni * lim) / den
    ci = (ni * lre - nr * lim) / den
    bbr, bbi = _cmul(cr, ci, bre_ref[...], bim_ref[...])
    bbr_ref[...] = bbr
    bbi_ref[...] = bbi


def _ssm_prep(lam_re, lam_im, log_dt, b_re, b_im, chunk):
    flat = lambda a: a.reshape(1, N_STATE)
    rep = lambda a: jnp.repeat(a, GROUP, axis=0)
    ldt = jnp.broadcast_to(log_dt[:, None], (N_GROUPS, STATE_DIM))
    bt = lambda b: jnp.swapaxes(b, 1, 2).reshape(SSM_WIDTH, STATE_DIM)
    return pl.pallas_call(
        functools.partial(_ssm_prep_body, chunk=chunk),
        out_shape=[jax.ShapeDtypeStruct((chunk, N_STATE), F32), jax.ShapeDtypeStruct((chunk, N_STATE), F32),
                   jax.ShapeDtypeStruct((SSM_WIDTH, STATE_DIM), F32), jax.ShapeDtypeStruct((SSM_WIDTH, STATE_DIM), F32)],
        compiler_params=pltpu.CompilerParams(vmem_limit_bytes=VMEM_LIMIT),
        name="ssm_prep",
    )(flat(lam_re), flat(lam_im), flat(ldt), rep(lam_re), rep(lam_im), rep(ldt), bt(b_re), bt(b_im))


def _block_diag(blocks):
    g, r, c = blocks.shape
    eye = jnp.eye(g, dtype=blocks.dtype)
    return (blocks[:, :, None, :] * eye[:, None, :, None]).reshape(g * r, g * c)


def _ssm_readout(u, xr, xi, wc_ref, dskip_ref, wglu_ref, bglu_ref):
    y = _dot(jnp.concatenate([xr, xi], axis=1).astype(BF16), wc_ref[...]) + dskip_ref[...] * u
    g = jax.nn.gelu(y)
    return g * jax.nn.sigmoid(_dot(g.astype(BF16), wglu_ref[...]) + bglu_ref[...])


def _ssm_scan_body(u_ref, x0r_ref, x0i_ref, pre_ref, pim_ref, wb_ref, wc_ref, dskip_ref, wglu_ref, bglu_ref,
                   y_ref, sr_ref, si_ref, cr_ref, ci_ref, xr_ref, xi_ref, *, chunk):
    c = pl.program_id(1)

    @pl.when(c == 0)
    def _():
        cr_ref[...] = x0r_ref[0]
        ci_ref[...] = x0i_ref[0]

    u = u_ref[...]
    bu = _dot(u.astype(BF16), wb_ref[...])
    n_groups = chunk // SCAN_GROUP
    xr = bu[:, :N_STATE].reshape(n_groups, SCAN_GROUP, N_STATE)
    xi = bu[:, N_STATE:].reshape(n_groups, SCAN_GROUP, N_STATE)
    pos = lax.broadcasted_iota(I32, (SCAN_GROUP, N_STATE), 0)
    d = 1
    while d < SCAN_GROUP:
        ar = jnp.where(pos >= d, pre_ref[d - 1:d, :], 0.0)
        ai = jnp.where(pos >= d, pim_ref[d - 1:d, :], 0.0)
        tr, ti = _cmul(ar, ai, pltpu.roll(xr, d, 1), pltpu.roll(xi, d, 1))
        xr, xi = xr + tr, xi + ti
        d *= 2
    pr, pi = pre_ref[...], pim_ref[...]
    cr, ci = cr_ref[...], ci_ref[...]
    for g in range(n_groups):
        tr, ti = _cmul(pr, pi, cr, ci)
        gr, gi = xr[g] + tr, xi[g] + ti
        xr_ref[g * SCAN_GROUP:(g + 1) * SCAN_GROUP, :] = gr
        xi_ref[g * SCAN_GROUP:(g + 1) * SCAN_GROUP, :] = gi
        cr, ci = gr[SCAN_GROUP - 1:, :], gi[SCAN_GROUP - 1:, :]
    cr_ref[...] = cr
    ci_ref[...] = ci
    sr_ref[0] = cr
    si_ref[0] = ci
    y_ref[...] = _ssm_readout(u, xr_ref[...], xi_ref[...], wc_ref, dskip_ref, wglu_ref, bglu_ref)


def _ssm_scan(u, x0r, x0i, pre, pim, wb, wc, dskip, wglu, bglu, bsz, t, chunk):
    nc = t // chunk
    row = lambda b, c: (b * nc + c, 0)
    const = lambda b, c: (0, 0)
    state = pl.BlockSpec((1, 1, N_STATE), lambda b, c: (b, 0, 0))
    return pl.pallas_call(
        functools.partial(_ssm_scan_body, chunk=chunk),
        grid=(bsz, nc),
        in_specs=[pl.BlockSpec((chunk, SSM_WIDTH), row), state, state,
                  pl.BlockSpec((SCAN_GROUP, N_STATE), const), pl.BlockSpec((SCAN_GROUP, N_STATE), const),
                  pl.BlockSpec((SSM_WIDTH, 2 * N_STATE), const), pl.BlockSpec((2 * N_STATE, SSM_WIDTH), const),
                  pl.BlockSpec((1, SSM_WIDTH), const), pl.BlockSpec((SSM_WIDTH, SSM_WIDTH), const),
                  pl.BlockSpec((1, SSM_WIDTH), const)],
        out_specs=[pl.BlockSpec((chunk, SSM_WIDTH), row), state, state],
        out_shape=[jax.ShapeDtypeStruct((bsz * t, SSM_WIDTH), F32),
                   jax.ShapeDtypeStruct((bsz, 1, N_STATE), F32), jax.ShapeDtypeStruct((bsz, 1, N_STATE), F32)],
        scratch_shapes=[pltpu.VMEM((1, N_STATE), F32), pltpu.VMEM((1, N_STATE), F32),
                        pltpu.VMEM((chunk, N_STATE), F32), pltpu.VMEM((chunk, N_STATE), F32)],
        compiler_params=_params("arbitrary", "arbitrary"),
        name="ssm_scan",
    )(u, x0r, x0i, pre, pim, wb, wc, dskip, wglu, bglu)


def _ssm_step_body(u_ref, x0r_ref, x0i_ref, pre_ref, pim_ref, wb_ref, wc_ref, dskip_ref, wglu_ref, bglu_ref,
                   y_ref, sr_ref, si_ref):
    u = u_ref[...]
    bu = _dot(u.astype(BF16), wb_ref[...])
    tr, ti = _cmul(pre_ref[0:1, :], pim_ref[0:1, :], x0r_ref[...], x0i_ref[...])
    xr, xi = bu[:, :N_STATE] + tr, bu[:, N_STATE:] + ti
    sr_ref[...] = xr
    si_ref[...] = xi
    y_ref[...] = _ssm_readout(u, xr, xi, wc_ref, dskip_ref, wglu_ref, bglu_ref)


def _ssm_step(u, x0r, x0i, pre, pim, wb, wc, dskip, wglu, bglu):
    n = u.shape[0]
    return pl.pallas_call(
        _ssm_step_body,
        out_shape=[jax.ShapeDtypeStruct((n, SSM_WIDTH), F32),
                   jax.ShapeDtypeStruct((n, N_STATE), F32), jax.ShapeDtypeStruct((n, N_STATE), F32)],
        compiler_params=pltpu.CompilerParams(vmem_limit_bytes=VMEM_LIMIT),
        name="ssm_step",
    )(u, x0r, x0i, pre, pim, wb, wc, dskip, wglu, bglu)


def _merge_body(x_ref, ya_ref, ys_ref, sga_ref, sgs_ref, wa_ref, ws_ref, wo_ref, gpost_ref, gpre_ref,
                x1_ref, h2_ref):
    merged = (sga_ref[...] * _dot(ya_ref[...].astype(BF16), wa_ref[...])
              + sgs_ref[...] * _dot(ys_ref[...].astype(BF16), ws_ref[...]))
    x1 = x_ref[...] + _rms(_dot(merged.astype(BF16), wo_ref[...]), gpost_ref[...])
    x1_ref[...] = x1
    h2_ref[...] = _rms(x1, gpre_ref[...]).astype(BF16)


def _merge(x2, ya, ys, sga, sgs, wa, ws, wo, gpost, gpre, tm):
    m = x2.shape[0]
    row = lambda i: (i, 0)
    const = lambda i: (0, 0)
    return pl.pallas_call(
        _merge_body,
        grid=(m // tm,),
        in_specs=[pl.BlockSpec((tm, D_MODEL), row), pl.BlockSpec((tm, ATT_WIDTH), row),
                  pl.BlockSpec((tm, SSM_WIDTH), row), pl.BlockSpec((tm, D_MODEL), row),
                  pl.BlockSpec((tm, D_MODEL), row),
                  pl.BlockSpec((ATT_WIDTH, D_MODEL), const), pl.BlockSpec((SSM_WIDTH, D_MODEL), const),
                  pl.BlockSpec((D_MODEL, D_MODEL), const), pl.BlockSpec((1, D_MODEL), const),
                  pl.BlockSpec((1, D_MODEL), const)],
        out_specs=[pl.BlockSpec((tm, D_MODEL), row), pl.BlockSpec((tm, D_MODEL), row)],
        out_shape=[jax.ShapeDtypeStruct((m, D_MODEL), F32), jax.ShapeDtypeStruct((m, D_MODEL), BF16)],
        compiler_params=_params("arbitrary"),
        name="merge",
    )(x2, ya, ys, sga, sgs, wa, ws, wo, gpost, gpre)


def _ffn_body(h_ref, x_ref, wua_ref, wub_ref, cwa_ref, cwb_ref, cba_ref, cbb_ref, wd_ref, g_ref, pa_ref, pb_ref,
              y_ref, oa_ref, ob_ref, f_ref, *carry, seq, tiles_per_seq):
    i, j = pl.program_id(0), pl.program_id(1)
    nj = pl.num_programs(1)
    tm = h_ref.shape[0]
    half = tm // 2
    halves = [slice(0, half), slice(half, tm)]
    ca, cb = carry if seq else (None, None)

    @pl.when(j == 0)
    def _():
        f_ref[...] = jnp.zeros(f_ref.shape, F32)

    if seq:
        @pl.when(i % tiles_per_seq == 0)
        def _():
            ca[j, 0:CONV_W - 1, :] = pa_ref[0]
            cb[j, 0:CONV_W - 1, :] = pb_ref[0]

    ups = [(_dot(h_ref[rows, :], wua_ref[...]), _dot(h_ref[rows, :], wub_ref[...])) for rows in halves]

    def conv(up, before, rows, cw_ref, cb_ref, prev_ref):
        if seq:
            row = lax.broadcasted_iota(I32, up.shape, 0)
            m1 = jnp.where(row == 0, before[1:2], pltpu.roll(up, 1, 0))
            m2 = jnp.where(row == 0, before[0:1], jnp.where(row == 1, before[1:2], pltpu.roll(up, 2, 0)))
        else:
            m2, m1 = prev_ref[rows, 0, :], prev_ref[rows, 1, :]
        return cb_ref[...] + m2 * cw_ref[0:1, :] + m1 * cw_ref[1:2, :] + up * cw_ref[2:3, :]

    before_a = ca[j, 0:CONV_W - 1, :] if seq else None
    before_b = cb[j, 0:CONV_W - 1, :] if seq else None
    for rows, (up_a, up_b) in zip(halves, ups):
        a = conv(up_a, before_a, rows, cwa_ref, cba_ref, pa_ref)
        b = conv(up_b, before_b, rows, cwb_ref, cbb_ref, pb_ref)
        f_ref[rows, :] += _dot((jax.nn.gelu(a) * b).astype(BF16), wd_ref[...])
        if seq:
            before_a, before_b = up_a[half - (CONV_W - 1):, :], up_b[half - (CONV_W - 1):, :]
        else:
            oa_ref[rows, :] = up_a
            ob_ref[rows, :] = up_b
    if seq:
        ca[j, 0:CONV_W - 1, :] = before_a
        cb[j, 0:CONV_W - 1, :] = before_b
        oa_ref[0] = before_a
        ob_ref[0] = before_b

    @pl.when(j == nj - 1)
    def _():
        y_ref[...] = x_ref[...] + _rms(f_ref[...], g_ref[...])


def _ffn(h2, x1, w_up, conv_w, conv_b, w_down, g_post, conv_prev, bsz, t, tm, tf, seq):
    m = bsz * t
    nj = D_FF // tf
    row = lambda i, j: (i, 0)
    if seq:
        tiles_per_seq = t // tm
        prev_a = pl.BlockSpec((1, CONV_W - 1, tf), lambda i, j: (i // tiles_per_seq, 0, j))
        prev_b = pl.BlockSpec((1, CONV_W - 1, tf), lambda i, j: (i // tiles_per_seq, 0, j + nj))
        out_tail = pl.BlockSpec((1, CONV_W - 1, tf), lambda i, j: (i, 0, j))
        tail_shape = jax.ShapeDtypeStruct((m // tm, CONV_W - 1, D_FF), F32)
        scratch = [pltpu.VMEM((nj, 8, tf), F32), pltpu.VMEM((nj, 8, tf), F32)]
    else:
        tiles_per_seq = 1
        prev_a = pl.BlockSpec((tm, CONV_W - 1, tf), lambda i, j: (i, 0, j))
        prev_b = pl.BlockSpec((tm, CONV_W - 1, tf), lambda i, j: (i, 0, j + nj))
        out_tail = pl.BlockSpec((tm, tf), lambda i, j: (i, j))
        tail_shape = jax.ShapeDtypeStruct((m, D_FF), F32)
        scratch = []
    return pl.pallas_call(
        functools.partial(_ffn_body, seq=seq, tiles_per_seq=tiles_per_seq),
        grid=(m // tm, nj),
        in_specs=[pl.BlockSpec((tm, D_MODEL), row), pl.BlockSpec((tm, D_MODEL), row),
                  pl.BlockSpec((D_MODEL, tf), lambda i, j: (0, j)),
                  pl.BlockSpec((D_MODEL, tf), lambda i, j: (0, j + nj)),
                  pl.BlockSpec((CONV_W, tf), lambda i, j: (0, j)),
                  pl.BlockSpec((CONV_W, tf), lambda i, j: (0, j + nj)),
                  pl.BlockSpec((1, tf), lambda i, j: (0, j)),
                  pl.BlockSpec((1, tf), lambda i, j: (0, j + nj)),
                  pl.BlockSpec((tf, D_MODEL), lambda i, j: (j, 0)),
                  pl.BlockSpec((1, D_MODEL), lambda i, j: (0, 0)),
                  prev_a, prev_b],
        out_specs=[pl.BlockSpec((tm, D_MODEL), row), out_tail, out_tail],
        out_shape=[jax.ShapeDtypeStruct((m, D_MODEL), F32), tail_shape, tail_shape],
        scratch_shapes=[pltpu.VMEM((tm, D_MODEL), F32)] + scratch,
        compiler_params=_params("arbitrary", "arbitrary"),
        name="conv_ffn",
    )(h2, x1, w_up, w_up, conv_w, conv_w, conv_b, conv_b, w_down, g_post, conv_prev, conv_prev)


def _pack_w_in(w_in):
    points = np.cumsum(SPLITS)[:-1].tolist()
    wq, wk, wv, wqi, wki, wwi, wu, wga, wgs = jnp.split(w_in, points, axis=-1)
    pad = jnp.zeros((D_MODEL, LANES - IDX_DIM - IDX_HEADS), w_in.dtype)
    return jnp.concatenate([wq, wk, wv, wqi, wki, wwi, pad, wu, wga, wgs], axis=-1).astype(BF16)


def _layer_weights(lw):
    (w_in, g_pre_mix, g_post_mix, lam_re, lam_im, log_dt, b_re, b_im, c_re, c_im, d_skip,
     w_glu, b_glu, w_att_out, w_ssm_out, w_o, g_pre_ffn, g_post_ffn, w_up, conv_w, conv_b, w_down) = lw
    pre, pim, bbr, bbi = _ssm_prep(lam_re, lam_im, log_dt, b_re, b_im, SCAN_GROUP)
    wb = jnp.concatenate([_block_diag(bbr.reshape(N_GROUPS, GROUP, STATE_DIM)),
                          _block_diag(bbi.reshape(N_GROUPS, GROUP, STATE_DIM))], axis=1).astype(BF16)
    wc = jnp.concatenate([_block_diag(jnp.swapaxes(c_re, 1, 2)),
                          _block_diag(-jnp.swapaxes(c_im, 1, 2))], axis=0).astype(BF16)
    vec = lambda a: a.reshape(1, -1)
    return dict(
        w_in=_pack_w_in(w_in), g_pre_mix=vec(g_pre_mix), g_post_mix=vec(g_post_mix),
        pre=pre, pim=pim, wb=wb, wc=wc, d_skip=vec(d_skip), w_glu=w_glu.astype(BF16), b_glu=vec(b_glu),
        w_att_out=w_att_out.astype(BF16), w_ssm_out=w_ssm_out.astype(BF16), w_o=w_o.astype(BF16),
        g_pre_ffn=vec(g_pre_ffn), g_post_ffn=vec(g_post_ffn), w_up=w_up.astype(BF16), conv_w=conv_w,
        conv_b=vec(conv_b), w_down=w_down.astype(BF16))


def _prompt_layer(x, w, bias_tiles, tq, tm, chunk, tf):
    bsz, t, _ = x.shape
    x2 = x.reshape(bsz * t, D_MODEL)
    k, v, kw, u, sga, sgs, qt, qit, kwt, kh, vt, ki = _in_proj(x2, w["g_pre_mix"], w["w_in"], bsz, t, tm, True)
    y_att = _prompt_attn(qt, qit, kwt, kh, vt, ki, bias_tiles, bsz, t, tq, min(TOPK_MAX, t // 4))
    zero_state = jnp.zeros((bsz, 1, N_STATE), F32)
    y_ssm, s_re, s_im = _ssm_scan(u, zero_state, zero_state, w["pre"], w["pim"], w["wb"], w["wc"], w["d_skip"],
                                  w["w_glu"], w["b_glu"], bsz, t, chunk)
    x1, h2 = _merge(x2, y_att, y_ssm, sga, sgs, w["w_att_out"], w["w_ssm_out"], w["w_o"],
                    w["g_post_mix"], w["g_pre_ffn"], tm)
    zero_conv = jnp.zeros((bsz, CONV_W - 1, 2 * D_FF), F32)
    y, tail_a, tail_b = _ffn(h2, x1, w["w_up"], w["conv_w"], w["conv_b"], w["w_down"], w["g_post_ffn"],
                             zero_conv, bsz, t, tm, tf, True)
    state = (jnp.transpose(k, (0, 3, 1, 2)), jnp.transpose(v, (0, 3, 1, 2)),
             jnp.swapaxes(kwt[:, :IDX_DIM, :], 1, 2),
             s_re.reshape(bsz, N_GROUPS, STATE_DIM), s_im.reshape(bsz, N_GROUPS, STATE_DIM),
             jnp.concatenate([tail_a, tail_b], axis=-1)[t // tm - 1::t // tm])
    return y.reshape(bsz, t, D_MODEL), state


def _sample_layer(x, w, bias_row, cache_k, cache_v, cache_kidx, page_table, st_re, st_im, st_conv, tf):
    db, tq, _ = x.shape
    assert tq == 1, "the sample group is decoded one token per sequence"
    n_pool = cache_k.shape[0]
    past = page_table.shape[1] * PAGE_SIZE
    x2 = x.reshape(db, D_MODEL)
    k, v, kw, u, sga, sgs, q, qi = _in_proj(x2, w["g_pre_mix"], w["w_in"], 1, db, db, False)
    keys = _sample_scores(page_table, qi.reshape(db, IDX_HEADS, IDX_DIM),
                          kw[:, IDX_DIM:IDX_DIM + IDX_HEADS].reshape(db, IDX_HEADS, 1),
                          kw[:, :IDX_DIM].reshape(db, 1, IDX_DIM), jnp.swapaxes(cache_kidx, 1, 2))
    mask = _sample_select(keys.reshape(db, past + LANES), min(TOPK_MAX, (past + tq) // 4))
    heads = lambda a: a.astype(F32).reshape(db, N_HEADS, HEAD_DIM)
    heads_t = lambda a: jnp.swapaxes(heads(a), 1, 2)
    pages_t = lambda c: jnp.transpose(c, (0, 2, 3, 1))
    y_att_t = _sample_attn(page_table, heads(q), heads_t(q), heads(k), heads_t(v), mask[:, None, :], bias_row,
                           pages_t(cache_k), pages_t(cache_v))
    y_att = jnp.swapaxes(y_att_t, 1, 2).reshape(db, ATT_WIDTH)
    y_ssm, s_re, s_im = _ssm_step(u, st_re.reshape(db, N_STATE), st_im.reshape(db, N_STATE), w["pre"], w["pim"],
                                  w["wb"], w["wc"], w["d_skip"], w["w_glu"], w["b_glu"])
    x1, h2 = _merge(x2, y_att, y_ssm, sga, sgs, w["w_att_out"], w["w_ssm_out"], w["w_o"],
                    w["g_post_mix"], w["g_pre_ffn"], db)
    y, up_a, up_b = _ffn(h2, x1, w["w_up"], w["conv_w"], w["conv_b"], w["w_down"], w["g_post_ffn"],
                         st_conv, db, 1, db, tf, False)
    conv_new = jnp.concatenate([st_conv[:, 1:], jnp.concatenate([up_a, up_b], axis=-1)[:, None, :]], axis=1)
    state = (k.reshape(db, 1, N_HEADS, HEAD_DIM), v.reshape(db, 1, N_HEADS, HEAD_DIM),
             kw[:, :IDX_DIM].reshape(db, 1, IDX_DIM),
             s_re.reshape(db, N_GROUPS, STATE_DIM), s_im.reshape(db, N_GROUPS, STATE_DIM), conv_new)
    return y.reshape(db, 1, D_MODEL), state


def kernel(x_prompt, x_sample, cache_k, cache_v, cache_kidx, state_ssm_re, state_ssm_im, state_conv, page_table,
           rel_bias, w_in, g_pre_mix, g_post_mix, lam_re, lam_im, log_dt, b_re, b_im, c_re, c_im, d_skip, w_glu,
           b_glu, w_att_out, w_ssm_out, w_o, g_pre_ffn, g_post_ffn, w_up, conv_w, conv_b, w_down):
    depth = w_in.shape[0]
    t = x_prompt.shape[1]
    past = page_table.shape[1] * PAGE_SIZE
    tq = min(256, t)
    tm = min(512, t)
    chunk = min(256, t)
    tf = 512
    bias_tiles, bias_row = _bias_tiles(rel_bias, tq, past)
    y_p, y_s = x_prompt, x_sample
    outs_p, outs_s = [], []
    for l in range(depth):
        lw = (w_in[l], g_pre_mix[l], g_post_mix[l], lam_re[l], lam_im[l], log_dt[l], b_re[l], b_im[l],
              c_re[l], c_im[l], d_skip[l], w_glu[l], b_glu[l], w_att_out[l], w_ssm_out[l], w_o[l],
              g_pre_ffn[l], g_post_ffn[l], w_up[l], conv_w[l], conv_b[l], w_down[l])
        w = _layer_weights(lw)
        y_p, st_p = _prompt_layer(y_p, w, bias_tiles, tq, tm, chunk, tf)
        y_s, st_s = _sample_layer(y_s, w, bias_row, cache_k[l], cache_v[l], cache_kidx[l], page_table,
                                  state_ssm_re[l], state_ssm_im[l], state_conv[l], tf)
        outs_p.append(st_p)
        outs_s.append(st_s)
    k_p, v_p, ki_p, sr_p, si_p, cv_p = [jnp.stack(a) for a in zip(*outs_p)]
    k_s, v_s, ki_s, sr_s, si_s, cv_s = [jnp.stack(a) for a in zip(*outs_s)]
    return (y_p, y_s, k_p, v_p, ki_p, sr_p, si_p, cv_p, k_s, v_s, ki_s, sr_s, si_s, cv_s)
```

```python
import functools
import math

import numpy as np
import jax
import jax.numpy as jnp
from jax import lax
from jax.experimental import pallas as pl
from jax.experimental.pallas import tpu as pltpu

F32 = jnp.float32
BF16 = jnp.bfloat16
I32 = jnp.int32

D_MODEL = 1024
PAGE_SIZE = 128
N_HEADS = 8
HEAD_DIM = 64
ATT_WIDTH = N_HEADS * HEAD_DIM
IDX_HEADS = 4
IDX_DIM = 64
TOPK_MAX = 256
N_BUCKETS = 32
MAX_DISTANCE = 128
SSM_WIDTH = 512
GROUP = 16
N_GROUPS = SSM_WIDTH // GROUP
STATE_DIM = 64
N_STATE = N_GROUPS * STATE_DIM
D_FF = 4 * D_MODEL
CONV_W = 3
EPS = 1e-6
SPLITS = (ATT_WIDTH, ATT_WIDTH, ATT_WIDTH, IDX_HEADS * IDX_DIM, IDX_DIM, IDX_HEADS, SSM_WIDTH, D_MODEL, D_MODEL)

LANES = 128
KEY_EXCLUDED = -(2 ** 31)
IDX_BITS = 14
BF16_SUBLANES = 16
V_ROWS = HEAD_DIM + BF16_SUBLANES
SCAN_GROUP = 8
MASKED = -1e30
VMEM_LIMIT = 56 * 1024 * 1024

_C_Q, _C_K, _C_V = 0, ATT_WIDTH, 2 * ATT_WIDTH
_C_QI = 3 * ATT_WIDTH
_C_KW = _C_QI + IDX_HEADS * IDX_DIM
_C_U = _C_KW + LANES
_C_GA = _C_U + SSM_WIDTH
_C_GS = _C_GA + D_MODEL
_C_END = _C_GS + D_MODEL


def _params(*sem):
    return pltpu.CompilerParams(dimension_semantics=sem, vmem_limit_bytes=VMEM_LIMIT)


def _rms(x, g):
    inv = lax.rsqrt(jnp.mean(x * x, axis=-1, keepdims=True) + EPS)
    return (x * inv) * g


def _dot(a, b):
    return jnp.dot(a, b, preferred_element_type=F32)


def _dot_nt(a, b):
    return lax.dot_general(a, b, (((1,), (1,)), ((), ())), preferred_element_type=F32)


def _bucket_starts():
    n = np.arange(0, 1 << IDX_BITS, dtype=np.int32)
    max_exact = N_BUCKETS // 2
    nf = np.maximum(n, 1).astype(np.float32)
    large = max_exact + (np.log(nf / np.float32(max_exact)) / np.float32(math.log(MAX_DISTANCE / max_exact))
                         * np.float32(N_BUCKETS - max_exact)).astype(np.int32)
    large = np.minimum(large, N_BUCKETS - 1)
    bucket = np.where(n < max_exact, n, large)
    assert np.all(np.diff(bucket) >= 0)
    starts = [int(np.argmax(bucket >= b)) for b in range(N_BUCKETS)]
    assert all(bucket[s] == b for b, s in enumerate(starts))
    return starts


_BUCKET_START = _bucket_starts()


def _bias_tiles_body(rb_ref, tile_ref, row_ref, *, tq, past):
    h = pl.program_id(0)

    def bias_of(dist):
        val = jnp.full(dist.shape, rb_ref[0, h], F32)
        for b in range(1, N_BUCKETS):
            val = jnp.where(dist >= _BUCKET_START[b], rb_ref[b, h], val)
        return val

    key_off = lax.broadcasted_iota(I32, (tq, tq), 0)
    qry_off = lax.broadcasted_iota(I32, (tq, tq), 1)
    far = rb_ref[N_BUCKETS - 1, h]
    tile_ref[0, 1] = bias_of(qry_off - key_off) - far
    tile_ref[0, 0] = bias_of(qry_off - key_off + tq) - far
    lane = lax.broadcasted_iota(I32, (1, past + LANES), 1)
    row_ref[0] = bias_of(past - lane)


def _bias_tiles(rel_bias, tq, past):
    assert tq + 1 >= _BUCKET_START[N_BUCKETS - 1]
    return pl.pallas_call(
        functools.partial(_bias_tiles_body, tq=tq, past=past),
        grid=(N_HEADS,),
        in_specs=[pl.BlockSpec(memory_space=pltpu.SMEM)],
        out_specs=[pl.BlockSpec((1, 2, tq, tq), lambda h: (h, 0, 0, 0)),
                   pl.BlockSpec((1, 1, past + LANES), lambda h: (h, 0, 0))],
        out_shape=[jax.ShapeDtypeStruct((N_HEADS, 2, tq, tq), F32),
                   jax.ShapeDtypeStruct((N_HEADS, 1, past + LANES), F32)],
        compiler_params=_params("arbitrary"),
        name="bias_tiles",
    )(rel_bias)


def _in_proj_body(x_ref, g_ref, w_ref, k_ref, v_ref, kw_ref, u_ref, sga_ref, sgs_ref, *more_refs, seq):
    h = _rms(x_ref[...], g_ref[...]).astype(BF16)

    def proj(c0, c1):
        return _dot(h, w_ref[:, c0:c1])

    q = proj(_C_Q, _C_K) * HEAD_DIM ** -0.5
    k = proj(_C_K, _C_V)
    v = proj(_C_V, _C_QI)
    qi = proj(_C_QI, _C_KW)
    kw = proj(_C_KW, _C_U)
    lane = lax.broadcasted_iota(I32, kw.shape, 1)
    kw = jnp.where(lane >= IDX_DIM, kw * IDX_HEADS ** -0.5, kw)
    kw_ref[...] = kw
    u_ref[...] = proj(_C_U, _C_GA)
    sga_ref[...] = jax.nn.sigmoid(proj(_C_GA, _C_GS))
    sgs_ref[...] = jax.nn.sigmoid(proj(_C_GS, _C_END))
    if seq:
        qt_ref, qit_ref, kwt_ref, kh_ref, vt_ref, ki_ref = more_refs
        tm = k.shape[0]
        vt = v.T.reshape(N_HEADS, HEAD_DIM, tm)
        k_ref[0] = k.T.reshape(N_HEADS, HEAD_DIM, tm)
        v_ref[0] = vt
        qt_ref[...] = q.T.reshape(N_HEADS, HEAD_DIM, tm).astype(BF16)
        qit_ref[...] = (qi * IDX_DIM ** -0.5).T.reshape(IDX_HEADS, IDX_DIM, tm).astype(BF16)
        kwt_ref[0] = kw.T
        for hd in range(N_HEADS):
            kh_ref[0, hd] = k[:, hd * HEAD_DIM:(hd + 1) * HEAD_DIM].astype(BF16)
        vt_ref[0, :, :HEAD_DIM, :] = vt.astype(BF16)
        vt_ref[0, :, HEAD_DIM:, :] = jnp.ones((N_HEADS, V_ROWS - HEAD_DIM, tm), BF16)
        ki_ref[0] = kw[:, :IDX_DIM].astype(BF16)
    else:
        q_ref, qi_ref = more_refs
        k_ref[...] = k
        v_ref[...] = v
        q_ref[...] = q.astype(BF16)
        qi_ref[...] = qi


def _in_proj(x2, g, w_packed, bsz, t, tm, seq):
    m = bsz * t
    nt = t // tm
    row = lambda b, i: (b * nt + i, 0)
    widths = (LANES, SSM_WIDTH, D_MODEL, D_MODEL)
    if seq:
        kv_spec = pl.BlockSpec((1, N_HEADS, HEAD_DIM, tm), lambda b, i: (b, 0, 0, i))
        kv_shape = jax.ShapeDtypeStruct((bsz, N_HEADS, HEAD_DIM, t), F32)
    else:
        kv_spec = pl.BlockSpec((tm, ATT_WIDTH), row)
        kv_shape = jax.ShapeDtypeStruct((m, ATT_WIDTH), F32)
    out_specs = [kv_spec, kv_spec] + [pl.BlockSpec((tm, w), row) for w in widths]
    out_shape = [kv_shape, kv_shape] + [jax.ShapeDtypeStruct((m, w), F32) for w in widths]
    if seq:
        col = lambda b, i: (0, 0, b * nt + i)
        out_specs += [pl.BlockSpec((N_HEADS, HEAD_DIM, tm), col),
                      pl.BlockSpec((IDX_HEADS, IDX_DIM, tm), col),
                      pl.BlockSpec((1, LANES, tm), lambda b, i: (b, 0, i)),
                      pl.BlockSpec((1, N_HEADS, tm, HEAD_DIM), lambda b, i: (b, 0, i, 0)),
                      pl.BlockSpec((1, N_HEADS, V_ROWS, tm), lambda b, i: (b, 0, 0, i)),
                      pl.BlockSpec((1, tm, IDX_DIM), lambda b, i: (b, i, 0))]
        out_shape += [jax.ShapeDtypeStruct((N_HEADS, HEAD_DIM, m), BF16),
                      jax.ShapeDtypeStruct((IDX_HEADS, IDX_DIM, m), BF16),
                      jax.ShapeDtypeStruct((bsz, LANES, t), F32),
                      jax.ShapeDtypeStruct((bsz, N_HEADS, t, HEAD_DIM), BF16),
                      jax.ShapeDtypeStruct((bsz, N_HEADS, V_ROWS, t), BF16),
                      jax.ShapeDtypeStruct((bsz, t, IDX_DIM), BF16)]
    else:
        out_specs += [pl.BlockSpec((tm, ATT_WIDTH), row), pl.BlockSpec((tm, IDX_HEADS * IDX_DIM), row)]
        out_shape += [jax.ShapeDtypeStruct((m, ATT_WIDTH), BF16),
                      jax.ShapeDtypeStruct((m, IDX_HEADS * IDX_DIM), F32)]
    return pl.pallas_call(
        functools.partial(_in_proj_body, seq=seq),
        grid=(bsz, nt),
        in_specs=[pl.BlockSpec((tm, D_MODEL), row),
                  pl.BlockSpec((1, D_MODEL), lambda b, i: (0, 0)),
                  pl.BlockSpec((D_MODEL, _C_END), lambda b, i: (0, 0))],
        out_specs=out_specs,
        out_shape=out_shape,
        compiler_params=_params("arbitrary", "arbitrary"),
        name="in_proj",
    )(x2, g, w_packed)


def _sortable_key(score):
    bits = pltpu.bitcast(score, I32)
    return jnp.where(bits < 0, bits ^ 0x7FFFFFFF, bits)


def _topk_mask(keys_ref, nblk, height, topk):
    n_lanes = keys_ref.shape[1]
    n_parts = height // 8
    assert n_parts * 8 == height and n_parts & (n_parts - 1) == 0

    def row_block(j):
        return pl.ds(pl.multiple_of(j * height, height), height)

    def count(pred, bound):
        def body(j, acc):
            hit = jnp.where(pred(keys_ref[row_block(j), :], bound), 1, 0)
            parts = [hit[r:r + 8] for r in range(0, height, 8)]
            while len(parts) > 1:
                parts = [a + b for a, b in zip(parts[::2], parts[1::2])]
            return acc + parts[0]
        acc = lax.fori_loop(0, nblk, body, jnp.zeros((8, n_lanes), I32))
        return jnp.sum(acc, axis=0, keepdims=True)

    zero = jnp.zeros((1, n_lanes), I32)

    prefix = jnp.where(count(lambda k, b: k >= b, zero) >= topk, 0, KEY_EXCLUDED).astype(I32)

    def value_bit(b, prefix):
        cand = prefix | jnp.left_shift(1, 30 - b)
        return jnp.where(count(lambda k, c: k >= c, cand) >= topk, cand, prefix)
    thr = lax.fori_loop(0, 31, value_bit, prefix)

    n_above = count(lambda k, t: k > t, thr)
    need = jnp.where(thr == KEY_EXCLUDED, 0, topk - n_above).astype(F32)
    at_or_before = (lax.broadcasted_iota(I32, (height, height), 0)
                    >= lax.broadcasted_iota(I32, (height, height), 1))
    lower_ones = jnp.where(at_or_before, 1.0, 0.0).astype(BF16)

    def emit(j, seen):
        k = keys_ref[row_block(j), :]
        tie = k == thr
        rank = seen + _dot(lower_ones, jnp.where(tie, 1.0, 0.0).astype(BF16))
        mask = jnp.where(k > thr, 0.0, jnp.where(tie, jnp.where(rank <= need, 0.0, MASKED), MASKED))
        keys_ref[row_block(j), :] = pltpu.bitcast(mask.astype(F32), I32)
        return rank[height - 1:, :]
    lax.fori_loop(0, nblk, emit, jnp.zeros((1, n_lanes), F32))


def _prompt_attn_body(qt_ref, qit_ref, wt_ref, kh_ref, vt_ref, ki_ref, bias_ref, o_ref, keys_ref, m_ref, acc_ref,
                      alpha_ref, p_ref, *, tq, topk):
    i = pl.program_id(1)
    nblk = i + 1
    key_off = lax.broadcasted_iota(I32, (tq, tq), 0)
    qry_off = lax.broadcasted_iota(I32, (tq, tq), 1)

    def key_block(j):
        return pl.ds(pl.multiple_of(j * tq, tq), tq)

    qit = jnp.concatenate([qit_ref[h] for h in range(IDX_HEADS)], axis=1)
    wt = [wt_ref[0, h:h + 1, :] for h in range(IDX_HEADS)]

    def score_block(j, _, diagonal):
        ki = ki_ref[0, key_block(j), :]
        dots = _dot(ki, qit)
        s = None
        for h in range(IDX_HEADS):
            sh = jnp.maximum(dots[:, h * tq:(h + 1) * tq], 0.0) * wt[h]
            s = sh if s is None else s + sh
        key = _sortable_key(s)
        if diagonal:
            key = jnp.where(key_off <= qry_off, key, KEY_EXCLUDED)
        keys_ref[key_block(j), :] = key
        return 0
    lax.fori_loop(0, i, functools.partial(score_block, diagonal=False), 0)
    score_block(i, 0, diagonal=True)

    _topk_mask(keys_ref, nblk, tq, topk)

    m_ref[...] = jnp.full(m_ref.shape, MASKED, F32)
    acc_ref[...] = jnp.zeros(acc_ref.shape, F32)
    p_ref[...] = jnp.zeros(p_ref.shape, BF16)
    alpha_ref[...] = jnp.ones(alpha_ref.shape, F32)

    def accumulate(j):
        for h in range(N_HEADS):
            acc_ref[h] = alpha_ref[h] * acc_ref[h] + _dot(vt_ref[0, h, :, key_block(j)], p_ref[h])

    def attend(j, _, near):
        accumulate(jnp.maximum(j - 1, 0))
        masked = pltpu.bitcast(keys_ref[key_block(j), :], F32)
        for h in range(N_HEADS):
            s = _dot(kh_ref[0, h, key_block(j), :], qt_ref[h]) + masked
            if near:
                s = s + bias_ref[h, j - i + 1]
            m_old = m_ref[h]
            m_new = jnp.maximum(m_old, jnp.max(s, axis=0, keepdims=True))
            p_ref[h] = jnp.exp(s - m_new).astype(BF16)
            alpha_ref[h] = jnp.exp(m_old - m_new)
            m_ref[h] = m_new
        return 0

    n_far = jnp.maximum(i - 1, 0)
    lax.fori_loop(0, n_far, functools.partial(attend, near=False), 0)
    lax.fori_loop(n_far, nblk, functools.partial(attend, near=True), 0)
    accumulate(i)
    out_t = jnp.concatenate(
        [acc_ref[h, :HEAD_DIM, :] / acc_ref[h, HEAD_DIM:HEAD_DIM + 1, :] for h in range(N_HEADS)], axis=0)
    o_ref[...] = out_t.T


def _prompt_attn(qt, qit, kwt, kh, vt, ki, bias_tiles, bsz, t, tq, topk):
    nq = t // tq
    col = lambda b, i: (0, 0, b * nq + i)
    whole = dict(pipeline_mode=pl.Buffered(1))
    return pl.pallas_call(
        functools.partial(_prompt_attn_body, tq=tq, topk=topk),
        grid=(bsz, nq),
        in_specs=[pl.BlockSpec((N_HEADS, HEAD_DIM, tq), col),
                  pl.BlockSpec((IDX_HEADS, IDX_DIM, tq), col),
                  pl.BlockSpec((1, 8, tq), lambda b, i: (b, IDX_DIM // 8, i)),
                  pl.BlockSpec((1, N_HEADS, t, HEAD_DIM), lambda b, i: (b, 0, 0, 0), **whole),
                  pl.BlockSpec((1, N_HEADS, V_ROWS, t), lambda b, i: (b, 0, 0, 0), **whole),
                  pl.BlockSpec((1, t, IDX_DIM), lambda b, i: (b, 0, 0), **whole),
                  pl.BlockSpec((N_HEADS, 2, tq, tq), lambda b, i: (0, 0, 0, 0), **whole)],
        out_specs=pl.BlockSpec((tq, ATT_WIDTH), lambda b, i: (b * nq + i, 0)),
        out_shape=jax.ShapeDtypeStruct((bsz * t, ATT_WIDTH), F32),
        scratch_shapes=[pltpu.VMEM((t, tq), I32),
                        pltpu.VMEM((N_HEADS, 1, tq), F32), pltpu.VMEM((N_HEADS, V_ROWS, tq), F32),
                        pltpu.VMEM((N_HEADS, 1, tq), F32), pltpu.VMEM((N_HEADS, tq, tq), BF16)],
        compiler_params=_params("arbitrary", "arbitrary"),
        name="prompt_attn",
    )(qt, qit, kwt, kh, vt, ki, bias_tiles)


def _sample_scores_body(pt_ref, qi_ref, w_ref, kn_ref, *rest, n_pages):
    page_refs, key_ref = rest[:n_pages], rest[n_pages]
    qi = qi_ref[...].astype(BF16)
    w = w_ref[...]

    def weighted(s):
        return jnp.sum(jnp.maximum(s * IDX_DIM ** -0.5, 0.0) * w, axis=0, keepdims=True)

    for p in range(n_pages):
        s = _dot(qi, page_refs[p][...].astype(BF16))
        key_ref[:, p * PAGE_SIZE:(p + 1) * PAGE_SIZE] = _sortable_key(weighted(s))
    s_self = jnp.sum(qi.astype(F32) * kn_ref[...].astype(BF16).astype(F32), axis=1, keepdims=True)
    lane = lax.broadcasted_iota(I32, (1, LANES), 1)
    key_ref[:, n_pages * PAGE_SIZE:] = jnp.where(lane == 0, _sortable_key(weighted(s_self)), KEY_EXCLUDED)


def _sample_scores(page_table, qi3, w3, kn3, cache_kidx):
    db, n_pages = page_table.shape
    n_cols = n_pages * PAGE_SIZE + LANES
    per = lambda s, pt: (s, 0, 0)
    page_specs = [pl.BlockSpec((None, IDX_DIM, PAGE_SIZE), functools.partial(lambda s, pt, p: (pt[s, p], 0, 0), p=p))
                  for p in range(n_pages)]
    return pl.pallas_call(
        functools.partial(_sample_scores_body, n_pages=n_pages),
        grid_spec=pltpu.PrefetchScalarGridSpec(
            num_scalar_prefetch=1, grid=(db,),
            in_specs=[pl.BlockSpec((None, IDX_HEADS, IDX_DIM), per),
                      pl.BlockSpec((None, IDX_HEADS, 1), per),
                      pl.BlockSpec((None, 1, IDX_DIM), per)] + page_specs,
            out_specs=pl.BlockSpec((None, 1, n_cols), per)),
        out_shape=jax.ShapeDtypeStruct((db, 1, n_cols), I32),
        compiler_params=_params("arbitrary"),
        name="sample_scores",
    )(page_table, qi3, w3, kn3, *([cache_kidx] * n_pages))


def _sample_select_body(key_ref, mask_ref, keys_scr, *, n_cols, topk):
    keys_scr[...] = key_ref[...].T
    _topk_mask(keys_scr, n_cols // LANES, LANES, topk)
    mask_ref[...] = pltpu.bitcast(keys_scr[...], F32).T


def _sample_select(keys, topk):
    n_rows, n_cols = keys.shape
    return pl.pallas_call(
        functools.partial(_sample_select_body, n_cols=n_cols, topk=topk),
        out_shape=jax.ShapeDtypeStruct((n_rows, n_cols), F32),
        scratch_shapes=[pltpu.VMEM((n_cols, n_rows), I32)],
        compiler_params=pltpu.CompilerParams(vmem_limit_bytes=VMEM_LIMIT),
        name="sample_select",
    )(keys)


def _sample_attn_body(pt_ref, q_ref, qt_ref, kn_ref, vnt_ref, mask_ref, bias_ref, *rest, n_pages):
    kt_refs, vt_refs, o_ref = rest[:n_pages], rest[n_pages:2 * n_pages], rest[2 * n_pages]
    past = n_pages * PAGE_SIZE
    mask = mask_ref[...]
    s_self_all = jnp.sum(q_ref[...] * kn_ref[...], axis=1, keepdims=True)
    for h in range(N_HEADS):
        q_col = qt_ref[:, h:h + 1]
        bias = bias_ref[h]
        s = jnp.concatenate([jnp.sum(kt_refs[p][h] * q_col, axis=0, keepdims=True) for p in range(n_pages)], axis=1)
        s = s + bias[:, :past] + mask[:, :past]
        s_self = s_self_all[h:h + 1, :] + bias[:, past:past + 1] + mask[:, past:past + 1]
        m = jnp.maximum(jnp.max(s, axis=1, keepdims=True), s_self)
        p_past = jnp.exp(s - m)
        p_self = jnp.exp(s_self - m)
        l = jnp.sum(p_past, axis=1, keepdims=True) + p_self
        acc = vt_refs[0][h] * p_past[:, :PAGE_SIZE]
        for p in range(1, n_pages):
            acc = acc + vt_refs[p][h] * p_past[:, p * PAGE_SIZE:(p + 1) * PAGE_SIZE]
        out = jnp.sum(acc, axis=1, keepdims=True) + p_self * vnt_ref[:, h:h + 1]
        o_ref[:, h:h + 1] = out / l


def _sample_attn(page_table, q3, qt3, kn3, vnt3, mask3, bias_row, cache_kt, cache_vt):
    db, n_pages = page_table.shape
    n_cols = n_pages * PAGE_SIZE + LANES
    per = lambda s, pt: (s, 0, 0)
    page_specs = [pl.BlockSpec((None, N_HEADS, HEAD_DIM, PAGE_SIZE),
                               functools.partial(lambda s, pt, p: (pt[s, p], 0, 0, 0), p=p)) for p in range(n_pages)]
    return pl.pallas_call(
        functools.partial(_sample_attn_body, n_pages=n_pages),
        grid_spec=pltpu.PrefetchScalarGridSpec(
            num_scalar_prefetch=1, grid=(db,),
            in_specs=[pl.BlockSpec((None, N_HEADS, HEAD_DIM), per),
                      pl.BlockSpec((None, HEAD_DIM, N_HEADS), per),
                      pl.BlockSpec((None, N_HEADS, HEAD_DIM), per),
                      pl.BlockSpec((None, HEAD_DIM, N_HEADS), per),
                      pl.BlockSpec((None, 1, n_cols), per),
                      pl.BlockSpec((N_HEADS, 1, n_cols), lambda s, pt: (0, 0, 0))] + page_specs + page_specs,
            out_specs=pl.BlockSpec((None, HEAD_DIM, N_HEADS), per)),
        out_shape=jax.ShapeDtypeStruct((db, HEAD_DIM, N_HEADS), F32),
        compiler_params=_params("arbitrary"),
        name="sample_attn",
    )(page_table, q3, qt3, kn3, vnt3, mask3, bias_row, *([cache_kt] * n_pages), *([cache_vt] * n_pages))


def _cmul(ar, ai, br, bi):
    return ar * br - ai * bi, ar * bi + ai * br


def _ssm_prep_body(lre_f, lim_f, ldt_f, lre_r, lim_r, ldt_r, bre_ref, bim_ref, pre_ref, pim_ref, bbr_ref, bbi_ref,
                   *, chunk):
    def lam_bar(lre, lim, ldt):
        dt = jnp.exp(ldt)
        mag = jnp.exp(lre * dt)
        return mag * jnp.cos(lim * dt), mag * jnp.sin(lim * dt)

    lbr, lbi = lam_bar(lre_f[...], lim_f[...], ldt_f[...])
    pr = jnp.broadcast_to(lbr, (chunk, N_STATE))
    pi = jnp.broadcast_to(lbi, (chunk, N_STATE))
    row = lax.broadcasted_iota(I32, (chunk, N_STATE), 0)
    d = 1
    while d < chunk:
        sr = jnp.where(row >= d, pltpu.roll(pr, d, 0), 1.0)
        si = jnp.where(row >= d, pltpu.roll(pi, d, 0), 0.0)
        pr, pi = _cmul(pr, pi, sr, si)
        d *= 2
    pre_ref[...] = pr
    pim_ref[...] = pi

    lre, lim = lre_r[...], lim_r[...]
    lbr, lbi = lam_bar(lre, lim, ldt_r[...])
    den = lre * lre + lim * lim
    nr, ni = lbr - 1.0, lbi
    cr = (nr * lre + ni * lim) / den
    ci = (ni * lre - nr * lim) / den
    bbr, bbi = _cmul(cr, ci, bre_ref[...], bim_ref[...])
    bbr_ref[...] = bbr
    bbi_ref[...] = bbi


def _ssm_prep(lam_re, lam_im, log_dt, b_re, b_im, chunk):
    flat = lambda a: a.reshape(1, N_STATE)
    rep = lambda a: jnp.repeat(a, GROUP, axis=0)
    ldt = jnp.broadcast_to(log_dt[:, None], (N_GROUPS, STATE_DIM))
    bt = lambda b: jnp.swapaxes(b, 1, 2).reshape(SSM_WIDTH, STATE_DIM)
    return pl.pallas_call(
        functools.partial(_ssm_prep_body, chunk=chunk),
        out_shape=[jax.ShapeDtypeStruct((chunk, N_STATE), F32), jax.ShapeDtypeStruct((chunk, N_STATE), F32),
                   jax.ShapeDtypeStruct((SSM_WIDTH, STATE_DIM), F32), jax.ShapeDtypeStruct((SSM_WIDTH, STATE_DIM), F32)],
        compiler_params=pltpu.CompilerParams(vmem_limit_bytes=VMEM_LIMIT),
        name="ssm_prep",
    )(flat(lam_re), flat(lam_im), flat(ldt), rep(lam_re), rep(lam_im), rep(ldt), bt(b_re), bt(b_im))


def _block_diag(blocks):
    g, r, c = blocks.shape
    eye = jnp.eye(g, dtype=blocks.dtype)
    return (blocks[:, :, None, :] * eye[:, None, :, None]).reshape(g * r, g * c)


def _ssm_readout(u, xr, xi, wc_ref, dskip_ref, wglu_ref, bglu_ref):
    y = _dot(jnp.concatenate([xr, xi], axis=1).astype(BF16), wc_ref[...]) + dskip_ref[...] * u
    g = jax.nn.gelu(y)
    return g * jax.nn.sigmoid(_dot(g.astype(BF16), wglu_ref[...]) + bglu_ref[...])


def _ssm_scan_body(u_ref, x0r_ref, x0i_ref, pre_ref, pim_ref, wb_ref, wc_ref, dskip_ref, wglu_ref, bglu_ref,
                   y_ref, sr_ref, si_ref, cr_ref, ci_ref, xr_ref, xi_ref, *, chunk):
    c = pl.program_id(1)

    @pl.when(c == 0)
    def _():
        cr_ref[...] = x0r_ref[0]
        ci_ref[...] = x0i_ref[0]

    u = u_ref[...]
    bu = _dot(u.astype(BF16), wb_ref[...])
    n_groups = chunk // SCAN_GROUP
    xr = bu[:, :N_STATE].reshape(n_groups, SCAN_GROUP, N_STATE)
    xi = bu[:, N_STATE:].reshape(n_groups, SCAN_GROUP, N_STATE)
    pos = lax.broadcasted_iota(I32, (SCAN_GROUP, N_STATE), 0)
    d = 1
    while d < SCAN_GROUP:
        ar = jnp.where(pos >= d, pre_ref[d - 1:d, :], 0.0)
        ai = jnp.where(pos >= d, pim_ref[d - 1:d, :], 0.0)
        tr, ti = _cmul(ar, ai, pltpu.roll(xr, d, 1), pltpu.roll(xi, d, 1))
        xr, xi = xr + tr, xi + ti
        d *= 2
    pr, pi = pre_ref[...], pim_ref[...]
    cr, ci = cr_ref[...], ci_ref[...]
    for g in range(n_groups):
        tr, ti = _cmul(pr, pi, cr, ci)
        gr, gi = xr[g] + tr, xi[g] + ti
        xr_ref[g * SCAN_GROUP:(g + 1) * SCAN_GROUP, :] = gr
        xi_ref[g * SCAN_GROUP:(g + 1) * SCAN_GROUP, :] = gi
        cr, ci = gr[SCAN_GROUP - 1:, :], gi[SCAN_GROUP - 1:, :]
    cr_ref[...] = cr
    ci_ref[...] = ci
    sr_ref[0] = cr
    si_ref[0] = ci
    y_ref[...] = _ssm_readout(u, xr_ref[...], xi_ref[...], wc_ref, dskip_ref, wglu_ref, bglu_ref)


def _ssm_scan(u, x0r, x0i, pre, pim, wb, wc, dskip, wglu, bglu, bsz, t, chunk):
    nc = t // chunk
    row = lambda b, c: (b * nc + c, 0)
    const = lambda b, c: (0, 0)
    state = pl.BlockSpec((1, 1, N_STATE), lambda b, c: (b, 0, 0))
    return pl.pallas_call(
        functools.partial(_ssm_scan_body, chunk=chunk),
        grid=(bsz, nc),
        in_specs=[pl.BlockSpec((chunk, SSM_WIDTH), row), state, state,
                  pl.BlockSpec((SCAN_GROUP, N_STATE), const), pl.BlockSpec((SCAN_GROUP, N_STATE), const),
                  pl.BlockSpec((SSM_WIDTH, 2 * N_STATE), const), pl.BlockSpec((2 * N_STATE, SSM_WIDTH), const),
                  pl.BlockSpec((1, SSM_WIDTH), const), pl.BlockSpec((SSM_WIDTH, SSM_WIDTH), const),
                  pl.BlockSpec((1, SSM_WIDTH), const)],
        out_specs=[pl.BlockSpec((chunk, SSM_WIDTH), row), state, state],
        out_shape=[jax.ShapeDtypeStruct((bsz * t, SSM_WIDTH), F32),
                   jax.ShapeDtypeStruct((bsz, 1, N_STATE), F32), jax.ShapeDtypeStruct((bsz, 1, N_STATE), F32)],
        scratch_shapes=[pltpu.VMEM((1, N_STATE), F32), pltpu.VMEM((1, N_STATE), F32),
                        pltpu.VMEM((chunk, N_STATE), F32), pltpu.VMEM((chunk, N_STATE), F32)],
        compiler_params=_params("arbitrary", "arbitrary"),
        name="ssm_scan",
    )(u, x0r, x0i, pre, pim, wb, wc, dskip, wglu, bglu)


def _ssm_step_body(u_ref, x0r_ref, x0i_ref, pre_ref, pim_ref, wb_ref, wc_ref, dskip_ref, wglu_ref, bglu_ref,
                   y_ref, sr_ref, si_ref):
    u = u_ref[...]
    bu = _dot(u.astype(BF16), wb_ref[...])
    tr, ti = _cmul(pre_ref[0:1, :], pim_ref[0:1, :], x0r_ref[...], x0i_ref[...])
    xr, xi = bu[:, :N_STATE] + tr, bu[:, N_STATE:] + ti
    sr_ref[...] = xr
    si_ref[...] = xi
    y_ref[...] = _ssm_readout(u, xr, xi, wc_ref, dskip_ref, wglu_ref, bglu_ref)


def _ssm_step(u, x0r, x0i, pre, pim, wb, wc, dskip, wglu, bglu):
    n = u.shape[0]
    return pl.pallas_call(
        _ssm_step_body,
        out_shape=[jax.ShapeDtypeStruct((n, SSM_WIDTH), F32),
                   jax.ShapeDtypeStruct((n, N_STATE), F32), jax.ShapeDtypeStruct((n, N_STATE), F32)],
        compiler_params=pltpu.CompilerParams(vmem_limit_bytes=VMEM_LIMIT),
        name="ssm_step",
    )(u, x0r, x0i, pre, pim, wb, wc, dskip, wglu, bglu)


def _merge_body(x_ref, ya_ref, ys_ref, sga_ref, sgs_ref, wa_ref, ws_ref, wo_ref, gpost_ref, gpre_ref,
                x1_ref, h2_ref):
    merged = (sga_ref[...] * _dot(ya_ref[...].astype(BF16), wa_ref[...])
              + sgs_ref[...] * _dot(ys_ref[...].astype(BF16), ws_ref[...]))
    x1 = x_ref[...] + _rms(_dot(merged.astype(BF16), wo_ref[...]), gpost_ref[...])
    x1_ref[...] = x1
    h2_ref[...] = _rms(x1, gpre_ref[...]).astype(BF16)


def _merge(x2, ya, ys, sga, sgs, wa, ws, wo, gpost, gpre, tm):
    m = x2.shape[0]
    row = lambda i: (i, 0)
    const = lambda i: (0, 0)
    return pl.pallas_call(
        _merge_body,
        grid=(m // tm,),
        in_specs=[pl.BlockSpec((tm, D_MODEL), row), pl.BlockSpec((tm, ATT_WIDTH), row),
                  pl.BlockSpec((tm, SSM_WIDTH), row), pl.BlockSpec((tm, D_MODEL), row),
                  pl.BlockSpec((tm, D_MODEL), row),
                  pl.BlockSpec((ATT_WIDTH, D_MODEL), const), pl.BlockSpec((SSM_WIDTH, D_MODEL), const),
                  pl.BlockSpec((D_MODEL, D_MODEL), const), pl.BlockSpec((1, D_MODEL), const),
                  pl.BlockSpec((1, D_MODEL), const)],
        out_specs=[pl.BlockSpec((tm, D_MODEL), row), pl.BlockSpec((tm, D_MODEL), row)],
        out_shape=[jax.ShapeDtypeStruct((m, D_MODEL), F32), jax.ShapeDtypeStruct((m, D_MODEL), BF16)],
        compiler_params=_params("arbitrary"),
        name="merge",
    )(x2, ya, ys, sga, sgs, wa, ws, wo, gpost, gpre)


def _ffn_body(h_ref, x_ref, wua_ref, wub_ref, cwa_ref, cwb_ref, cba_ref, cbb_ref, wd_ref, g_ref, pa_ref, pb_ref,
              y_ref, oa_ref, ob_ref, f_ref, *carry, seq, tiles_per_seq):
    i, j = pl.program_id(0), pl.program_id(1)
    nj = pl.num_programs(1)
    tm = h_ref.shape[0]
    half = tm // 2
    halves = [slice(0, half), slice(half, tm)]
    ca, cb = carry if seq else (None, None)

    @pl.when(j == 0)
    def _():
        f_ref[...] = jnp.zeros(f_ref.shape, F32)

    if seq:
        @pl.when(i % tiles_per_seq == 0)
        def _():
            ca[j, 0:CONV_W - 1, :] = pa_ref[0]
            cb[j, 0:CONV_W - 1, :] = pb_ref[0]

    ups = [(_dot(h_ref[rows, :], wua_ref[...]), _dot(h_ref[rows, :], wub_ref[...])) for rows in halves]

    def conv(up, before, rows, cw_ref, cb_ref, prev_ref):
        if seq:
            row = lax.broadcasted_iota(I32, up.shape, 0)
            m1 = jnp.where(row == 0, before[1:2], pltpu.roll(up, 1, 0))
            m2 = jnp.where(row == 0, before[0:1], jnp.where(row == 1, before[1:2], pltpu.roll(up, 2, 0)))
        else:
            m2, m1 = prev_ref[rows, 0, :], prev_ref[rows, 1, :]
        return cb_ref[...] + m2 * cw_ref[0:1, :] + m1 * cw_ref[1:2, :] + up * cw_ref[2:3, :]

    before_a = ca[j, 0:CONV_W - 1, :] if seq else None
    before_b = cb[j, 0:CONV_W - 1, :] if seq else None
    for rows, (up_a, up_b) in zip(halves, ups):
        a = conv(up_a, before_a, rows, cwa_ref, cba_ref, pa_ref)
        b = conv(up_b, before_b, rows, cwb_ref, cbb_ref, pb_ref)
        f_ref[rows, :] += _dot((jax.nn.gelu(a) * b).astype(BF16), wd_ref[...])
        if seq:
            before_a, before_b = up_a[half - (CONV_W - 1):, :], up_b[half - (CONV_W - 1):, :]
        else:
            oa_ref[rows, :] = up_a
            ob_ref[rows, :] = up_b
    if seq:
        ca[j, 0:CONV_W - 1, :] = before_a
        cb[j, 0:CONV_W - 1, :] = before_b
        oa_ref[0] = before_a
        ob_ref[0] = before_b

    @pl.when(j == nj - 1)
    def _():
        y_ref[...] = x_ref[...] + _rms(f_ref[...], g_ref[...])


def _ffn(h2, x1, w_up, conv_w, conv_b, w_down, g_post, conv_prev, bsz, t, tm, tf, seq):
    m = bsz * t
    nj = D_FF // tf
    row = lambda i, j: (i, 0)
    if seq:
        tiles_per_seq = t // tm
        prev_a = pl.BlockSpec((1, CONV_W - 1, tf), lambda i, j: (i // tiles_per_seq, 0, j))
        prev_b = pl.BlockSpec((1, CONV_W - 1, tf), lambda i, j: (i // tiles_per_seq, 0, j + nj))
        out_tail = pl.BlockSpec((1, CONV_W - 1, tf), lambda i, j: (i, 0, j))
        tail_shape = jax.ShapeDtypeStruct((m // tm, CONV_W - 1, D_FF), F32)
        scratch = [pltpu.VMEM((nj, 8, tf), F32), pltpu.VMEM((nj, 8, tf), F32)]
    else:
        tiles_per_seq = 1
        prev_a = pl.BlockSpec((tm, CONV_W - 1, tf), lambda i, j: (i, 0, j))
        prev_b = pl.BlockSpec((tm, CONV_W - 1, tf), lambda i, j: (i, 0, j + nj))
        out_tail = pl.BlockSpec((tm, tf), lambda i, j: (i, j))
        tail_shape = jax.ShapeDtypeStruct((m, D_FF), F32)
        scratch = []
    return pl.pallas_call(
        functools.partial(_ffn_body, seq=seq, tiles_per_seq=tiles_per_seq),
        grid=(m // tm, nj),
        in_specs=[pl.BlockSpec((tm, D_MODEL), row), pl.BlockSpec((tm, D_MODEL), row),
                  pl.BlockSpec((D_MODEL, tf), lambda i, j: (0, j)),
                  pl.BlockSpec((D_MODEL, tf), lambda i, j: (0, j + nj)),
                  pl.BlockSpec((CONV_W, tf), lambda i, j: (0, j)),
                  pl.BlockSpec((CONV_W, tf), lambda i, j: (0, j + nj)),
                  pl.BlockSpec((1, tf), lambda i, j: (0, j)),
                  pl.BlockSpec((1, tf), lambda i, j: (0, j + nj)),
                  pl.BlockSpec((tf, D_MODEL), lambda i, j: (j, 0)),
                  pl.BlockSpec((1, D_MODEL), lambda i, j: (0, 0)),
                  prev_a, prev_b],
        out_specs=[pl.BlockSpec((tm, D_MODEL), row), out_tail, out_tail],
        out_shape=[jax.ShapeDtypeStruct((m, D_MODEL), F32), tail_shape, tail_shape],
        scratch_shapes=[pltpu.VMEM((tm, D_MODEL), F32)] + scratch,
        compiler_params=_params("arbitrary", "arbitrary"),
        name="conv_ffn",
    )(h2, x1, w_up, w_up, conv_w, conv_w, conv_b, conv_b, w_down, g_post, conv_prev, conv_prev)


def _pack_w_in(w_in):
    points = np.cumsum(SPLITS)[:-1].tolist()
    wq, wk, wv, wqi, wki, wwi, wu, wga, wgs = jnp.split(w_in, points, axis=-1)
    pad = jnp.zeros((D_MODEL, LANES - IDX_DIM - IDX_HEADS), w_in.dtype)
    return jnp.concatenate([wq, wk, wv, wqi, wki, wwi, pad, wu, wga, wgs], axis=-1).astype(BF16)


def _layer_weights(lw):
    (w_in, g_pre_mix, g_post_mix, lam_re, lam_im, log_dt, b_re, b_im, c_re, c_im, d_skip,
     w_glu, b_glu, w_att_out, w_ssm_out, w_o, g_pre_ffn, g_post_ffn, w_up, conv_w, conv_b, w_down) = lw
    pre, pim, bbr, bbi = _ssm_prep(lam_re, lam_im, log_dt, b_re, b_im, SCAN_GROUP)
    wb = jnp.concatenate([_block_diag(bbr.reshape(N_GROUPS, GROUP, STATE_DIM)),
                          _block_diag(bbi.reshape(N_GROUPS, GROUP, STATE_DIM))], axis=1).astype(BF16)
    wc = jnp.concatenate([_block_diag(jnp.swapaxes(c_re, 1, 2)),
                          _block_diag(-jnp.swapaxes(c_im, 1, 2))], axis=0).astype(BF16)
    vec = lambda a: a.reshape(1, -1)
    return dict(
        w_in=_pack_w_in(w_in), g_pre_mix=vec(g_pre_mix), g_post_mix=vec(g_post_mix),
        pre=pre, pim=pim, wb=wb, wc=wc, d_skip=vec(d_skip), w_glu=w_glu.astype(BF16), b_glu=vec(b_glu),
        w_att_out=w_att_out.astype(BF16), w_ssm_out=w_ssm_out.astype(BF16), w_o=w_o.astype(BF16),
        g_pre_ffn=vec(g_pre_ffn), g_post_ffn=vec(g_post_ffn), w_up=w_up.astype(BF16), conv_w=conv_w,
        conv_b=vec(conv_b), w_down=w_down.astype(BF16))


def _prompt_layer(x, w, bias_tiles, tq, tm, chunk, tf):
    bsz, t, _ = x.shape
    x2 = x.reshape(bsz * t, D_MODEL)
    k, v, kw, u, sga, sgs, qt, qit, kwt, kh, vt, ki = _in_proj(x2, w["g_pre_mix"], w["w_in"], bsz, t, tm, True)
    y_att = _prompt_attn(qt, qit, kwt, kh, vt, ki, bias_tiles, bsz, t, tq, min(TOPK_MAX, t // 4))
    zero_state = jnp.zeros((bsz, 1, N_STATE), F32)
    y_ssm, s_re, s_im = _ssm_scan(u, zero_state, zero_state, w["pre"], w["pim"], w["wb"], w["wc"], w["d_skip"],
                                  w["w_glu"], w["b_glu"], bsz, t, chunk)
    x1, h2 = _merge(x2, y_att, y_ssm, sga, sgs, w["w_att_out"], w["w_ssm_out"], w["w_o"],
                    w["g_post_mix"], w["g_pre_ffn"], tm)
    zero_conv = jnp.zeros((bsz, CONV_W - 1, 2 * D_FF), F32)
    y, tail_a, tail_b = _ffn(h2, x1, w["w_up"], w["conv_w"], w["conv_b"], w["w_down"], w["g_post_ffn"],
                             zero_conv, bsz, t, tm, tf, True)
    state = (jnp.transpose(k, (0, 3, 1, 2)), jnp.transpose(v, (0, 3, 1, 2)),
             jnp.swapaxes(kwt[:, :IDX_DIM, :], 1, 2),
             s_re.reshape(bsz, N_GROUPS, STATE_DIM), s_im.reshape(bsz, N_GROUPS, STATE_DIM),
             jnp.concatenate([tail_a, tail_b], axis=-1)[t // tm - 1::t // tm])
    return y.reshape(bsz, t, D_MODEL), state


def _sample_layer(x, w, bias_row, cache_k, cache_v, cache_kidx, page_table, st_re, st_im, st_conv, tf):
    db, tq, _ = x.shape
    assert tq == 1, "the sample group is decoded one token per sequence"
    n_pool = cache_k.shape[0]
    past = page_table.shape[1] * PAGE_SIZE
    x2 = x.reshape(db, D_MODEL)
    k, v, kw, u, sga, sgs, q, qi = _in_proj(x2, w["g_pre_mix"], w["w_in"], 1, db, db, False)
    keys = _sample_scores(page_table, qi.reshape(db, IDX_HEADS, IDX_DIM),
                          kw[:, IDX_DIM:IDX_DIM + IDX_HEADS].reshape(db, IDX_HEADS, 1),
                          kw[:, :IDX_DIM].reshape(db, 1, IDX_DIM), jnp.swapaxes(cache_kidx, 1, 2))
    mask = _sample_select(keys.reshape(db, past + LANES), min(TOPK_MAX, (past + tq) // 4))
    heads = lambda a: a.astype(F32).reshape(db, N_HEADS, HEAD_DIM)
    heads_t = lambda a: jnp.swapaxes(heads(a), 1, 2)
    pages_t = lambda c: jnp.transpose(c, (0, 2, 3, 1))
    y_att_t = _sample_attn(page_table, heads(q), heads_t(q), heads(k), heads_t(v), mask[:, None, :], bias_row,
                           pages_t(cache_k), pages_t(cache_v))
    y_att = jnp.swapaxes(y_att_t, 1, 2).reshape(db, ATT_WIDTH)
    y_ssm, s_re, s_im = _ssm_step(u, st_re.reshape(db, N_STATE), st_im.reshape(db, N_STATE), w["pre"], w["pim"],
                                  w["wb"], w["wc"], w["d_skip"], w["w_glu"], w["b_glu"])
    x1, h2 = _merge(x2, y_att, y_ssm, sga, sgs, w["w_att_out"], w["w_ssm_out"], w["w_o"],
                    w["g_post_mix"], w["g_pre_ffn"], db)
    y, up_a, up_b = _ffn(h2, x1, w["w_up"], w["conv_w"], w["conv_b"], w["w_down"], w["g_post_ffn"],
                         st_conv, db, 1, db, tf, False)
    conv_new = jnp.concatenate([st_conv[:, 1:], jnp.concatenate([up_a, up_b], axis=-1)[:, None, :]], axis=1)
    state = (k.reshape(db, 1, N_HEADS, HEAD_DIM), v.reshape(db, 1, N_HEADS, HEAD_DIM),
             kw[:, :IDX_DIM].reshape(db, 1, IDX_DIM),
             s_re.reshape(db, N_GROUPS, STATE_DIM), s_im.reshape(db, N_GROUPS, STATE_DIM), conv_new)
    return y.reshape(db, 1, D_MODEL), state


def kernel(x_prompt, x_sample, cache_k, cache_v, cache_kidx, state_ssm_re, state_ssm_im, state_conv, page_table,
           rel_bias, w_in, g_pre_mix, g_post_mix, lam_re, lam_im, log_dt, b_re, b_im, c_re, c_im, d_skip, w_glu,
           b_glu, w_att_out, w_ssm_out, w_o, g_pre_ffn, g_post_ffn, w_up, conv_w, conv_b, w_down):
    depth = w_in.shape[0]
    t = x_prompt.shape[1]
    past = page_table.shape[1] * PAGE_SIZE
    tq = min(256, t)
    tm = min(512, t)
    chunk = min(256, t)
    tf = 512
    bias_tiles, bias_row = _bias_tiles(rel_bias, tq, past)
    y_p, y_s = x_prompt, x_sample
    outs_p, outs_s = [], []
    for l in range(depth):
        lw = (w_in[l], g_pre_mix[l], g_post_mix[l], lam_re[l], lam_im[l], log_dt[l], b_re[l], b_im[l],
              c_re[l], c_im[l], d_skip[l], w_glu[l], b_glu[l], w_att_out[l], w_ssm_out[l], w_o[l],
              g_pre_ffn[l], g_post_ffn[l], w_up[l], conv_w[l], conv_b[l], w_down[l])
        w = _layer_weights(lw)
        y_p, st_p = _prompt_layer(y_p, w, bias_tiles, tq, tm, chunk, tf)
        y_s, st_s = _sample_layer(y_s, w, bias_row, cache_k[l], cache_v[l], cache_kidx[l], page_table,
                                  state_ssm_re[l], state_ssm_im[l], state_conv[l], tf)
        outs_p.append(st_p)
        outs_s.append(st_s)
    k_p, v_p, ki_p, sr_p, si_p, cv_p = [jnp.stack(a) for a in zip(*outs_p)]
    k_s, v_s, ki_s, sr_s, si_s, cv_s = [jnp.stack(a) for a in zip(*outs_s)]
    return (y_p, y_s, k_p, v_p, ki_p, sr_p, si_p, cv_p, k_s, v_s, ki_s, sr_s, si_s, cv_s)
```

```python
import functools
import math

import numpy as np
import jax
import jax.numpy as jnp
from jax import lax
from jax.experimental import pallas as pl
from jax.experimental.pallas import tpu as pltpu

F32 = jnp.float32
BF16 = jnp.bfloat16
I32 = jnp.int32

D_MODEL = 1024
PAGE_SIZE = 128
N_HEADS = 8
HEAD_DIM = 64
ATT_WIDTH = N_HEADS * HEAD_DIM
IDX_HEADS = 4
IDX_DIM = 64
TOPK_MAX = 256
N_BUCKETS = 32
MAX_DISTANCE = 128
SSM_WIDTH = 512
GROUP = 16
N_GROUPS = SSM_WIDTH // GROUP
STATE_DIM = 64
N_STATE = N_GROUPS * STATE_DIM
D_FF = 4 * D_MODEL
CONV_W = 3
EPS = 1e-6
SPLITS = (ATT_WIDTH, ATT_WIDTH, ATT_WIDTH, IDX_HEADS * IDX_DIM, IDX_DIM, IDX_HEADS, SSM_WIDTH, D_MODEL, D_MODEL)

LANES = 128
KEY_EXCLUDED = -(2 ** 31)
IDX_BITS = 14
BF16_SUBLANES = 16
V_ROWS = HEAD_DIM + BF16_SUBLANES
LOG2_E = math.log2(math.e)
FFN_PART_ROWS = 256
SCAN_GROUP = 8
MASKED = -1e30
VMEM_LIMIT = 56 * 1024 * 1024

_C_Q, _C_K, _C_V = 0, ATT_WIDTH, 2 * ATT_WIDTH
_C_QI = 3 * ATT_WIDTH
_C_KW = _C_QI + IDX_HEADS * IDX_DIM
_C_U = _C_KW + LANES
_C_GA = _C_U + SSM_WIDTH
_C_GS = _C_GA + D_MODEL
_C_END = _C_GS + D_MODEL


def _params(*sem):
    return pltpu.CompilerParams(dimension_semantics=sem, vmem_limit_bytes=VMEM_LIMIT)


def _rms(x, g):
    inv = lax.rsqrt(jnp.mean(x * x, axis=-1, keepdims=True) + EPS)
    return (x * inv) * g


def _dot(a, b):
    return jnp.dot(a, b, preferred_element_type=F32)


def _dot_nt(a, b):
    return lax.dot_general(a, b, (((1,), (1,)), ((), ())), preferred_element_type=F32)


def _bucket_starts():
    n = np.arange(0, 1 << IDX_BITS, dtype=np.int32)
    max_exact = N_BUCKETS // 2
    nf = np.maximum(n, 1).astype(np.float32)
    large = max_exact + (np.log(nf / np.float32(max_exact)) / np.float32(math.log(MAX_DISTANCE / max_exact))
                         * np.float32(N_BUCKETS - max_exact)).astype(np.int32)
    large = np.minimum(large, N_BUCKETS - 1)
    bucket = np.where(n < max_exact, n, large)
    assert np.all(np.diff(bucket) >= 0)
    starts = [int(np.argmax(bucket >= b)) for b in range(N_BUCKETS)]
    assert all(bucket[s] == b for b, s in enumerate(starts))
    return starts


_BUCKET_START = _bucket_starts()


def _bias_tiles_body(rb_ref, tile_ref, row_ref, *, tq, past):
    h = pl.program_id(0)

    def bias_of(dist):
        val = jnp.full(dist.shape, rb_ref[0, h], F32)
        for b in range(1, N_BUCKETS):
            val = jnp.where(dist >= _BUCKET_START[b], rb_ref[b, h], val)
        return val

    key_off = lax.broadcasted_iota(I32, (tq, tq), 0)
    qry_off = lax.broadcasted_iota(I32, (tq, tq), 1)
    far = rb_ref[N_BUCKETS - 1, h]
    tile_ref[0, 1] = (bias_of(qry_off - key_off) - far) * LOG2_E
    tile_ref[0, 0] = (bias_of(qry_off - key_off + tq) - far) * LOG2_E
    lane = lax.broadcasted_iota(I32, (1, past + LANES), 1)
    row_ref[0] = bias_of(past - lane)


def _bias_tiles(rel_bias, tq, past):
    assert tq + 1 >= _BUCKET_START[N_BUCKETS - 1]
    return pl.pallas_call(
        functools.partial(_bias_tiles_body, tq=tq, past=past),
        grid=(N_HEADS,),
        in_specs=[pl.BlockSpec(memory_space=pltpu.SMEM)],
        out_specs=[pl.BlockSpec((1, 2, tq, tq), lambda h: (h, 0, 0, 0)),
                   pl.BlockSpec((1, 1, past + LANES), lambda h: (h, 0, 0))],
        out_shape=[jax.ShapeDtypeStruct((N_HEADS, 2, tq, tq), F32),
                   jax.ShapeDtypeStruct((N_HEADS, 1, past + LANES), F32)],
        compiler_params=_params("arbitrary"),
        name="bias_tiles",
    )(rel_bias)


def _in_proj_body(x_ref, g_ref, w_ref, k_ref, v_ref, kw_ref, u_ref, sga_ref, sgs_ref, *more_refs, seq):
    h = _rms(x_ref[...], g_ref[...]).astype(BF16)

    def proj(c0, c1):
        return _dot(h, w_ref[:, c0:c1])

    q = proj(_C_Q, _C_K) * HEAD_DIM ** -0.5
    k = proj(_C_K, _C_V)
    v = proj(_C_V, _C_QI)
    qi = proj(_C_QI, _C_KW)
    kw = proj(_C_KW, _C_U)
    lane = lax.broadcasted_iota(I32, kw.shape, 1)
    kw = jnp.where(lane >= IDX_DIM, kw * IDX_HEADS ** -0.5, kw)
    kw_ref[...] = kw
    u_ref[...] = proj(_C_U, _C_GA)
    sga_ref[...] = jax.nn.sigmoid(proj(_C_GA, _C_GS))
    sgs_ref[...] = jax.nn.sigmoid(proj(_C_GS, _C_END))
    if seq:
        qt_ref, qit_ref, kwt_ref, kh_ref, vt_ref, ki_ref = more_refs
        tm = k.shape[0]
        vt = v.T.reshape(N_HEADS, HEAD_DIM, tm)
        k_ref[0] = k.T.reshape(N_HEADS, HEAD_DIM, tm)
        v_ref[0] = vt
        qt_ref[...] = (q * LOG2_E).T.reshape(N_HEADS, HEAD_DIM, tm).astype(BF16)
        qit_ref[...] = (qi * IDX_DIM ** -0.5).T.reshape(IDX_HEADS, IDX_DIM, tm).astype(BF16)
        kwt_ref[0] = kw.T
        for hd in range(N_HEADS):
            kh_ref[0, hd] = k[:, hd * HEAD_DIM:(hd + 1) * HEAD_DIM].astype(BF16)
        vt_ref[0, :, :HEAD_DIM, :] = vt.astype(BF16)
        vt_ref[0, :, HEAD_DIM:, :] = jnp.ones((N_HEADS, V_ROWS - HEAD_DIM, tm), BF16)
        ki_ref[0] = kw[:, :IDX_DIM].astype(BF16)
    else:
        q_ref, qi_ref = more_refs
        k_ref[...] = k
        v_ref[...] = v
        q_ref[...] = q.astype(BF16)
        qi_ref[...] = qi


def _in_proj(x2, g, w_packed, bsz, t, tm, seq):
    m = bsz * t
    nt = t // tm
    row = lambda b, i: (b * nt + i, 0)
    widths = (LANES, SSM_WIDTH, D_MODEL, D_MODEL)
    if seq:
        kv_spec = pl.BlockSpec((1, N_HEADS, HEAD_DIM, tm), lambda b, i: (b, 0, 0, i))
        kv_shape = jax.ShapeDtypeStruct((bsz, N_HEADS, HEAD_DIM, t), F32)
    else:
        kv_spec = pl.BlockSpec((tm, ATT_WIDTH), row)
        kv_shape = jax.ShapeDtypeStruct((m, ATT_WIDTH), F32)
    out_specs = [kv_spec, kv_spec] + [pl.BlockSpec((tm, w), row) for w in widths]
    out_shape = [kv_shape, kv_shape] + [jax.ShapeDtypeStruct((m, w), F32) for w in widths]
    if seq:
        col = lambda b, i: (0, 0, b * nt + i)
        out_specs += [pl.BlockSpec((N_HEADS, HEAD_DIM, tm), col),
                      pl.BlockSpec((IDX_HEADS, IDX_DIM, tm), col),
                      pl.BlockSpec((1, LANES, tm), lambda b, i: (b, 0, i)),
                      pl.BlockSpec((1, N_HEADS, tm, HEAD_DIM), lambda b, i: (b, 0, i, 0)),
                      pl.BlockSpec((1, N_HEADS, V_ROWS, tm), lambda b, i: (b, 0, 0, i)),
                      pl.BlockSpec((1, tm, IDX_DIM), lambda b, i: (b, i, 0))]
        out_shape += [jax.ShapeDtypeStruct((N_HEADS, HEAD_DIM, m), BF16),
                      jax.ShapeDtypeStruct((IDX_HEADS, IDX_DIM, m), BF16),
                      jax.ShapeDtypeStruct((bsz, LANES, t), F32),
                      jax.ShapeDtypeStruct((bsz, N_HEADS, t, HEAD_DIM), BF16),
                      jax.ShapeDtypeStruct((bsz, N_HEADS, V_ROWS, t), BF16),
                      jax.ShapeDtypeStruct((bsz, t, IDX_DIM), BF16)]
    else:
        out_specs += [pl.BlockSpec((tm, ATT_WIDTH), row), pl.BlockSpec((tm, IDX_HEADS * IDX_DIM), row)]
        out_shape += [jax.ShapeDtypeStruct((m, ATT_WIDTH), BF16),
                      jax.ShapeDtypeStruct((m, IDX_HEADS * IDX_DIM), F32)]
    return pl.pallas_call(
        functools.partial(_in_proj_body, seq=seq),
        grid=(bsz, nt),
        in_specs=[pl.BlockSpec((tm, D_MODEL), row),
                  pl.BlockSpec((1, D_MODEL), lambda b, i: (0, 0)),
                  pl.BlockSpec((D_MODEL, _C_END), lambda b, i: (0, 0))],
        out_specs=out_specs,
        out_shape=out_shape,
        compiler_params=_params("arbitrary", "arbitrary"),
        name="in_proj",
    )(x2, g, w_packed)


def _sortable_key(score):
    bits = pltpu.bitcast(score, I32)
    return jnp.where(bits < 0, bits ^ 0x7FFFFFFF, bits)


def _topk_mask(keys_ref, nblk, height, topk):
    n_lanes = keys_ref.shape[1]
    n_parts = height // 8
    assert n_parts * 8 == height and n_parts & (n_parts - 1) == 0

    def row_block(j):
        return pl.ds(pl.multiple_of(j * height, height), height)

    def count(pred, bound):
        def body(j, acc):
            hit = jnp.where(pred(keys_ref[row_block(j), :], bound), 1, 0)
            parts = [hit[r:r + 8] for r in range(0, height, 8)]
            while len(parts) > 1:
                parts = [a + b for a, b in zip(parts[::2], parts[1::2])]
            return acc + parts[0]
        acc = lax.fori_loop(0, nblk, body, jnp.zeros((8, n_lanes), I32))
        return jnp.sum(acc, axis=0, keepdims=True)

    zero = jnp.zeros((1, n_lanes), I32)

    prefix = jnp.where(count(lambda k, b: k >= b, zero) >= topk, 0, KEY_EXCLUDED).astype(I32)

    def value_bit(b, prefix):
        cand = prefix | jnp.left_shift(1, 30 - b)
        return jnp.where(count(lambda k, c: k >= c, cand) >= topk, cand, prefix)
    thr = lax.fori_loop(0, 31, value_bit, prefix)

    n_above = count(lambda k, t: k > t, thr)
    need = jnp.where(thr == KEY_EXCLUDED, 0, topk - n_above).astype(F32)
    at_or_before = (lax.broadcasted_iota(I32, (height, height), 0)
                    >= lax.broadcasted_iota(I32, (height, height), 1))
    lower_ones = jnp.where(at_or_before, 1.0, 0.0).astype(BF16)

    def emit(j, seen):
        k = keys_ref[row_block(j), :]
        tie = k == thr
        rank = seen + _dot(lower_ones, jnp.where(tie, 1.0, 0.0).astype(BF16))
        mask = jnp.where(k > thr, 0.0, jnp.where(tie, jnp.where(rank <= need, 0.0, MASKED), MASKED))
        keys_ref[row_block(j), :] = pltpu.bitcast(mask.astype(F32), I32)
        return rank[height - 1:, :]
    lax.fori_loop(0, nblk, emit, jnp.zeros((1, n_lanes), F32))


def _prompt_attn_body(qt_ref, qit_ref, wt_ref, kh_ref, vt_ref, ki_ref, bias_ref, o_ref, keys_ref, m_ref, acc_ref,
                      alpha_ref, p_ref, *, tq, topk):
    i = pl.program_id(1)
    nblk = i + 1
    key_off = lax.broadcasted_iota(I32, (tq, tq), 0)
    qry_off = lax.broadcasted_iota(I32, (tq, tq), 1)

    def key_block(j):
        return pl.ds(pl.multiple_of(j * tq, tq), tq)

    qit = jnp.concatenate([qit_ref[h] for h in range(IDX_HEADS)], axis=1)
    wt = [wt_ref[0, h:h + 1, :] for h in range(IDX_HEADS)]

    def score_block(j, _, diagonal):
        ki = ki_ref[0, key_block(j), :]
        dots = _dot(ki, qit)
        s = None
        for h in range(IDX_HEADS):
            sh = jnp.maximum(dots[:, h * tq:(h + 1) * tq], 0.0) * wt[h]
            s = sh if s is None else s + sh
        key = _sortable_key(s)
        if diagonal:
            key = jnp.where(key_off <= qry_off, key, KEY_EXCLUDED)
        keys_ref[key_block(j), :] = key
        return 0
    lax.fori_loop(0, i, functools.partial(score_block, diagonal=False), 0)
    score_block(i, 0, diagonal=True)

    _topk_mask(keys_ref, nblk, tq, topk)

    m_ref[...] = jnp.full(m_ref.shape, MASKED, F32)
    acc_ref[...] = jnp.zeros(acc_ref.shape, F32)
    p_ref[...] = jnp.zeros(p_ref.shape, BF16)
    alpha_ref[...] = jnp.ones(alpha_ref.shape, F32)

    def accumulate(j):
        for h in range(N_HEADS):
            acc_ref[h] = alpha_ref[h] * acc_ref[h] + _dot(vt_ref[0, h, :, key_block(j)], p_ref[h])

    def attend(j, _, near):
        accumulate(jnp.maximum(j - 1, 0))
        masked = pltpu.bitcast(keys_ref[key_block(j), :], F32)
        for h in range(N_HEADS):
            s = _dot(kh_ref[0, h, key_block(j), :], qt_ref[h]) + masked
            if near:
                s = s + bias_ref[h, j - i + 1]
            m_old = m_ref[h]
            m_new = jnp.maximum(m_old, jnp.max(s, axis=0, keepdims=True))
            p_ref[h] = jnp.exp2(s - m_new).astype(BF16)
            alpha_ref[h] = jnp.exp2(m_old - m_new)
            m_ref[h] = m_new
        return 0

    n_far = jnp.maximum(i - 1, 0)
    lax.fori_loop(0, n_far, functools.partial(attend, near=False), 0)
    lax.fori_loop(n_far, nblk, functools.partial(attend, near=True), 0)
    accumulate(i)
    out_t = jnp.concatenate(
        [acc_ref[h, :HEAD_DIM, :] / acc_ref[h, HEAD_DIM:HEAD_DIM + 1, :] for h in range(N_HEADS)], axis=0)
    o_ref[...] = out_t.T


def _prompt_attn(qt, qit, kwt, kh, vt, ki, bias_tiles, bsz, t, tq, topk):
    nq = t // tq
    col = lambda b, i: (0, 0, b * nq + i)
    whole = dict(pipeline_mode=pl.Buffered(1))
    return pl.pallas_call(
        functools.partial(_prompt_attn_body, tq=tq, topk=topk),
        grid=(bsz, nq),
        in_specs=[pl.BlockSpec((N_HEADS, HEAD_DIM, tq), col),
                  pl.BlockSpec((IDX_HEADS, IDX_DIM, tq), col),
                  pl.BlockSpec((1, 8, tq), lambda b, i: (b, IDX_DIM // 8, i)),
                  pl.BlockSpec((1, N_HEADS, t, HEAD_DIM), lambda b, i: (b, 0, 0, 0), **whole),
                  pl.BlockSpec((1, N_HEADS, V_ROWS, t), lambda b, i: (b, 0, 0, 0), **whole),
                  pl.BlockSpec((1, t, IDX_DIM), lambda b, i: (b, 0, 0), **whole),
                  pl.BlockSpec((N_HEADS, 2, tq, tq), lambda b, i: (0, 0, 0, 0), **whole)],
        out_specs=pl.BlockSpec((tq, ATT_WIDTH), lambda b, i: (b * nq + i, 0)),
        out_shape=jax.ShapeDtypeStruct((bsz * t, ATT_WIDTH), F32),
        scratch_shapes=[pltpu.VMEM((t, tq), I32),
                        pltpu.VMEM((N_HEADS, 1, tq), F32), pltpu.VMEM((N_HEADS, V_ROWS, tq), F32),
                        pltpu.VMEM((N_HEADS, 1, tq), F32), pltpu.VMEM((N_HEADS, tq, tq), BF16)],
        compiler_params=_params("arbitrary", "arbitrary"),
        name="prompt_attn",
    )(qt, qit, kwt, kh, vt, ki, bias_tiles)


def _sample_scores_body(pt_ref, qi_ref, w_ref, kn_ref, *rest, n_pages):
    page_refs, key_ref = rest[:n_pages], rest[n_pages]
    qi = qi_ref[...].astype(BF16)
    w = w_ref[...]

    def weighted(s):
        return jnp.sum(jnp.maximum(s * IDX_DIM ** -0.5, 0.0) * w, axis=0, keepdims=True)

    for p in range(n_pages):
        s = _dot(qi, page_refs[p][...].astype(BF16))
        key_ref[:, p * PAGE_SIZE:(p + 1) * PAGE_SIZE] = _sortable_key(weighted(s))
    s_self = jnp.sum(qi.astype(F32) * kn_ref[...].astype(BF16).astype(F32), axis=1, keepdims=True)
    lane = lax.broadcasted_iota(I32, (1, LANES), 1)
    key_ref[:, n_pages * PAGE_SIZE:] = jnp.where(lane == 0, _sortable_key(weighted(s_self)), KEY_EXCLUDED)


def _sample_scores(page_table, qi3, w3, kn3, cache_kidx):
    db, n_pages = page_table.shape
    n_cols = n_pages * PAGE_SIZE + LANES
    per = lambda s, pt: (s, 0, 0)
    page_specs = [pl.BlockSpec((None, IDX_DIM, PAGE_SIZE), functools.partial(lambda s, pt, p: (pt[s, p], 0, 0), p=p))
                  for p in range(n_pages)]
    return pl.pallas_call(
        functools.partial(_sample_scores_body, n_pages=n_pages),
        grid_spec=pltpu.PrefetchScalarGridSpec(
            num_scalar_prefetch=1, grid=(db,),
            in_specs=[pl.BlockSpec((None, IDX_HEADS, IDX_DIM), per),
                      pl.BlockSpec((None, IDX_HEADS, 1), per),
                      pl.BlockSpec((None, 1, IDX_DIM), per)] + page_specs,
            out_specs=pl.BlockSpec((None, 1, n_cols), per)),
        out_shape=jax.ShapeDtypeStruct((db, 1, n_cols), I32),
        compiler_params=_params("arbitrary"),
        name="sample_scores",
    )(page_table, qi3, w3, kn3, *([cache_kidx] * n_pages))


def _sample_select_body(key_ref, mask_ref, keys_scr, *, n_cols, topk):
    keys_scr[...] = key_ref[...].T
    _topk_mask(keys_scr, n_cols // LANES, LANES, topk)
    mask_ref[...] = pltpu.bitcast(keys_scr[...], F32).T


def _sample_select(keys, topk):
    n_rows, n_cols = keys.shape
    return pl.pallas_call(
        functools.partial(_sample_select_body, n_cols=n_cols, topk=topk),
        out_shape=jax.ShapeDtypeStruct((n_rows, n_cols), F32),
        scratch_shapes=[pltpu.VMEM((n_cols, n_rows), I32)],
        compiler_params=pltpu.CompilerParams(vmem_limit_bytes=VMEM_LIMIT),
        name="sample_select",
    )(keys)


def _sample_attn_body(pt_ref, q_ref, qt_ref, kn_ref, vnt_ref, mask_ref, bias_ref, *rest, n_pages):
    kt_refs, vt_refs, o_ref = rest[:n_pages], rest[n_pages:2 * n_pages], rest[2 * n_pages]
    past = n_pages * PAGE_SIZE
    mask = mask_ref[...]
    s_self_all = jnp.sum(q_ref[...] * kn_ref[...], axis=1, keepdims=True)
    for h in range(N_HEADS):
        q_col = qt_ref[:, h:h + 1]
        bias = bias_ref[h]
        s = jnp.concatenate([jnp.sum(kt_refs[p][h] * q_col, axis=0, keepdims=True) for p in range(n_pages)], axis=1)
        s = s + bias[:, :past] + mask[:, :past]
        s_self = s_self_all[h:h + 1, :] + bias[:, past:past + 1] + mask[:, past:past + 1]
        m = jnp.maximum(jnp.max(s, axis=1, keepdims=True), s_self)
        p_past = jnp.exp(s - m)
        p_self = jnp.exp(s_self - m)
        l = jnp.sum(p_past, axis=1, keepdims=True) + p_self
        acc = vt_refs[0][h] * p_past[:, :PAGE_SIZE]
        for p in range(1, n_pages):
            acc = acc + vt_refs[p][h] * p_past[:, p * PAGE_SIZE:(p + 1) * PAGE_SIZE]
        out = jnp.sum(acc, axis=1, keepdims=True) + p_self * vnt_ref[:, h:h + 1]
        o_ref[:, h:h + 1] = out / l


def _sample_attn(page_table, q3, qt3, kn3, vnt3, mask3, bias_row, cache_kt, cache_vt):
    db, n_pages = page_table.shape
    n_cols = n_pages * PAGE_SIZE + LANES
    per = lambda s, pt: (s, 0, 0)
    page_specs = [pl.BlockSpec((None, N_HEADS, HEAD_DIM, PAGE_SIZE),
                               functools.partial(lambda s, pt, p: (pt[s, p], 0, 0, 0), p=p)) for p in range(n_pages)]
    return pl.pallas_call(
        functools.partial(_sample_attn_body, n_pages=n_pages),
        grid_spec=pltpu.PrefetchScalarGridSpec(
            num_scalar_prefetch=1, grid=(db,),
            in_specs=[pl.BlockSpec((None, N_HEADS, HEAD_DIM), per),
                      pl.BlockSpec((None, HEAD_DIM, N_HEADS), per),
                      pl.BlockSpec((None, N_HEADS, HEAD_DIM), per),
                      pl.BlockSpec((None, HEAD_DIM, N_HEADS), per),
                      pl.BlockSpec((None, 1, n_cols), per),
                      pl.BlockSpec((N_HEADS, 1, n_cols), lambda s, pt: (0, 0, 0))] + page_specs + page_specs,
            out_specs=pl.BlockSpec((None, HEAD_DIM, N_HEADS), per)),
        out_shape=jax.ShapeDtypeStruct((db, HEAD_DIM, N_HEADS), F32),
        compiler_params=_params("arbitrary"),
        name="sample_attn",
    )(page_table, q3, qt3, kn3, vnt3, mask3, bias_row, *([cache_kt] * n_pages), *([cache_vt] * n_pages))


def _cmul(ar, ai, br, bi):
    return ar * br - ai * bi, ar * bi + ai * br


def _ssm_prep_body(lre_f, lim_f, ldt_f, lre_r, lim_r, ldt_r, bre_ref, bim_ref, pre_ref, pim_ref, bbr_ref, bbi_ref,
                   *, chunk):
    def lam_bar(lre, lim, ldt):
        dt = jnp.exp(ldt)
        mag = jnp.exp(lre * dt)
        return mag * jnp.cos(lim * dt), mag * jnp.sin(lim * dt)

    lbr, lbi = lam_bar(lre_f[...], lim_f[...], ldt_f[...])
    pr = jnp.broadcast_to(lbr, (chunk, N_STATE))
    pi = jnp.broadcast_to(lbi, (chunk, N_STATE))
    row = lax.broadcasted_iota(I32, (chunk, N_STATE), 0)
    d = 1
    while d < chunk:
        sr = jnp.where(row >= d, pltpu.roll(pr, d, 0), 1.0)
        si = jnp.where(row >= d, pltpu.roll(pi, d, 0), 0.0)
        pr, pi = _cmul(pr, pi, sr, si)
        d *= 2
    pre_ref[...] = pr
    pim_ref[...] = pi

    lre, lim = lre_r[...], lim_r[...]
    lbr, lbi = lam_bar(lre, lim, ldt_r[...])
    den = lre * lre + lim * lim
    nr, ni = lbr - 1.0, lbi
    cr = (nr * lre + ni * lim) / den
    ci = (ni * lre - nr * lim) / den
    bbr, bbi = _cmul(cr, ci, bre_ref[...], bim_ref[...])
    bbr_ref[...] = bbr
    bbi_ref[...] = bbi


def _ssm_prep(lam_re, lam_im, log_dt, b_re, b_im, chunk):
    flat = lambda a: a.reshape(1, N_STATE)
    rep = lambda a: jnp.repeat(a, GROUP, axis=0)
    ldt = jnp.broadcast_to(log_dt[:, None], (N_GROUPS, STATE_DIM))
    bt = lambda b: jnp.swapaxes(b, 1, 2).reshape(SSM_WIDTH, STATE_DIM)
    return pl.pallas_call(
        functools.partial(_ssm_prep_body, chunk=chunk),
        out_shape=[jax.ShapeDtypeStruct((chunk, N_STATE), F32), jax.ShapeDtypeStruct((chunk, N_STATE), F32),
                   jax.ShapeDtypeStruct((SSM_WIDTH, STATE_DIM), F32), jax.ShapeDtypeStruct((SSM_WIDTH, STATE_DIM), F32)],
        compiler_params=pltpu.CompilerParams(vmem_limit_bytes=VMEM_LIMIT),
        name="ssm_prep",
    )(flat(lam_re), flat(lam_im), flat(ldt), rep(lam_re), rep(lam_im), rep(ldt), bt(b_re), bt(b_im))


def _block_diag(blocks):
    g, r, c = blocks.shape
    eye = jnp.eye(g, dtype=blocks.dtype)
    return (blocks[:, :, None, :] * eye[:, None, :, None]).reshape(g * r, g * c)


def _ssm_readout(u, xr, xi, wc_ref, dskip_ref, wglu_ref, bglu_ref):
    y = _dot(jnp.concatenate([xr, xi], axis=1).astype(BF16), wc_ref[...]) + dskip_ref[...] * u
    g = jax.nn.gelu(y)
    return g * jax.nn.sigmoid(_dot(g.astype(BF16), wglu_ref[...]) + bglu_ref[...])


def _ssm_scan_body(u_ref, x0r_ref, x0i_ref, pre_ref, pim_ref, wb_ref, wc_ref, dskip_ref, wglu_ref, bglu_ref,
                   y_ref, sr_ref, si_ref, cr_ref, ci_ref, xr_ref, xi_ref, *, chunk):
    c = pl.program_id(1)

    @pl.when(c == 0)
    def _():
        cr_ref[...] = x0r_ref[0]
        ci_ref[...] = x0i_ref[0]

    u = u_ref[...]
    bu = _dot(u.astype(BF16), wb_ref[...])
    n_groups = chunk // SCAN_GROUP
    xr = bu[:, :N_STATE].reshape(n_groups, SCAN_GROUP, N_STATE)
    xi = bu[:, N_STATE:].reshape(n_groups, SCAN_GROUP, N_STATE)
    pos = lax.broadcasted_iota(I32, (SCAN_GROUP, N_STATE), 0)
    d = 1
    while d < SCAN_GROUP:
        ar = jnp.where(pos >= d, pre_ref[d - 1:d, :], 0.0)
        ai = jnp.where(pos >= d, pim_ref[d - 1:d, :], 0.0)
        tr, ti = _cmul(ar, ai, pltpu.roll(xr, d, 1), pltpu.roll(xi, d, 1))
        xr, xi = xr + tr, xi + ti
        d *= 2
    pr, pi = pre_ref[...], pim_ref[...]
    cr, ci = cr_ref[...], ci_ref[...]
    for g in range(n_groups):
        tr, ti = _cmul(pr, pi, cr, ci)
        gr, gi = xr[g] + tr, xi[g] + ti
        xr_ref[g * SCAN_GROUP:(g + 1) * SCAN_GROUP, :] = gr
        xi_ref[g * SCAN_GROUP:(g + 1) * SCAN_GROUP, :] = gi
        cr, ci = gr[SCAN_GROUP - 1:, :], gi[SCAN_GROUP - 1:, :]
    cr_ref[...] = cr
    ci_ref[...] = ci
    sr_ref[0] = cr
    si_ref[0] = ci
    y_ref[...] = _ssm_readout(u, xr_ref[...], xi_ref[...], wc_ref, dskip_ref, wglu_ref, bglu_ref)


def _ssm_scan(u, x0r, x0i, pre, pim, wb, wc, dskip, wglu, bglu, bsz, t, chunk):
    nc = t // chunk
    row = lambda b, c: (b * nc + c, 0)
    const = lambda b, c: (0, 0)
    state = pl.BlockSpec((1, 1, N_STATE), lambda b, c: (b, 0, 0))
    return pl.pallas_call(
        functools.partial(_ssm_scan_body, chunk=chunk),
        grid=(bsz, nc),
        in_specs=[pl.BlockSpec((chunk, SSM_WIDTH), row), state, state,
                  pl.BlockSpec((SCAN_GROUP, N_STATE), const), pl.BlockSpec((SCAN_GROUP, N_STATE), const),
                  pl.BlockSpec((SSM_WIDTH, 2 * N_STATE), const), pl.BlockSpec((2 * N_STATE, SSM_WIDTH), const),
                  pl.BlockSpec((1, SSM_WIDTH), const), pl.BlockSpec((SSM_WIDTH, SSM_WIDTH), const),
                  pl.BlockSpec((1, SSM_WIDTH), const)],
        out_specs=[pl.BlockSpec((chunk, SSM_WIDTH), row), state, state],
        out_shape=[jax.ShapeDtypeStruct((bsz * t, SSM_WIDTH), F32),
                   jax.ShapeDtypeStruct((bsz, 1, N_STATE), F32), jax.ShapeDtypeStruct((bsz, 1, N_STATE), F32)],
        scratch_shapes=[pltpu.VMEM((1, N_STATE), F32), pltpu.VMEM((1, N_STATE), F32),
                        pltpu.VMEM((chunk, N_STATE), F32), pltpu.VMEM((chunk, N_STATE), F32)],
        compiler_params=_params("arbitrary", "arbitrary"),
        name="ssm_scan",
    )(u, x0r, x0i, pre, pim, wb, wc, dskip, wglu, bglu)


def _ssm_step_body(u_ref, x0r_ref, x0i_ref, pre_ref, pim_ref, wb_ref, wc_ref, dskip_ref, wglu_ref, bglu_ref,
                   y_ref, sr_ref, si_ref):
    u = u_ref[...]
    bu = _dot(u.astype(BF16), wb_ref[...])
    tr, ti = _cmul(pre_ref[0:1, :], pim_ref[0:1, :], x0r_ref[...], x0i_ref[...])
    xr, xi = bu[:, :N_STATE] + tr, bu[:, N_STATE:] + ti
    sr_ref[...] = xr
    si_ref[...] = xi
    y_ref[...] = _ssm_readout(u, xr, xi, wc_ref, dskip_ref, wglu_ref, bglu_ref)


def _ssm_step(u, x0r, x0i, pre, pim, wb, wc, dskip, wglu, bglu):
    n = u.shape[0]
    return pl.pallas_call(
        _ssm_step_body,
        out_shape=[jax.ShapeDtypeStruct((n, SSM_WIDTH), F32),
                   jax.ShapeDtypeStruct((n, N_STATE), F32), jax.ShapeDtypeStruct((n, N_STATE), F32)],
        compiler_params=pltpu.CompilerParams(vmem_limit_bytes=VMEM_LIMIT),
        name="ssm_step",
    )(u, x0r, x0i, pre, pim, wb, wc, dskip, wglu, bglu)


def _merge_body(x_ref, ya_ref, ys_ref, sga_ref, sgs_ref, wa_ref, ws_ref, wo_ref, gpost_ref, gpre_ref,
                x1_ref, h2_ref):
    merged = (sga_ref[...] * _dot(ya_ref[...].astype(BF16), wa_ref[...])
              + sgs_ref[...] * _dot(ys_ref[...].astype(BF16), ws_ref[...]))
    x1 = x_ref[...] + _rms(_dot(merged.astype(BF16), wo_ref[...]), gpost_ref[...])
    x1_ref[...] = x1
    h2_ref[...] = _rms(x1, gpre_ref[...]).astype(BF16)


def _merge(x2, ya, ys, sga, sgs, wa, ws, wo, gpost, gpre, tm):
    m = x2.shape[0]
    row = lambda i: (i, 0)
    const = lambda i: (0, 0)
    return pl.pallas_call(
        _merge_body,
        grid=(m // tm,),
        in_specs=[pl.BlockSpec((tm, D_MODEL), row), pl.BlockSpec((tm, ATT_WIDTH), row),
                  pl.BlockSpec((tm, SSM_WIDTH), row), pl.BlockSpec((tm, D_MODEL), row),
                  pl.BlockSpec((tm, D_MODEL), row),
                  pl.BlockSpec((ATT_WIDTH, D_MODEL), const), pl.BlockSpec((SSM_WIDTH, D_MODEL), const),
                  pl.BlockSpec((D_MODEL, D_MODEL), const), pl.BlockSpec((1, D_MODEL), const),
                  pl.BlockSpec((1, D_MODEL), const)],
        out_specs=[pl.BlockSpec((tm, D_MODEL), row), pl.BlockSpec((tm, D_MODEL), row)],
        out_shape=[jax.ShapeDtypeStruct((m, D_MODEL), F32), jax.ShapeDtypeStruct((m, D_MODEL), BF16)],
        compiler_params=_params("arbitrary"),
        name="merge",
    )(x2, ya, ys, sga, sgs, wa, ws, wo, gpost, gpre)


def _ffn_body(h_ref, x_ref, wua_ref, wub_ref, cwa_ref, cwb_ref, cba_ref, cbb_ref, wd_ref, g_ref, pa_ref, pb_ref,
              y_ref, oa_ref, ob_ref, f_ref, *carry, seq, tiles_per_seq):
    i, j = pl.program_id(0), pl.program_id(1)
    nj = pl.num_programs(1)
    tm = h_ref.shape[0]
    half = min(FFN_PART_ROWS, tm // 2)
    halves = [slice(r, r + half) for r in range(0, tm, half)]
    ca, cb = carry if seq else (None, None)

    @pl.when(j == 0)
    def _():
        f_ref[...] = jnp.zeros(f_ref.shape, F32)

    if seq:
        @pl.when(i % tiles_per_seq == 0)
        def _():
            ca[j, 0:CONV_W - 1, :] = pa_ref[0]
            cb[j, 0:CONV_W - 1, :] = pb_ref[0]

    ups = [(_dot(h_ref[rows, :], wua_ref[...]), _dot(h_ref[rows, :], wub_ref[...])) for rows in halves]

    def conv(up, before, rows, cw_ref, cb_ref, prev_ref):
        if seq:
            row = lax.broadcasted_iota(I32, up.shape, 0)
            m1 = jnp.where(row == 0, before[1:2], pltpu.roll(up, 1, 0))
            m2 = jnp.where(row == 0, before[0:1], jnp.where(row == 1, before[1:2], pltpu.roll(up, 2, 0)))
        else:
            m2, m1 = prev_ref[rows, 0, :], prev_ref[rows, 1, :]
        return cb_ref[...] + m2 * cw_ref[0:1, :] + m1 * cw_ref[1:2, :] + up * cw_ref[2:3, :]

    before_a = ca[j, 0:CONV_W - 1, :] if seq else None
    before_b = cb[j, 0:CONV_W - 1, :] if seq else None
    for rows, (up_a, up_b) in zip(halves, ups):
        a = conv(up_a, before_a, rows, cwa_ref, cba_ref, pa_ref)
        b = conv(up_b, before_b, rows, cwb_ref, cbb_ref, pb_ref)
        f_ref[rows, :] += _dot((jax.nn.gelu(a) * b).astype(BF16), wd_ref[...])
        if seq:
            before_a, before_b = up_a[half - (CONV_W - 1):, :], up_b[half - (CONV_W - 1):, :]
        else:
            oa_ref[rows, :] = up_a
            ob_ref[rows, :] = up_b
    if seq:
        ca[j, 0:CONV_W - 1, :] = before_a
        cb[j, 0:CONV_W - 1, :] = before_b
        oa_ref[0] = before_a
        ob_ref[0] = before_b

    @pl.when(j == nj - 1)
    def _():
        y_ref[...] = x_ref[...] + _rms(f_ref[...], g_ref[...])


def _ffn(h2, x1, w_up, conv_w, conv_b, w_down, g_post, conv_prev, bsz, t, tm, tf, seq):
    m = bsz * t
    nj = D_FF // tf
    row = lambda i, j: (i, 0)
    if seq:
        tiles_per_seq = t // tm
        prev_a = pl.BlockSpec((1, CONV_W - 1, tf), lambda i, j: (i // tiles_per_seq, 0, j))
        prev_b = pl.BlockSpec((1, CONV_W - 1, tf), lambda i, j: (i // tiles_per_seq, 0, j + nj))
        out_tail = pl.BlockSpec((1, CONV_W - 1, tf), lambda i, j: (i, 0, j))
        tail_shape = jax.ShapeDtypeStruct((m // tm, CONV_W - 1, D_FF), F32)
        scratch = [pltpu.VMEM((nj, 8, tf), F32), pltpu.VMEM((nj, 8, tf), F32)]
    else:
        tiles_per_seq = 1
        prev_a = pl.BlockSpec((tm, CONV_W - 1, tf), lambda i, j: (i, 0, j))
        prev_b = pl.BlockSpec((tm, CONV_W - 1, tf), lambda i, j: (i, 0, j + nj))
        out_tail = pl.BlockSpec((tm, tf), lambda i, j: (i, j))
        tail_shape = jax.ShapeDtypeStruct((m, D_FF), F32)
        scratch = []
    return pl.pallas_call(
        functools.partial(_ffn_body, seq=seq, tiles_per_seq=tiles_per_seq),
        grid=(m // tm, nj),
        in_specs=[pl.BlockSpec((tm, D_MODEL), row), pl.BlockSpec((tm, D_MODEL), row),
                  pl.BlockSpec((D_MODEL, tf), lambda i, j: (0, j)),
                  pl.BlockSpec((D_MODEL, tf), lambda i, j: (0, j + nj)),
                  pl.BlockSpec((CONV_W, tf), lambda i, j: (0, j)),
                  pl.BlockSpec((CONV_W, tf), lambda i, j: (0, j + nj)),
                  pl.BlockSpec((1, tf), lambda i, j: (0, j)),
                  pl.BlockSpec((1, tf), lambda i, j: (0, j + nj)),
                  pl.BlockSpec((tf, D_MODEL), lambda i, j: (j, 0)),
                  pl.BlockSpec((1, D_MODEL), lambda i, j: (0, 0)),
                  prev_a, prev_b],
        out_specs=[pl.BlockSpec((tm, D_MODEL), row), out_tail, out_tail],
        out_shape=[jax.ShapeDtypeStruct((m, D_MODEL), F32), tail_shape, tail_shape],
        scratch_shapes=[pltpu.VMEM((tm, D_MODEL), F32)] + scratch,
        compiler_params=_params("arbitrary", "arbitrary"),
        name="conv_ffn",
    )(h2, x1, w_up, w_up, conv_w, conv_w, conv_b, conv_b, w_down, g_post, conv_prev, conv_prev)


def _pack_w_in(w_in):
    points = np.cumsum(SPLITS)[:-1].tolist()
    wq, wk, wv, wqi, wki, wwi, wu, wga, wgs = jnp.split(w_in, points, axis=-1)
    pad = jnp.zeros((D_MODEL, LANES - IDX_DIM - IDX_HEADS), w_in.dtype)
    return jnp.concatenate([wq, wk, wv, wqi, wki, wwi, pad, wu, wga, wgs], axis=-1).astype(BF16)


def _layer_weights(lw):
    (w_in, g_pre_mix, g_post_mix, lam_re, lam_im, log_dt, b_re, b_im, c_re, c_im, d_skip,
     w_glu, b_glu, w_att_out, w_ssm_out, w_o, g_pre_ffn, g_post_ffn, w_up, conv_w, conv_b, w_down) = lw
    pre, pim, bbr, bbi = _ssm_prep(lam_re, lam_im, log_dt, b_re, b_im, SCAN_GROUP)
    wb = jnp.concatenate([_block_diag(bbr.reshape(N_GROUPS, GROUP, STATE_DIM)),
                          _block_diag(bbi.reshape(N_GROUPS, GROUP, STATE_DIM))], axis=1).astype(BF16)
    wc = jnp.concatenate([_block_diag(jnp.swapaxes(c_re, 1, 2)),
                          _block_diag(-jnp.swapaxes(c_im, 1, 2))], axis=0).astype(BF16)
    vec = lambda a: a.reshape(1, -1)
    return dict(
        w_in=_pack_w_in(w_in), g_pre_mix=vec(g_pre_mix), g_post_mix=vec(g_post_mix),
        pre=pre, pim=pim, wb=wb, wc=wc, d_skip=vec(d_skip), w_glu=w_glu.astype(BF16), b_glu=vec(b_glu),
        w_att_out=w_att_out.astype(BF16), w_ssm_out=w_ssm_out.astype(BF16), w_o=w_o.astype(BF16),
        g_pre_ffn=vec(g_pre_ffn), g_post_ffn=vec(g_post_ffn), w_up=w_up.astype(BF16), conv_w=conv_w,
        conv_b=vec(conv_b), w_down=w_down.astype(BF16))


def _prompt_layer(x, w, bias_tiles, tq, tm, chunk, tf):
    bsz, t, _ = x.shape
    x2 = x.reshape(bsz * t, D_MODEL)
    k, v, kw, u, sga, sgs, qt, qit, kwt, kh, vt, ki = _in_proj(x2, w["g_pre_mix"], w["w_in"], bsz, t, tm, True)
    y_att = _prompt_attn(qt, qit, kwt, kh, vt, ki, bias_tiles, bsz, t, tq, min(TOPK_MAX, t // 4))
    zero_state = jnp.zeros((bsz, 1, N_STATE), F32)
    y_ssm, s_re, s_im = _ssm_scan(u, zero_state, zero_state, w["pre"], w["pim"], w["wb"], w["wc"], w["d_skip"],
                                  w["w_glu"], w["b_glu"], bsz, t, chunk)
    x1, h2 = _merge(x2, y_att, y_ssm, sga, sgs, w["w_att_out"], w["w_ssm_out"], w["w_o"],
                    w["g_post_mix"], w["g_pre_ffn"], tm)
    zero_conv = jnp.zeros((bsz, CONV_W - 1, 2 * D_FF), F32)
    tm_ffn = min(2 * tm, t)
    y, tail_a, tail_b = _ffn(h2, x1, w["w_up"], w["conv_w"], w["conv_b"], w["w_down"], w["g_post_ffn"],
                             zero_conv, bsz, t, tm_ffn, tf, True)
    state = (jnp.transpose(k, (0, 3, 1, 2)), jnp.transpose(v, (0, 3, 1, 2)),
             jnp.swapaxes(kwt[:, :IDX_DIM, :], 1, 2),
             s_re.reshape(bsz, N_GROUPS, STATE_DIM), s_im.reshape(bsz, N_GROUPS, STATE_DIM),
             jnp.concatenate([tail_a, tail_b], axis=-1)[t // tm_ffn - 1::t // tm_ffn])
    return y.reshape(bsz, t, D_MODEL), state


def _sample_layer(x, w, bias_row, cache_k, cache_v, cache_kidx, page_table, st_re, st_im, st_conv, tf):
    db, tq, _ = x.shape
    assert tq == 1, "the sample group is decoded one token per sequence"
    n_pool = cache_k.shape[0]
    past = page_table.shape[1] * PAGE_SIZE
    x2 = x.reshape(db, D_MODEL)
    k, v, kw, u, sga, sgs, q, qi = _in_proj(x2, w["g_pre_mix"], w["w_in"], 1, db, db, False)
    keys = _sample_scores(page_table, qi.reshape(db, IDX_HEADS, IDX_DIM),
                          kw[:, IDX_DIM:IDX_DIM + IDX_HEADS].reshape(db, IDX_HEADS, 1),
                          kw[:, :IDX_DIM].reshape(db, 1, IDX_DIM), jnp.swapaxes(cache_kidx, 1, 2))
    mask = _sample_select(keys.reshape(db, past + LANES), min(TOPK_MAX, (past + tq) // 4))
    heads = lambda a: a.astype(F32).reshape(db, N_HEADS, HEAD_DIM)
    heads_t = lambda a: jnp.swapaxes(heads(a), 1, 2)
    pages_t = lambda c: jnp.transpose(c, (0, 2, 3, 1))
    y_att_t = _sample_attn(page_table, heads(q), heads_t(q), heads(k), heads_t(v), mask[:, None, :], bias_row,
                           pages_t(cache_k), pages_t(cache_v))
    y_att = jnp.swapaxes(y_att_t, 1, 2).reshape(db, ATT_WIDTH)
    y_ssm, s_re, s_im = _ssm_step(u, st_re.reshape(db, N_STATE), st_im.reshape(db, N_STATE), w["pre"], w["pim"],
                                  w["wb"], w["wc"], w["d_skip"], w["w_glu"], w["b_glu"])
    x1, h2 = _merge(x2, y_att, y_ssm, sga, sgs, w["w_att_out"], w["w_ssm_out"], w["w_o"],
                    w["g_post_mix"], w["g_pre_ffn"], db)
    y, up_a, up_b = _ffn(h2, x1, w["w_up"], w["conv_w"], w["conv_b"], w["w_down"], w["g_post_ffn"],
                         st_conv, db, 1, db, tf, False)
    conv_new = jnp.concatenate([st_conv[:, 1:], jnp.concatenate([up_a, up_b], axis=-1)[:, None, :]], axis=1)
    state = (k.reshape(db, 1, N_HEADS, HEAD_DIM), v.reshape(db, 1, N_HEADS, HEAD_DIM),
             kw[:, :IDX_DIM].reshape(db, 1, IDX_DIM),
             s_re.reshape(db, N_GROUPS, STATE_DIM), s_im.reshape(db, N_GROUPS, STATE_DIM), conv_new)
    return y.reshape(db, 1, D_MODEL), state


def kernel(x_prompt, x_sample, cache_k, cache_v, cache_kidx, state_ssm_re, state_ssm_im, state_conv, page_table,
           rel_bias, w_in, g_pre_mix, g_post_mix, lam_re, lam_im, log_dt, b_re, b_im, c_re, c_im, d_skip, w_glu,
           b_glu, w_att_out, w_ssm_out, w_o, g_pre_ffn, g_post_ffn, w_up, conv_w, conv_b, w_down):
    depth = w_in.shape[0]
    t = x_prompt.shape[1]
    past = page_table.shape[1] * PAGE_SIZE
    tq = min(256, t)
    tm = min(512, t)
    chunk = min(256, t)
    tf = 512
    bias_tiles, bias_row = _bias_tiles(rel_bias, tq, past)
    y_p, y_s = x_prompt, x_sample
    outs_p, outs_s = [], []
    for l in range(depth):
        lw = (w_in[l], g_pre_mix[l], g_post_mix[l], lam_re[l], lam_im[l], log_dt[l], b_re[l], b_im[l],
              c_re[l], c_im[l], d_skip[l], w_glu[l], b_glu[l], w_att_out[l], w_ssm_out[l], w_o[l],
              g_pre_ffn[l], g_post_ffn[l], w_up[l], conv_w[l], conv_b[l], w_down[l])
        w = _layer_weights(lw)
        y_p, st_p = _prompt_layer(y_p, w, bias_tiles, tq, tm, chunk, tf)
        y_s, st_s = _sample_layer(y_s, w, bias_row, cache_k[l], cache_v[l], cache_kidx[l], page_table,
                                  state_ssm_re[l], state_ssm_im[l], state_conv[l], tf)
        outs_p.append(st_p)
        outs_s.append(st_s)
    k_p, v_p, ki_p, sr_p, si_p, cv_p = [jnp.stack(a) for a in zip(*outs_p)]
    k_s, v_s, ki_s, sr_s, si_s, cv_s = [jnp.stack(a) for a in zip(*outs_s)]
    return (y_p, y_s, k_p, v_p, ki_p, sr_p, si_p, cv_p, k_s, v_s, ki_s, sr_s, si_s, cv_s)
```

```python
import functools
import math

import numpy as np
import jax
import jax.numpy as jnp
from jax import lax
from jax.experimental import pallas as pl
from jax.experimental.pallas import tpu as pltpu

F32 = jnp.float32
BF16 = jnp.bfloat16
I32 = jnp.int32

D_MODEL = 1024
PAGE_SIZE = 128
N_HEADS = 8
HEAD_DIM = 64
ATT_WIDTH = N_HEADS * HEAD_DIM
IDX_HEADS = 4
IDX_DIM = 64
TOPK_MAX = 256
N_BUCKETS = 32
MAX_DISTANCE = 128
SSM_WIDTH = 512
GROUP = 16
N_GROUPS = SSM_WIDTH // GROUP
STATE_DIM = 64
N_STATE = N_GROUPS * STATE_DIM
D_FF = 4 * D_MODEL
CONV_W = 3
EPS = 1e-6
SPLITS = (ATT_WIDTH, ATT_WIDTH, ATT_WIDTH, IDX_HEADS * IDX_DIM, IDX_DIM, IDX_HEADS, SSM_WIDTH, D_MODEL, D_MODEL)

LANES = 128
KEY_EXCLUDED = -(2 ** 31)
IDX_BITS = 14
BF16_SUBLANES = 16
V_ROWS = HEAD_DIM + BF16_SUBLANES
LOG2_E = math.log2(math.e)
FFN_PART_ROWS = 256
SCAN_GROUP = 8
MASKED = -1e30
VMEM_LIMIT = 56 * 1024 * 1024

_C_Q, _C_K, _C_V = 0, ATT_WIDTH, 2 * ATT_WIDTH
_C_QI = 3 * ATT_WIDTH
_C_KW = _C_QI + IDX_HEADS * IDX_DIM
_C_U = _C_KW + LANES
_C_GA = _C_U + SSM_WIDTH
_C_GS = _C_GA + D_MODEL
_C_END = _C_GS + D_MODEL


def _params(*sem):
    return pltpu.CompilerParams(dimension_semantics=sem, vmem_limit_bytes=VMEM_LIMIT)


def _rms(x, g):
    inv = lax.rsqrt(jnp.mean(x * x, axis=-1, keepdims=True) + EPS)
    return (x * inv) * g


def _dot(a, b):
    return jnp.dot(a, b, preferred_element_type=F32)


def _dot_nt(a, b):
    return lax.dot_general(a, b, (((1,), (1,)), ((), ())), preferred_element_type=F32)


def _bucket_starts():
    n = np.arange(0, 1 << IDX_BITS, dtype=np.int32)
    max_exact = N_BUCKETS // 2
    nf = np.maximum(n, 1).astype(np.float32)
    large = max_exact + (np.log(nf / np.float32(max_exact)) / np.float32(math.log(MAX_DISTANCE / max_exact))
                         * np.float32(N_BUCKETS - max_exact)).astype(np.int32)
    large = np.minimum(large, N_BUCKETS - 1)
    bucket = np.where(n < max_exact, n, large)
    assert np.all(np.diff(bucket) >= 0)
    starts = [int(np.argmax(bucket >= b)) for b in range(N_BUCKETS)]
    assert all(bucket[s] == b for b, s in enumerate(starts))
    return starts


_BUCKET_START = _bucket_starts()


def _bias_tiles_body(rb_ref, tile_ref, row_ref, *, tq, past):
    h = pl.program_id(0)

    def bias_of(dist):
        val = jnp.full(dist.shape, rb_ref[0, h], F32)
        for b in range(1, N_BUCKETS):
            val = jnp.where(dist >= _BUCKET_START[b], rb_ref[b, h], val)
        return val

    key_off = lax.broadcasted_iota(I32, (tq, tq), 0)
    qry_off = lax.broadcasted_iota(I32, (tq, tq), 1)
    far = rb_ref[N_BUCKETS - 1, h]
    tile_ref[0, 1] = (bias_of(qry_off - key_off) - far) * LOG2_E
    tile_ref[0, 0] = (bias_of(qry_off - key_off + tq) - far) * LOG2_E
    lane = lax.broadcasted_iota(I32, (1, past + LANES), 1)
    row_ref[0] = bias_of(past - lane)


def _bias_tiles(rel_bias, tq, past):
    assert tq + 1 >= _BUCKET_START[N_BUCKETS - 1]
    return pl.pallas_call(
        functools.partial(_bias_tiles_body, tq=tq, past=past),
        grid=(N_HEADS,),
        in_specs=[pl.BlockSpec(memory_space=pltpu.SMEM)],
        out_specs=[pl.BlockSpec((1, 2, tq, tq), lambda h: (h, 0, 0, 0)),
                   pl.BlockSpec((1, 1, past + LANES), lambda h: (h, 0, 0))],
        out_shape=[jax.ShapeDtypeStruct((N_HEADS, 2, tq, tq), F32),
                   jax.ShapeDtypeStruct((N_HEADS, 1, past + LANES), F32)],
        compiler_params=_params("arbitrary"),
        name="bias_tiles",
    )(rel_bias)


def _in_proj_body(x_ref, g_ref, w_ref, k_ref, v_ref, kw_ref, u_ref, sga_ref, sgs_ref, *more_refs, seq):
    h = _rms(x_ref[...], g_ref[...]).astype(BF16)

    def proj(c0, c1):
        return _dot(h, w_ref[:, c0:c1])

    q = proj(_C_Q, _C_K) * HEAD_DIM ** -0.5
    k = proj(_C_K, _C_V)
    v = proj(_C_V, _C_QI)
    qi = proj(_C_QI, _C_KW)
    kw = proj(_C_KW, _C_U)
    lane = lax.broadcasted_iota(I32, kw.shape, 1)
    kw = jnp.where(lane >= IDX_DIM, kw * IDX_HEADS ** -0.5, kw)
    kw_ref[...] = kw
    u_ref[...] = proj(_C_U, _C_GA)
    sga_ref[...] = jax.nn.sigmoid(proj(_C_GA, _C_GS))
    sgs_ref[...] = jax.nn.sigmoid(proj(_C_GS, _C_END))
    if seq:
        qt_ref, qit_ref, kwt_ref, kh_ref, vt_ref, ki_ref = more_refs
        tm = k.shape[0]
        vt = v.T.reshape(N_HEADS, HEAD_DIM, tm)
        k_ref[0] = k.T.reshape(N_HEADS, HEAD_DIM, tm)
        v_ref[0] = vt
        qt_ref[...] = (q * LOG2_E).T.reshape(N_HEADS, HEAD_DIM, tm).astype(BF16)
        qit_ref[...] = (qi * IDX_DIM ** -0.5).T.reshape(IDX_HEADS, IDX_DIM, tm).astype(BF16)
        kwt_ref[0] = kw.T
        for hd in range(N_HEADS):
            kh_ref[0, hd] = k[:, hd * HEAD_DIM:(hd + 1) * HEAD_DIM].astype(BF16)
        vt_ref[0, :, :HEAD_DIM, :] = vt.astype(BF16)
        vt_ref[0, :, HEAD_DIM:, :] = jnp.ones((N_HEADS, V_ROWS - HEAD_DIM, tm), BF16)
        ki_ref[0] = kw[:, :IDX_DIM].astype(BF16)
    else:
        q_ref, qi_ref = more_refs
        k_ref[...] = k
        v_ref[...] = v
        q_ref[...] = q.astype(BF16)
        qi_ref[...] = qi


def _in_proj(x2, g, w_packed, bsz, t, tm, seq):
    m = bsz * t
    nt = t // tm
    row = lambda b, i: (b * nt + i, 0)
    widths = (LANES, SSM_WIDTH, D_MODEL, D_MODEL)
    if seq:
        kv_spec = pl.BlockSpec((1, N_HEADS, HEAD_DIM, tm), lambda b, i: (b, 0, 0, i))
        kv_shape = jax.ShapeDtypeStruct((bsz, N_HEADS, HEAD_DIM, t), F32)
    else:
        kv_spec = pl.BlockSpec((tm, ATT_WIDTH), row)
        kv_shape = jax.ShapeDtypeStruct((m, ATT_WIDTH), F32)
    out_specs = [kv_spec, kv_spec] + [pl.BlockSpec((tm, w), row) for w in widths]
    out_shape = [kv_shape, kv_shape] + [jax.ShapeDtypeStruct((m, w), F32) for w in widths]
    if seq:
        col = lambda b, i: (0, 0, b * nt + i)
        out_specs += [pl.BlockSpec((N_HEADS, HEAD_DIM, tm), col),
                      pl.BlockSpec((IDX_HEADS, IDX_DIM, tm), col),
                      pl.BlockSpec((1, LANES, tm), lambda b, i: (b, 0, i)),
                      pl.BlockSpec((1, N_HEADS, tm, HEAD_DIM), lambda b, i: (b, 0, i, 0)),
                      pl.BlockSpec((1, N_HEADS, V_ROWS, tm), lambda b, i: (b, 0, 0, i)),
                      pl.BlockSpec((1, tm, IDX_DIM), lambda b, i: (b, i, 0))]
        out_shape += [jax.ShapeDtypeStruct((N_HEADS, HEAD_DIM, m), BF16),
                      jax.ShapeDtypeStruct((IDX_HEADS, IDX_DIM, m), BF16),
                      jax.ShapeDtypeStruct((bsz, LANES, t), F32),
                      jax.ShapeDtypeStruct((bsz, N_HEADS, t, HEAD_DIM), BF16),
                      jax.ShapeDtypeStruct((bsz, N_HEADS, V_ROWS, t), BF16),
                      jax.ShapeDtypeStruct((bsz, t, IDX_DIM), BF16)]
    else:
        out_specs += [pl.BlockSpec((tm, ATT_WIDTH), row), pl.BlockSpec((tm, IDX_HEADS * IDX_DIM), row)]
        out_shape += [jax.ShapeDtypeStruct((m, ATT_WIDTH), BF16),
                      jax.ShapeDtypeStruct((m, IDX_HEADS * IDX_DIM), F32)]
    return pl.pallas_call(
        functools.partial(_in_proj_body, seq=seq),
        grid=(bsz, nt),
        in_specs=[pl.BlockSpec((tm, D_MODEL), row),
                  pl.BlockSpec((1, D_MODEL), lambda b, i: (0, 0)),
                  pl.BlockSpec((D_MODEL, _C_END), lambda b, i: (0, 0))],
        out_specs=out_specs,
        out_shape=out_shape,
        compiler_params=_params("arbitrary", "arbitrary"),
        name="in_proj",
    )(x2, g, w_packed)


def _fori_by_two(lo, hi, body, init):
    n_pairs = (hi - lo) // 2
    carry = lax.fori_loop(0, n_pairs, lambda t, c: body(lo + 2 * t + 1, body(lo + 2 * t, c)), init)
    return lax.fori_loop(lo + 2 * n_pairs, hi, body, carry)


def _sortable_key(score):
    bits = pltpu.bitcast(score, I32)
    return jnp.where(bits < 0, bits ^ 0x7FFFFFFF, bits)


def _topk_mask(keys_ref, nblk, height, topk):
    n_lanes = keys_ref.shape[1]
    n_parts = height // 8
    assert n_parts * 8 == height and n_parts & (n_parts - 1) == 0

    def row_block(j):
        return pl.ds(pl.multiple_of(j * height, height), height)

    def count(pred, bound):
        def block_hits(j):
            hit = jnp.where(pred(keys_ref[row_block(j), :], bound), 1, 0)
            parts = [hit[r:r + 8] for r in range(0, height, 8)]
            while len(parts) > 1:
                parts = [a + b for a, b in zip(parts[::2], parts[1::2])]
            return parts[0]

        acc = _fori_by_two(0, nblk, lambda j, acc: acc + block_hits(j), jnp.zeros((8, n_lanes), I32))
        return jnp.sum(acc, axis=0, keepdims=True)

    zero = jnp.zeros((1, n_lanes), I32)

    prefix = jnp.where(count(lambda k, b: k >= b, zero) >= topk, 0, KEY_EXCLUDED).astype(I32)

    def value_bit(b, prefix):
        cand = prefix | jnp.left_shift(1, 30 - b)
        return jnp.where(count(lambda k, c: k >= c, cand) >= topk, cand, prefix)
    thr = lax.fori_loop(0, 31, value_bit, prefix)

    n_above = count(lambda k, t: k > t, thr)
    need = jnp.where(thr == KEY_EXCLUDED, 0, topk - n_above).astype(F32)
    at_or_before = (lax.broadcasted_iota(I32, (height, height), 0)
                    >= lax.broadcasted_iota(I32, (height, height), 1))
    lower_ones = jnp.where(at_or_before, 1.0, 0.0).astype(BF16)

    def emit(j, seen):
        k = keys_ref[row_block(j), :]
        tie = k == thr
        rank = seen + _dot(lower_ones, jnp.where(tie, 1.0, 0.0).astype(BF16))
        mask = jnp.where(k > thr, 0.0, jnp.where(tie, jnp.where(rank <= need, 0.0, MASKED), MASKED))
        keys_ref[row_block(j), :] = pltpu.bitcast(mask.astype(F32), I32)
        return rank[height - 1:, :]
    lax.fori_loop(0, nblk, emit, jnp.zeros((1, n_lanes), F32))


def _prompt_attn_body(qt_ref, qit_ref, wt_ref, kh_ref, vt_ref, ki_ref, bias_ref, o_ref, keys_ref, m_ref, acc_ref,
                      alpha_ref, p_ref, *, tq, topk):
    i = pl.program_id(1)
    nblk = i + 1
    key_off = lax.broadcasted_iota(I32, (tq, tq), 0)
    qry_off = lax.broadcasted_iota(I32, (tq, tq), 1)

    def key_block(j):
        return pl.ds(pl.multiple_of(j * tq, tq), tq)

    qit = jnp.concatenate([qit_ref[h] for h in range(IDX_HEADS)], axis=1)
    wt = [wt_ref[0, h:h + 1, :] for h in range(IDX_HEADS)]

    def score_block(j, _, diagonal):
        ki = ki_ref[0, key_block(j), :]
        dots = _dot(ki, qit)
        s = None
        for h in range(IDX_HEADS):
            sh = jnp.maximum(dots[:, h * tq:(h + 1) * tq], 0.0) * wt[h]
            s = sh if s is None else s + sh
        key = _sortable_key(s)
        if diagonal:
            key = jnp.where(key_off <= qry_off, key, KEY_EXCLUDED)
        keys_ref[key_block(j), :] = key
        return 0
    _fori_by_two(0, i, functools.partial(score_block, diagonal=False), 0)
    score_block(i, 0, diagonal=True)

    _topk_mask(keys_ref, nblk, tq, topk)

    m_ref[...] = jnp.full(m_ref.shape, MASKED, F32)
    acc_ref[...] = jnp.zeros(acc_ref.shape, F32)
    p_ref[...] = jnp.zeros(p_ref.shape, BF16)
    alpha_ref[...] = jnp.ones(alpha_ref.shape, F32)

    def accumulate(j):
        for h in range(N_HEADS):
            acc_ref[h] = alpha_ref[h] * acc_ref[h] + _dot(vt_ref[0, h, :, key_block(j)], p_ref[h])

    def attend(j, _, near):
        accumulate(jnp.maximum(j - 1, 0))
        masked = pltpu.bitcast(keys_ref[key_block(j), :], F32)
        for h in range(N_HEADS):
            s = _dot(kh_ref[0, h, key_block(j), :], qt_ref[h]) + masked
            if near:
                s = s + bias_ref[h, j - i + 1]
            m_old = m_ref[h]
            m_new = jnp.maximum(m_old, jnp.max(s, axis=0, keepdims=True))
            p_ref[h] = jnp.exp2(s - m_new).astype(BF16)
            alpha_ref[h] = jnp.exp2(m_old - m_new)
            m_ref[h] = m_new
        return 0

    n_far = jnp.maximum(i - 1, 0)
    lax.fori_loop(0, n_far, functools.partial(attend, near=False), 0)
    lax.fori_loop(n_far, nblk, functools.partial(attend, near=True), 0)
    accumulate(i)
    out_t = jnp.concatenate(
        [acc_ref[h, :HEAD_DIM, :] / acc_ref[h, HEAD_DIM:HEAD_DIM + 1, :] for h in range(N_HEADS)], axis=0)
    o_ref[...] = out_t.T


def _prompt_attn(qt, qit, kwt, kh, vt, ki, bias_tiles, bsz, t, tq, topk):
    nq = t // tq
    col = lambda b, i: (0, 0, b * nq + i)
    whole = dict(pipeline_mode=pl.Buffered(1))
    return pl.pallas_call(
        functools.partial(_prompt_attn_body, tq=tq, topk=topk),
        grid=(bsz, nq),
        in_specs=[pl.BlockSpec((N_HEADS, HEAD_DIM, tq), col),
                  pl.BlockSpec((IDX_HEADS, IDX_DIM, tq), col),
                  pl.BlockSpec((1, 8, tq), lambda b, i: (b, IDX_DIM // 8, i)),
                  pl.BlockSpec((1, N_HEADS, t, HEAD_DIM), lambda b, i: (b, 0, 0, 0), **whole),
                  pl.BlockSpec((1, N_HEADS, V_ROWS, t), lambda b, i: (b, 0, 0, 0), **whole),
                  pl.BlockSpec((1, t, IDX_DIM), lambda b, i: (b, 0, 0), **whole),
                  pl.BlockSpec((N_HEADS, 2, tq, tq), lambda b, i: (0, 0, 0, 0), **whole)],
        out_specs=pl.BlockSpec((tq, ATT_WIDTH), lambda b, i: (b * nq + i, 0)),
        out_shape=jax.ShapeDtypeStruct((bsz * t, ATT_WIDTH), F32),
        scratch_shapes=[pltpu.VMEM((t, tq), I32),
                        pltpu.VMEM((N_HEADS, 1, tq), F32), pltpu.VMEM((N_HEADS, V_ROWS, tq), F32),
                        pltpu.VMEM((N_HEADS, 1, tq), F32), pltpu.VMEM((N_HEADS, tq, tq), BF16)],
        compiler_params=_params("arbitrary", "arbitrary"),
        name="prompt_attn",
    )(qt, qit, kwt, kh, vt, ki, bias_tiles)


def _sample_scores_body(pt_ref, qi_ref, w_ref, kn_ref, *rest, n_pages):
    page_refs, key_ref = rest[:n_pages], rest[n_pages]
    qi = qi_ref[...].astype(BF16)
    w = w_ref[...]

    def weighted(s):
        return jnp.sum(jnp.maximum(s * IDX_DIM ** -0.5, 0.0) * w, axis=0, keepdims=True)

    for p in range(n_pages):
        s = _dot(qi, page_refs[p][...].astype(BF16))
        key_ref[:, p * PAGE_SIZE:(p + 1) * PAGE_SIZE] = _sortable_key(weighted(s))
    s_self = jnp.sum(qi.astype(F32) * kn_ref[...].astype(BF16).astype(F32), axis=1, keepdims=True)
    lane = lax.broadcasted_iota(I32, (1, LANES), 1)
    key_ref[:, n_pages * PAGE_SIZE:] = jnp.where(lane == 0, _sortable_key(weighted(s_self)), KEY_EXCLUDED)


def _sample_scores(page_table, qi3, w3, kn3, cache_kidx):
    db, n_pages = page_table.shape
    n_cols = n_pages * PAGE_SIZE + LANES
    per = lambda s, pt: (s, 0, 0)
    page_specs = [pl.BlockSpec((None, IDX_DIM, PAGE_SIZE), functools.partial(lambda s, pt, p: (pt[s, p], 0, 0), p=p))
                  for p in range(n_pages)]
    return pl.pallas_call(
        functools.partial(_sample_scores_body, n_pages=n_pages),
        grid_spec=pltpu.PrefetchScalarGridSpec(
            num_scalar_prefetch=1, grid=(db,),
            in_specs=[pl.BlockSpec((None, IDX_HEADS, IDX_DIM), per),
                      pl.BlockSpec((None, IDX_HEADS, 1), per),
                      pl.BlockSpec((None, 1, IDX_DIM), per)] + page_specs,
            out_specs=pl.BlockSpec((None, 1, n_cols), per)),
        out_shape=jax.ShapeDtypeStruct((db, 1, n_cols), I32),
        compiler_params=_params("arbitrary"),
        name="sample_scores",
    )(page_table, qi3, w3, kn3, *([cache_kidx] * n_pages))


def _sample_select_body(key_ref, mask_ref, keys_scr, *, n_cols, topk):
    keys_scr[...] = key_ref[...].T
    _topk_mask(keys_scr, n_cols // LANES, LANES, topk)
    mask_ref[...] = pltpu.bitcast(keys_scr[...], F32).T


def _sample_select(keys, topk):
    n_rows, n_cols = keys.shape
    return pl.pallas_call(
        functools.partial(_sample_select_body, n_cols=n_cols, topk=topk),
        out_shape=jax.ShapeDtypeStruct((n_rows, n_cols), F32),
        scratch_shapes=[pltpu.VMEM((n_cols, n_rows), I32)],
        compiler_params=pltpu.CompilerParams(vmem_limit_bytes=VMEM_LIMIT),
        name="sample_select",
    )(keys)


def _sample_attn_body(pt_ref, q_ref, qt_ref, kn_ref, vnt_ref, mask_ref, bias_ref, *rest, n_pages):
    kt_refs, vt_refs, o_ref = rest[:n_pages], rest[n_pages:2 * n_pages], rest[2 * n_pages]
    past = n_pages * PAGE_SIZE
    mask = mask_ref[...]
    s_self_all = jnp.sum(q_ref[...] * kn_ref[...], axis=1, keepdims=True)
    for h in range(N_HEADS):
        q_col = qt_ref[:, h:h + 1]
        bias = bias_ref[h]
        s = jnp.concatenate([jnp.sum(kt_refs[p][h] * q_col, axis=0, keepdims=True) for p in range(n_pages)], axis=1)
        s = s + bias[:, :past] + mask[:, :past]
        s_self = s_self_all[h:h + 1, :] + bias[:, past:past + 1] + mask[:, past:past + 1]
        m = jnp.maximum(jnp.max(s, axis=1, keepdims=True), s_self)
        p_past = jnp.exp(s - m)
        p_self = jnp.exp(s_self - m)
        l = jnp.sum(p_past, axis=1, keepdims=True) + p_self
        acc = vt_refs[0][h] * p_past[:, :PAGE_SIZE]
        for p in range(1, n_pages):
            acc = acc + vt_refs[p][h] * p_past[:, p * PAGE_SIZE:(p + 1) * PAGE_SIZE]
        out = jnp.sum(acc, axis=1, keepdims=True) + p_self * vnt_ref[:, h:h + 1]
        o_ref[:, h:h + 1] = out / l


def _sample_attn(page_table, q3, qt3, kn3, vnt3, mask3, bias_row, cache_kt, cache_vt):
    db, n_pages = page_table.shape
    n_cols = n_pages * PAGE_SIZE + LANES
    per = lambda s, pt: (s, 0, 0)
    page_specs = [pl.BlockSpec((None, N_HEADS, HEAD_DIM, PAGE_SIZE),
                               functools.partial(lambda s, pt, p: (pt[s, p], 0, 0, 0), p=p)) for p in range(n_pages)]
    return pl.pallas_call(
        functools.partial(_sample_attn_body, n_pages=n_pages),
        grid_spec=pltpu.PrefetchScalarGridSpec(
            num_scalar_prefetch=1, grid=(db,),
            in_specs=[pl.BlockSpec((None, N_HEADS, HEAD_DIM), per),
                      pl.BlockSpec((None, HEAD_DIM, N_HEADS), per),
                      pl.BlockSpec((None, N_HEADS, HEAD_DIM), per),
                      pl.BlockSpec((None, HEAD_DIM, N_HEADS), per),
                      pl.BlockSpec((None, 1, n_cols), per),
                      pl.BlockSpec((N_HEADS, 1, n_cols), lambda s, pt: (0, 0, 0))] + page_specs + page_specs,
            out_specs=pl.BlockSpec((None, HEAD_DIM, N_HEADS), per)),
        out_shape=jax.ShapeDtypeStruct((db, HEAD_DIM, N_HEADS), F32),
        compiler_params=_params("arbitrary"),
        name="sample_attn",
    )(page_table, q3, qt3, kn3, vnt3, mask3, bias_row, *([cache_kt] * n_pages), *([cache_vt] * n_pages))


def _cmul(ar, ai, br, bi):
    return ar * br - ai * bi, ar * bi + ai * br


def _ssm_prep_body(lre_f, lim_f, ldt_f, lre_r, lim_r, ldt_r, bre_ref, bim_ref, pre_ref, pim_ref, bbr_ref, bbi_ref,
                   *, chunk):
    def lam_bar(lre, lim, ldt):
        dt = jnp.exp(ldt)
        mag = jnp.exp(lre * dt)
        return mag * jnp.cos(lim * dt), mag * jnp.sin(lim * dt)

    lbr, lbi = lam_bar(lre_f[...], lim_f[...], ldt_f[...])
    pr = jnp.broadcast_to(lbr, (chunk, N_STATE))
    pi = jnp.broadcast_to(lbi, (chunk, N_STATE))
    row = lax.broadcasted_iota(I32, (chunk, N_STATE), 0)
    d = 1
    while d < chunk:
        sr = jnp.where(row >= d, pltpu.roll(pr, d, 0), 1.0)
        si = jnp.where(row >= d, pltpu.roll(pi, d, 0), 0.0)
        pr, pi = _cmul(pr, pi, sr, si)
        d *= 2
    pre_ref[...] = pr
    pim_ref[...] = pi

    lre, lim = lre_r[...], lim_r[...]
    lbr, lbi = lam_bar(lre, lim, ldt_r[...])
    den = lre * lre + lim * lim
    nr, ni = lbr - 1.0, lbi
    cr = (nr * lre + ni * lim) / den
    ci = (ni * lre - nr * lim) / den
    bbr, bbi = _cmul(cr, ci, bre_ref[...], bim_ref[...])
    bbr_ref[...] = bbr
    bbi_ref[...] = bbi


def _ssm_prep(lam_re, lam_im, log_dt, b_re, b_im, chunk):
    flat = lambda a: a.reshape(1, N_STATE)
    rep = lambda a: jnp.repeat(a, GROUP, axis=0)
    ldt = jnp.broadcast_to(log_dt[:, None], (N_GROUPS, STATE_DIM))
    bt = lambda b: jnp.swapaxes(b, 1, 2).reshape(SSM_WIDTH, STATE_DIM)
    return pl.pallas_call(
        functools.partial(_ssm_prep_body, chunk=chunk),
        out_shape=[jax.ShapeDtypeStruct((chunk, N_STATE), F32), jax.ShapeDtypeStruct((chunk, N_STATE), F32),
                   jax.ShapeDtypeStruct((SSM_WIDTH, STATE_DIM), F32), jax.ShapeDtypeStruct((SSM_WIDTH, STATE_DIM), F32)],
        compiler_params=pltpu.CompilerParams(vmem_limit_bytes=VMEM_LIMIT),
        name="ssm_prep",
    )(flat(lam_re), flat(lam_im), flat(ldt), rep(lam_re), rep(lam_im), rep(ldt), bt(b_re), bt(b_im))


def _block_diag(blocks):
    g, r, c = blocks.shape
    eye = jnp.eye(g, dtype=blocks.dtype)
    return (blocks[:, :, None, :] * eye[:, None, :, None]).reshape(g * r, g * c)


def _ssm_readout(u, xr, xi, wc_ref, dskip_ref, wglu_ref, bglu_ref):
    y = _dot(jnp.concatenate([xr, xi], axis=1).astype(BF16), wc_ref[...]) + dskip_ref[...] * u
    g = jax.nn.gelu(y)
    return g * jax.nn.sigmoid(_dot(g.astype(BF16), wglu_ref[...]) + bglu_ref[...])


def _ssm_scan_body(u_ref, x0r_ref, x0i_ref, pre_ref, pim_ref, wb_ref, wc_ref, dskip_ref, wglu_ref, bglu_ref,
                   y_ref, sr_ref, si_ref, cr_ref, ci_ref, xr_ref, xi_ref, *, chunk):
    c = pl.program_id(1)

    @pl.when(c == 0)
    def _():
        cr_ref[...] = x0r_ref[0]
        ci_ref[...] = x0i_ref[0]

    u = u_ref[...]
    bu = _dot(u.astype(BF16), wb_ref[...])
    n_groups = chunk // SCAN_GROUP
    xr = bu[:, :N_STATE].reshape(n_groups, SCAN_GROUP, N_STATE)
    xi = bu[:, N_STATE:].reshape(n_groups, SCAN_GROUP, N_STATE)
    pos = lax.broadcasted_iota(I32, (SCAN_GROUP, N_STATE), 0)
    d = 1
    while d < SCAN_GROUP:
        ar = jnp.where(pos >= d, pre_ref[d - 1:d, :], 0.0)
        ai = jnp.where(pos >= d, pim_ref[d - 1:d, :], 0.0)
        tr, ti = _cmul(ar, ai, pltpu.roll(xr, d, 1), pltpu.roll(xi, d, 1))
        xr, xi = xr + tr, xi + ti
        d *= 2
    pr, pi = pre_ref[...], pim_ref[...]
    cr, ci = cr_ref[...], ci_ref[...]
    for g in range(n_groups):
        tr, ti = _cmul(pr, pi, cr, ci)
        gr, gi = xr[g] + tr, xi[g] + ti
        xr_ref[g * SCAN_GROUP:(g + 1) * SCAN_GROUP, :] = gr
        xi_ref[g * SCAN_GROUP:(g + 1) * SCAN_GROUP, :] = gi
        cr, ci = gr[SCAN_GROUP - 1:, :], gi[SCAN_GROUP - 1:, :]
    cr_ref[...] = cr
    ci_ref[...] = ci
    sr_ref[0] = cr
    si_ref[0] = ci
    y_ref[...] = _ssm_readout(u, xr_ref[...], xi_ref[...], wc_ref, dskip_ref, wglu_ref, bglu_ref)


def _ssm_scan(u, x0r, x0i, pre, pim, wb, wc, dskip, wglu, bglu, bsz, t, chunk):
    nc = t // chunk
    row = lambda b, c: (b * nc + c, 0)
    const = lambda b, c: (0, 0)
    state = pl.BlockSpec((1, 1, N_STATE), lambda b, c: (b, 0, 0))
    return pl.pallas_call(
        functools.partial(_ssm_scan_body, chunk=chunk),
        grid=(bsz, nc),
        in_specs=[pl.BlockSpec((chunk, SSM_WIDTH), row), state, state,
                  pl.BlockSpec((SCAN_GROUP, N_STATE), const), pl.BlockSpec((SCAN_GROUP, N_STATE), const),
                  pl.BlockSpec((SSM_WIDTH, 2 * N_STATE), const), pl.BlockSpec((2 * N_STATE, SSM_WIDTH), const),
                  pl.BlockSpec((1, SSM_WIDTH), const), pl.BlockSpec((SSM_WIDTH, SSM_WIDTH), const),
                  pl.BlockSpec((1, SSM_WIDTH), const)],
        out_specs=[pl.BlockSpec((chunk, SSM_WIDTH), row), state, state],
        out_shape=[jax.ShapeDtypeStruct((bsz * t, SSM_WIDTH), F32),
                   jax.ShapeDtypeStruct((bsz, 1, N_STATE), F32), jax.ShapeDtypeStruct((bsz, 1, N_STATE), F32)],
        scratch_shapes=[pltpu.VMEM((1, N_STATE), F32), pltpu.VMEM((1, N_STATE), F32),
                        pltpu.VMEM((chunk, N_STATE), F32), pltpu.VMEM((chunk, N_STATE), F32)],
        compiler_params=_params("arbitrary", "arbitrary"),
        name="ssm_scan",
    )(u, x0r, x0i, pre, pim, wb, wc, dskip, wglu, bglu)


def _ssm_step_body(u_ref, x0r_ref, x0i_ref, pre_ref, pim_ref, wb_ref, wc_ref, dskip_ref, wglu_ref, bglu_ref,
                   y_ref, sr_ref, si_ref):
    u = u_ref[...]
    bu = _dot(u.astype(BF16), wb_ref[...])
    tr, ti = _cmul(pre_ref[0:1, :], pim_ref[0:1, :], x0r_ref[...], x0i_ref[...])
    xr, xi = bu[:, :N_STATE] + tr, bu[:, N_STATE:] + ti
    sr_ref[...] = xr
    si_ref[...] = xi
    y_ref[...] = _ssm_readout(u, xr, xi, wc_ref, dskip_ref, wglu_ref, bglu_ref)


def _ssm_step(u, x0r, x0i, pre, pim, wb, wc, dskip, wglu, bglu):
    n = u.shape[0]
    return pl.pallas_call(
        _ssm_step_body,
        out_shape=[jax.ShapeDtypeStruct((n, SSM_WIDTH), F32),
                   jax.ShapeDtypeStruct((n, N_STATE), F32), jax.ShapeDtypeStruct((n, N_STATE), F32)],
        compiler_params=pltpu.CompilerParams(vmem_limit_bytes=VMEM_LIMIT),
        name="ssm_step",
    )(u, x0r, x0i, pre, pim, wb, wc, dskip, wglu, bglu)


def _merge_body(x_ref, ya_ref, ys_ref, sga_ref, sgs_ref, wa_ref, ws_ref, wo_ref, gpost_ref, gpre_ref,
                x1_ref, h2_ref):
    merged = (sga_ref[...] * _dot(ya_ref[...].astype(BF16), wa_ref[...])
              + sgs_ref[...] * _dot(ys_ref[...].astype(BF16), ws_ref[...]))
    x1 = x_ref[...] + _rms(_dot(merged.astype(BF16), wo_ref[...]), gpost_ref[...])
    x1_ref[...] = x1
    h2_ref[...] = _rms(x1, gpre_ref[...]).astype(BF16)


def _merge(x2, ya, ys, sga, sgs, wa, ws, wo, gpost, gpre, tm):
    m = x2.shape[0]
    row = lambda i: (i, 0)
    const = lambda i: (0, 0)
    return pl.pallas_call(
        _merge_body,
        grid=(m // tm,),
        in_specs=[pl.BlockSpec((tm, D_MODEL), row), pl.BlockSpec((tm, ATT_WIDTH), row),
                  pl.BlockSpec((tm, SSM_WIDTH), row), pl.BlockSpec((tm, D_MODEL), row),
                  pl.BlockSpec((tm, D_MODEL), row),
                  pl.BlockSpec((ATT_WIDTH, D_MODEL), const), pl.BlockSpec((SSM_WIDTH, D_MODEL), const),
                  pl.BlockSpec((D_MODEL, D_MODEL), const), pl.BlockSpec((1, D_MODEL), const),
                  pl.BlockSpec((1, D_MODEL), const)],
        out_specs=[pl.BlockSpec((tm, D_MODEL), row), pl.BlockSpec((tm, D_MODEL), row)],
        out_shape=[jax.ShapeDtypeStruct((m, D_MODEL), F32), jax.ShapeDtypeStruct((m, D_MODEL), BF16)],
        compiler_params=_params("arbitrary"),
        name="merge",
    )(x2, ya, ys, sga, sgs, wa, ws, wo, gpost, gpre)


def _ffn_body(h_ref, x_ref, wua_ref, wub_ref, cwa_ref, cwb_ref, cba_ref, cbb_ref, wd_ref, g_ref, pa_ref, pb_ref,
              y_ref, oa_ref, ob_ref, f_ref, *carry, seq, tiles_per_seq):
    i, j = pl.program_id(0), pl.program_id(1)
    nj = pl.num_programs(1)
    tm = h_ref.shape[0]
    half = min(FFN_PART_ROWS, tm // 2)
    halves = [slice(r, r + half) for r in range(0, tm, half)]
    ca, cb = carry if seq else (None, None)

    @pl.when(j == 0)
    def _():
        f_ref[...] = jnp.zeros(f_ref.shape, F32)

    if seq:
        @pl.when(i % tiles_per_seq == 0)
        def _():
            ca[j, 0:CONV_W - 1, :] = pa_ref[0]
            cb[j, 0:CONV_W - 1, :] = pb_ref[0]

    ups = [(_dot(h_ref[rows, :], wua_ref[...]), _dot(h_ref[rows, :], wub_ref[...])) for rows in halves]

    def conv(up, before, rows, cw_ref, cb_ref, prev_ref):
        if seq:
            row = lax.broadcasted_iota(I32, up.shape, 0)
            m1 = jnp.where(row == 0, before[1:2], pltpu.roll(up, 1, 0))
            m2 = jnp.where(row == 0, before[0:1], jnp.where(row == 1, before[1:2], pltpu.roll(up, 2, 0)))
        else:
            m2, m1 = prev_ref[rows, 0, :], prev_ref[rows, 1, :]
        return cb_ref[...] + m2 * cw_ref[0:1, :] + m1 * cw_ref[1:2, :] + up * cw_ref[2:3, :]

    before_a = ca[j, 0:CONV_W - 1, :] if seq else None
    before_b = cb[j, 0:CONV_W - 1, :] if seq else None
    for rows, (up_a, up_b) in zip(halves, ups):
        a = conv(up_a, before_a, rows, cwa_ref, cba_ref, pa_ref)
        b = conv(up_b, before_b, rows, cwb_ref, cbb_ref, pb_ref)
        f_ref[rows, :] += _dot((jax.nn.gelu(a) * b).astype(BF16), wd_ref[...])
        if seq:
            before_a, before_b = up_a[half - (CONV_W - 1):, :], up_b[half - (CONV_W - 1):, :]
        else:
            oa_ref[rows, :] = up_a
            ob_ref[rows, :] = up_b
    if seq:
        ca[j, 0:CONV_W - 1, :] = before_a
        cb[j, 0:CONV_W - 1, :] = before_b
        oa_ref[0] = before_a
        ob_ref[0] = before_b

    @pl.when(j == nj - 1)
    def _():
        y_ref[...] = x_ref[...] + _rms(f_ref[...], g_ref[...])


def _ffn(h2, x1, w_up, conv_w, conv_b, w_down, g_post, conv_prev, bsz, t, tm, tf, seq):
    m = bsz * t
    nj = D_FF // tf
    row = lambda i, j: (i, 0)
    if seq:
        tiles_per_seq = t // tm
        prev_a = pl.BlockSpec((1, CONV_W - 1, tf), lambda i, j: (i // tiles_per_seq, 0, j))
        prev_b = pl.BlockSpec((1, CONV_W - 1, tf), lambda i, j: (i // tiles_per_seq, 0, j + nj))
        out_tail = pl.BlockSpec((1, CONV_W - 1, tf), lambda i, j: (i, 0, j))
        tail_shape = jax.ShapeDtypeStruct((m // tm, CONV_W - 1, D_FF), F32)
        scratch = [pltpu.VMEM((nj, 8, tf), F32), pltpu.VMEM((nj, 8, tf), F32)]
    else:
        tiles_per_seq = 1
        prev_a = pl.BlockSpec((tm, CONV_W - 1, tf), lambda i, j: (i, 0, j))
        prev_b = pl.BlockSpec((tm, CONV_W - 1, tf), lambda i, j: (i, 0, j + nj))
        out_tail = pl.BlockSpec((tm, tf), lambda i, j: (i, j))
        tail_shape = jax.ShapeDtypeStruct((m, D_FF), F32)
        scratch = []
    return pl.pallas_call(
        functools.partial(_ffn_body, seq=seq, tiles_per_seq=tiles_per_seq),
        grid=(m // tm, nj),
        in_specs=[pl.BlockSpec((tm, D_MODEL), row), pl.BlockSpec((tm, D_MODEL), row),
                  pl.BlockSpec((D_MODEL, tf), lambda i, j: (0, j)),
                  pl.BlockSpec((D_MODEL, tf), lambda i, j: (0, j + nj)),
                  pl.BlockSpec((CONV_W, tf), lambda i, j: (0, j)),
                  pl.BlockSpec((CONV_W, tf), lambda i, j: (0, j + nj)),
                  pl.BlockSpec((1, tf), lambda i, j: (0, j)),
                  pl.BlockSpec((1, tf), lambda i, j: (0, j + nj)),
                  pl.BlockSpec((tf, D_MODEL), lambda i, j: (j, 0)),
                  pl.BlockSpec((1, D_MODEL), lambda i, j: (0, 0)),
                  prev_a, prev_b],
        out_specs=[pl.BlockSpec((tm, D_MODEL), row), out_tail, out_tail],
        out_shape=[jax.ShapeDtypeStruct((m, D_MODEL), F32), tail_shape, tail_shape],
        scratch_shapes=[pltpu.VMEM((tm, D_MODEL), F32)] + scratch,
        compiler_params=_params("arbitrary", "arbitrary"),
        name="conv_ffn",
    )(h2, x1, w_up, w_up, conv_w, conv_w, conv_b, conv_b, w_down, g_post, conv_prev, conv_prev)


def _pack_w_in(w_in):
    points = np.cumsum(SPLITS)[:-1].tolist()
    wq, wk, wv, wqi, wki, wwi, wu, wga, wgs = jnp.split(w_in, points, axis=-1)
    pad = jnp.zeros((D_MODEL, LANES - IDX_DIM - IDX_HEADS), w_in.dtype)
    return jnp.concatenate([wq, wk, wv, wqi, wki, wwi, pad, wu, wga, wgs], axis=-1).astype(BF16)


def _layer_weights(lw):
    (w_in, g_pre_mix, g_post_mix, lam_re, lam_im, log_dt, b_re, b_im, c_re, c_im, d_skip,
     w_glu, b_glu, w_att_out, w_ssm_out, w_o, g_pre_ffn, g_post_ffn, w_up, conv_w, conv_b, w_down) = lw
    pre, pim, bbr, bbi = _ssm_prep(lam_re, lam_im, log_dt, b_re, b_im, SCAN_GROUP)
    wb = jnp.concatenate([_block_diag(bbr.reshape(N_GROUPS, GROUP, STATE_DIM)),
                          _block_diag(bbi.reshape(N_GROUPS, GROUP, STATE_DIM))], axis=1).astype(BF16)
    wc = jnp.concatenate([_block_diag(jnp.swapaxes(c_re, 1, 2)),
                          _block_diag(-jnp.swapaxes(c_im, 1, 2))], axis=0).astype(BF16)
    vec = lambda a: a.reshape(1, -1)
    return dict(
        w_in=_pack_w_in(w_in), g_pre_mix=vec(g_pre_mix), g_post_mix=vec(g_post_mix),
        pre=pre, pim=pim, wb=wb, wc=wc, d_skip=vec(d_skip), w_glu=w_glu.astype(BF16), b_glu=vec(b_glu),
        w_att_out=w_att_out.astype(BF16), w_ssm_out=w_ssm_out.astype(BF16), w_o=w_o.astype(BF16),
        g_pre_ffn=vec(g_pre_ffn), g_post_ffn=vec(g_post_ffn), w_up=w_up.astype(BF16), conv_w=conv_w,
        conv_b=vec(conv_b), w_down=w_down.astype(BF16))


def _prompt_layer(x, w, bias_tiles, tq, tm, chunk, tf):
    bsz, t, _ = x.shape
    x2 = x.reshape(bsz * t, D_MODEL)
    k, v, kw, u, sga, sgs, qt, qit, kwt, kh, vt, ki = _in_proj(x2, w["g_pre_mix"], w["w_in"], bsz, t, tm, True)
    y_att = _prompt_attn(qt, qit, kwt, kh, vt, ki, bias_tiles, bsz, t, tq, min(TOPK_MAX, t // 4))
    zero_state = jnp.zeros((bsz, 1, N_STATE), F32)
    y_ssm, s_re, s_im = _ssm_scan(u, zero_state, zero_state, w["pre"], w["pim"], w["wb"], w["wc"], w["d_skip"],
                                  w["w_glu"], w["b_glu"], bsz, t, chunk)
    x1, h2 = _merge(x2, y_att, y_ssm, sga, sgs, w["w_att_out"], w["w_ssm_out"], w["w_o"],
                    w["g_post_mix"], w["g_pre_ffn"], tm)
    zero_conv = jnp.zeros((bsz, CONV_W - 1, 2 * D_FF), F32)
    tm_ffn = min(2 * tm, t)
    y, tail_a, tail_b = _ffn(h2, x1, w["w_up"], w["conv_w"], w["conv_b"], w["w_down"], w["g_post_ffn"],
                             zero_conv, bsz, t, tm_ffn, tf, True)
    state = (jnp.transpose(k, (0, 3, 1, 2)), jnp.transpose(v, (0, 3, 1, 2)),
             jnp.swapaxes(kwt[:, :IDX_DIM, :], 1, 2),
             s_re.reshape(bsz, N_GROUPS, STATE_DIM), s_im.reshape(bsz, N_GROUPS, STATE_DIM),
             jnp.concatenate([tail_a, tail_b], axis=-1)[t // tm_ffn - 1::t // tm_ffn])
    return y.reshape(bsz, t, D_MODEL), state


def _sample_layer(x, w, bias_row, cache_k, cache_v, cache_kidx, page_table, st_re, st_im, st_conv, tf):
    db, tq, _ = x.shape
    assert tq == 1, "the sample group is decoded one token per sequence"
    n_pool = cache_k.shape[0]
    past = page_table.shape[1] * PAGE_SIZE
    x2 = x.reshape(db, D_MODEL)
    k, v, kw, u, sga, sgs, q, qi = _in_proj(x2, w["g_pre_mix"], w["w_in"], 1, db, db, False)
    keys = _sample_scores(page_table, qi.reshape(db, IDX_HEADS, IDX_DIM),
                          kw[:, IDX_DIM:IDX_DIM + IDX_HEADS].reshape(db, IDX_HEADS, 1),
                          kw[:, :IDX_DIM].reshape(db, 1, IDX_DIM), jnp.swapaxes(cache_kidx, 1, 2))
    mask = _sample_select(keys.reshape(db, past + LANES), min(TOPK_MAX, (past + tq) // 4))
    heads = lambda a: a.astype(F32).reshape(db, N_HEADS, HEAD_DIM)
    heads_t = lambda a: jnp.swapaxes(heads(a), 1, 2)
    pages_t = lambda c: jnp.transpose(c, (0, 2, 3, 1))
    y_att_t = _sample_attn(page_table, heads(q), heads_t(q), heads(k), heads_t(v), mask[:, None, :], bias_row,
                           pages_t(cache_k), pages_t(cache_v))
    y_att = jnp.swapaxes(y_att_t, 1, 2).reshape(db, ATT_WIDTH)
    y_ssm, s_re, s_im = _ssm_step(u, st_re.reshape(db, N_STATE), st_im.reshape(db, N_STATE), w["pre"], w["pim"],
                                  w["wb"], w["wc"], w["d_skip"], w["w_glu"], w["b_glu"])
    x1, h2 = _merge(x2, y_att, y_ssm, sga, sgs, w["w_att_out"], w["w_ssm_out"], w["w_o"],
                    w["g_post_mix"], w["g_pre_ffn"], db)
    y, up_a, up_b = _ffn(h2, x1, w["w_up"], w["conv_w"], w["conv_b"], w["w_down"], w["g_post_ffn"],
                         st_conv, db, 1, db, tf, False)
    conv_new = jnp.concatenate([st_conv[:, 1:], jnp.concatenate([up_a, up_b], axis=-1)[:, None, :]], axis=1)
    state = (k.reshape(db, 1, N_HEADS, HEAD_DIM), v.reshape(db, 1, N_HEADS, HEAD_DIM),
             kw[:, :IDX_DIM].reshape(db, 1, IDX_DIM),
             s_re.reshape(db, N_GROUPS, STATE_DIM), s_im.reshape(db, N_GROUPS, STATE_DIM), conv_new)
    return y.reshape(db, 1, D_MODEL), state


def kernel(x_prompt, x_sample, cache_k, cache_v, cache_kidx, state_ssm_re, state_ssm_im, state_conv, page_table,
           rel_bias, w_in, g_pre_mix, g_post_mix, lam_re, lam_im, log_dt, b_re, b_im, c_re, c_im, d_skip, w_glu,
           b_glu, w_att_out, w_ssm_out, w_o, g_pre_ffn, g_post_ffn, w_up, conv_w, conv_b, w_down):
    depth = w_in.shape[0]
    t = x_prompt.shape[1]
    past = page_table.shape[1] * PAGE_SIZE
    tq = min(256, t)
    tm = min(512, t)
    chunk = min(256, t)
    tf = 512
    bias_tiles, bias_row = _bias_tiles(rel_bias, tq, past)
    y_p, y_s = x_prompt, x_sample
    outs_p, outs_s = [], []
    for l in range(depth):
        lw = (w_in[l], g_pre_mix[l], g_post_mix[l], lam_re[l], lam_im[l], log_dt[l], b_re[l], b_im[l],
              c_re[l], c_im[l], d_skip[l], w_glu[l], b_glu[l], w_att_out[l], w_ssm_out[l], w_o[l],
              g_pre_ffn[l], g_post_ffn[l], w_up[l], conv_w[l], conv_b[l], w_down[l])
        w = _layer_weights(lw)
        y_p, st_p = _prompt_layer(y_p, w, bias_tiles, tq, tm, chunk, tf)
        y_s, st_s = _sample_layer(y_s, w, bias_row, cache_k[l], cache_v[l], cache_kidx[l], page_table,
                                  state_ssm_re[l], state_ssm_im[l], state_conv[l], tf)
        outs_p.append(st_p)
        outs_s.append(st_s)
    k_p, v_p, ki_p, sr_p, si_p, cv_p = [jnp.stack(a) for a in zip(*outs_p)]
    k_s, v_s, ki_s, sr_s, si_s, cv_s = [jnp.stack(a) for a in zip(*outs_s)]
    return (y_p, y_s, k_p, v_p, ki_p, sr_p, si_p, cv_p, k_s, v_s, ki_s, sr_s, si_s, cv_s)
```

```python
import functools
import math

import numpy as np
import jax
import jax.numpy as jnp
from jax import lax
from jax.experimental import pallas as pl
from jax.experimental.pallas import tpu as pltpu

F32 = jnp.float32
BF16 = jnp.bfloat16
I32 = jnp.int32

D_MODEL = 1024
PAGE_SIZE = 128
N_HEADS = 8
HEAD_DIM = 64
ATT_WIDTH = N_HEADS * HEAD_DIM
IDX_HEADS = 4
IDX_DIM = 64
TOPK_MAX = 256
N_BUCKETS = 32
MAX_DISTANCE = 128
SSM_WIDTH = 512
GROUP = 16
N_GROUPS = SSM_WIDTH // GROUP
STATE_DIM = 64
N_STATE = N_GROUPS * STATE_DIM
D_FF = 4 * D_MODEL
CONV_W = 3
EPS = 1e-6
SPLITS = (ATT_WIDTH, ATT_WIDTH, ATT_WIDTH, IDX_HEADS * IDX_DIM, IDX_DIM, IDX_HEADS, SSM_WIDTH, D_MODEL, D_MODEL)

LANES = 128
KEY_EXCLUDED = -(2 ** 31)
IDX_BITS = 14
BF16_SUBLANES = 16
V_ROWS = HEAD_DIM + BF16_SUBLANES
LOG2_E = math.log2(math.e)
FFN_PART_ROWS = 256
SCAN_GROUP = 8
MASKED = -1e30
VMEM_LIMIT = 56 * 1024 * 1024

_C_Q, _C_K, _C_V = 0, ATT_WIDTH, 2 * ATT_WIDTH
_C_QI = 3 * ATT_WIDTH
_C_KW = _C_QI + IDX_HEADS * IDX_DIM
_C_U = _C_KW + LANES
_C_GA = _C_U + SSM_WIDTH
_C_GS = _C_GA + D_MODEL
_C_END = _C_GS + D_MODEL


def _params(*sem):
    return pltpu.CompilerParams(dimension_semantics=sem, vmem_limit_bytes=VMEM_LIMIT)


def _rms(x, g):
    inv = lax.rsqrt(jnp.mean(x * x, axis=-1, keepdims=True) + EPS)
    return (x * inv) * g


def _dot(a, b):
    return jnp.dot(a, b, preferred_element_type=F32)


def _dot_nt(a, b):
    return lax.dot_general(a, b, (((1,), (1,)), ((), ())), preferred_element_type=F32)


def _bucket_starts():
    n = np.arange(0, 1 << IDX_BITS, dtype=np.int32)
    max_exact = N_BUCKETS // 2
    nf = np.maximum(n, 1).astype(np.float32)
    large = max_exact + (np.log(nf / np.float32(max_exact)) / np.float32(math.log(MAX_DISTANCE / max_exact))
                         * np.float32(N_BUCKETS - max_exact)).astype(np.int32)
    large = np.minimum(large, N_BUCKETS - 1)
    bucket = np.where(n < max_exact, n, large)
    assert np.all(np.diff(bucket) >= 0)
    starts = [int(np.argmax(bucket >= b)) for b in range(N_BUCKETS)]
    assert all(bucket[s] == b for b, s in enumerate(starts))
    return starts


_BUCKET_START = _bucket_starts()


def _bias_tiles_body(rb_ref, tile_ref, row_ref, *, tq, past):
    h = pl.program_id(0)

    def bias_of(dist):
        val = jnp.full(dist.shape, rb_ref[0, h], F32)
        for b in range(1, N_BUCKETS):
            val = jnp.where(dist >= _BUCKET_START[b], rb_ref[b, h], val)
        return val

    key_off = lax.broadcasted_iota(I32, (tq, tq), 0)
    qry_off = lax.broadcasted_iota(I32, (tq, tq), 1)
    far = rb_ref[N_BUCKETS - 1, h]
    tile_ref[0, 1] = (bias_of(qry_off - key_off) - far) * LOG2_E
    tile_ref[0, 0] = (bias_of(qry_off - key_off + tq) - far) * LOG2_E
    lane = lax.broadcasted_iota(I32, (1, past + LANES), 1)
    row_ref[0] = bias_of(past - lane)


def _bias_tiles(rel_bias, tq, past):
    assert tq + 1 >= _BUCKET_START[N_BUCKETS - 1]
    return pl.pallas_call(
        functools.partial(_bias_tiles_body, tq=tq, past=past),
        grid=(N_HEADS,),
        in_specs=[pl.BlockSpec(memory_space=pltpu.SMEM)],
        out_specs=[pl.BlockSpec((1, 2, tq, tq), lambda h: (h, 0, 0, 0)),
                   pl.BlockSpec((1, 1, past + LANES), lambda h: (h, 0, 0))],
        out_shape=[jax.ShapeDtypeStruct((N_HEADS, 2, tq, tq), F32),
                   jax.ShapeDtypeStruct((N_HEADS, 1, past + LANES), F32)],
        compiler_params=_params("arbitrary"),
        name="bias_tiles",
    )(rel_bias)


def _in_proj_body(x_ref, g_ref, w_ref, k_ref, v_ref, kw_ref, u_ref, sga_ref, sgs_ref, *more_refs, seq):
    h = _rms(x_ref[...], g_ref[...]).astype(BF16)

    def proj(c0, c1):
        return _dot(h, w_ref[:, c0:c1])

    q = proj(_C_Q, _C_K) * HEAD_DIM ** -0.5
    k = proj(_C_K, _C_V)
    v = proj(_C_V, _C_QI)
    qi = proj(_C_QI, _C_KW)
    kw = proj(_C_KW, _C_U)
    lane = lax.broadcasted_iota(I32, kw.shape, 1)
    kw = jnp.where(lane >= IDX_DIM, kw * IDX_HEADS ** -0.5, kw)
    kw_ref[...] = kw
    u_ref[...] = proj(_C_U, _C_GA)
    sga_ref[...] = jax.nn.sigmoid(proj(_C_GA, _C_GS))
    sgs_ref[...] = jax.nn.sigmoid(proj(_C_GS, _C_END))
    if seq:
        qt_ref, qit_ref, kwt_ref, kh_ref, vt_ref, ki_ref = more_refs
        tm = k.shape[0]
        vt = v.T.reshape(N_HEADS, HEAD_DIM, tm)
        k_ref[0] = k.T.reshape(N_HEADS, HEAD_DIM, tm)
        v_ref[0] = vt
        qt_ref[...] = (q * LOG2_E).T.reshape(N_HEADS, HEAD_DIM, tm).astype(BF16)
        qit_ref[...] = (qi * IDX_DIM ** -0.5).T.reshape(IDX_HEADS, IDX_DIM, tm).astype(BF16)
        kwt_ref[0] = kw.T
        for hd in range(N_HEADS):
            kh_ref[0, hd] = k[:, hd * HEAD_DIM:(hd + 1) * HEAD_DIM].astype(BF16)
        vt_ref[0, :, :HEAD_DIM, :] = vt.astype(BF16)
        vt_ref[0, :, HEAD_DIM:, :] = jnp.ones((N_HEADS, V_ROWS - HEAD_DIM, tm), BF16)
        ki_ref[0] = kw[:, :IDX_DIM].astype(BF16)
    else:
        q_ref, qi_ref = more_refs
        k_ref[...] = k
        v_ref[...] = v
        q_ref[...] = q.astype(BF16)
        qi_ref[...] = qi


def _in_proj(x2, g, w_packed, bsz, t, tm, seq):
    m = bsz * t
    nt = t // tm
    row = lambda b, i: (b * nt + i, 0)
    widths = (LANES, SSM_WIDTH, D_MODEL, D_MODEL)
    if seq:
        kv_spec = pl.BlockSpec((1, N_HEADS, HEAD_DIM, tm), lambda b, i: (b, 0, 0, i))
        kv_shape = jax.ShapeDtypeStruct((bsz, N_HEADS, HEAD_DIM, t), F32)
    else:
        kv_spec = pl.BlockSpec((tm, ATT_WIDTH), row)
        kv_shape = jax.ShapeDtypeStruct((m, ATT_WIDTH), F32)
    out_specs = [kv_spec, kv_spec] + [pl.BlockSpec((tm, w), row) for w in widths]
    out_shape = [kv_shape, kv_shape] + [jax.ShapeDtypeStruct((m, w), F32) for w in widths]
    if seq:
        col = lambda b, i: (0, 0, b * nt + i)
        out_specs += [pl.BlockSpec((N_HEADS, HEAD_DIM, tm), col),
                      pl.BlockSpec((IDX_HEADS, IDX_DIM, tm), col),
                      pl.BlockSpec((1, LANES, tm), lambda b, i: (b, 0, i)),
                      pl.BlockSpec((1, N_HEADS, tm, HEAD_DIM), lambda b, i: (b, 0, i, 0)),
                      pl.BlockSpec((1, N_HEADS, V_ROWS, tm), lambda b, i: (b, 0, 0, i)),
                      pl.BlockSpec((1, tm, IDX_DIM), lambda b, i: (b, i, 0))]
        out_shape += [jax.ShapeDtypeStruct((N_HEADS, HEAD_DIM, m), BF16),
                      jax.ShapeDtypeStruct((IDX_HEADS, IDX_DIM, m), BF16),
                      jax.ShapeDtypeStruct((bsz, LANES, t), F32),
                      jax.ShapeDtypeStruct((bsz, N_HEADS, t, HEAD_DIM), BF16),
                      jax.ShapeDtypeStruct((bsz, N_HEADS, V_ROWS, t), BF16),
                      jax.ShapeDtypeStruct((bsz, t, IDX_DIM), BF16)]
    else:
        out_specs += [pl.BlockSpec((tm, ATT_WIDTH), row), pl.BlockSpec((tm, IDX_HEADS * IDX_DIM), row)]
        out_shape += [jax.ShapeDtypeStruct((m, ATT_WIDTH), BF16),
                      jax.ShapeDtypeStruct((m, IDX_HEADS * IDX_DIM), F32)]
    return pl.pallas_call(
        functools.partial(_in_proj_body, seq=seq),
        grid=(bsz, nt),
        in_specs=[pl.BlockSpec((tm, D_MODEL), row),
                  pl.BlockSpec((1, D_MODEL), lambda b, i: (0, 0)),
                  pl.BlockSpec((D_MODEL, _C_END), lambda b, i: (0, 0))],
        out_specs=out_specs,
        out_shape=out_shape,
        compiler_params=_params("arbitrary", "arbitrary"),
        name="in_proj",
    )(x2, g, w_packed)


def _fori_by_two(lo, hi, body, init):
    n_pairs = (hi - lo) // 2
    carry = lax.fori_loop(0, n_pairs, lambda t, c: body(lo + 2 * t + 1, body(lo + 2 * t, c)), init)
    return lax.fori_loop(lo + 2 * n_pairs, hi, body, carry)


def _sortable_key(score):
    bits = pltpu.bitcast(score, I32)
    return jnp.where(bits < 0, bits ^ 0x7FFFFFFF, bits)


def _topk_mask(keys_ref, nblk, height, topk):
    n_lanes = keys_ref.shape[1]
    n_parts = height // 8
    assert n_parts * 8 == height and n_parts & (n_parts - 1) == 0

    def row_block(j):
        return pl.ds(pl.multiple_of(j * height, height), height)

    def count(pred, bound):
        def block_hits(j):
            hit = jnp.where(pred(keys_ref[row_block(j), :], bound), 1, 0)
            parts = [hit[r:r + 8] for r in range(0, height, 8)]
            while len(parts) > 1:
                parts = [a + b for a, b in zip(parts[::2], parts[1::2])]
            return parts[0]

        acc = _fori_by_two(0, nblk, lambda j, acc: acc + block_hits(j), jnp.zeros((8, n_lanes), I32))
        return jnp.sum(acc, axis=0, keepdims=True)

    zero = jnp.zeros((1, n_lanes), I32)

    prefix = jnp.where(count(lambda k, b: k >= b, zero) >= topk, 0, KEY_EXCLUDED).astype(I32)

    def value_bit(b, prefix):
        cand = prefix | jnp.left_shift(1, 30 - b)
        return jnp.where(count(lambda k, c: k >= c, cand) >= topk, cand, prefix)
    thr = lax.fori_loop(0, 31, value_bit, prefix)

    n_above = count(lambda k, t: k > t, thr)
    need = jnp.where(thr == KEY_EXCLUDED, 0, topk - n_above).astype(F32)
    at_or_before = (lax.broadcasted_iota(I32, (height, height), 0)
                    >= lax.broadcasted_iota(I32, (height, height), 1))
    lower_ones = jnp.where(at_or_before, 1.0, 0.0).astype(BF16)

    def emit(j, seen):
        k = keys_ref[row_block(j), :]
        tie = k == thr
        rank = seen + _dot(lower_ones, jnp.where(tie, 1.0, 0.0).astype(BF16))
        mask = jnp.where(k > thr, 0.0, jnp.where(tie, jnp.where(rank <= need, 0.0, MASKED), MASKED))
        keys_ref[row_block(j), :] = pltpu.bitcast(mask.astype(F32), I32)
        return rank[height - 1:, :]
    lax.fori_loop(0, nblk, emit, jnp.zeros((1, n_lanes), F32))


def _prompt_attn_body(qt_ref, qit_ref, wt_ref, kh_ref, vt_ref, ki_ref, bias_ref, o_ref, keys_ref, m_ref, acc_ref,
                      alpha_ref, p_ref, *, tq, topk):
    i = pl.program_id(1)
    nblk = i + 1
    key_off = lax.broadcasted_iota(I32, (tq, tq), 0)
    qry_off = lax.broadcasted_iota(I32, (tq, tq), 1)

    def key_block(j):
        return pl.ds(pl.multiple_of(j * tq, tq), tq)

    qit = jnp.concatenate([qit_ref[h] for h in range(IDX_HEADS)], axis=1)
    wt = [wt_ref[0, h:h + 1, :] for h in range(IDX_HEADS)]

    def score_block(j, _, diagonal):
        ki = ki_ref[0, key_block(j), :]
        dots = _dot(ki, qit)
        s = None
        for h in range(IDX_HEADS):
            sh = jnp.maximum(dots[:, h * tq:(h + 1) * tq], 0.0) * wt[h]
            s = sh if s is None else s + sh
        key = _sortable_key(s)
        if diagonal:
            key = jnp.where(key_off <= qry_off, key, KEY_EXCLUDED)
        keys_ref[key_block(j), :] = key
        return 0
    _fori_by_two(0, i, functools.partial(score_block, diagonal=False), 0)
    score_block(i, 0, diagonal=True)

    _topk_mask(keys_ref, nblk, tq, topk)

    m_ref[...] = jnp.full(m_ref.shape, MASKED, F32)
    acc_ref[...] = jnp.zeros(acc_ref.shape, F32)
    p_ref[...] = jnp.zeros(p_ref.shape, BF16)
    alpha_ref[...] = jnp.ones(alpha_ref.shape, F32)

    def accumulate(j):
        for h in range(N_HEADS):
            acc_ref[h] = alpha_ref[h] * acc_ref[h] + _dot(vt_ref[0, h, :, key_block(j)], p_ref[h])

    def attend(j, _, near):
        accumulate(jnp.maximum(j - 1, 0))
        masked = pltpu.bitcast(keys_ref[key_block(j), :], F32)
        for h in range(N_HEADS):
            s = _dot(kh_ref[0, h, key_block(j), :], qt_ref[h]) + masked
            if near:
                s = s + bias_ref[h, j - i + 1]
            m_old = m_ref[h]
            m_new = jnp.maximum(m_old, jnp.max(s, axis=0, keepdims=True))
            p_ref[h] = jnp.exp2(s - m_new).astype(BF16)
            alpha_ref[h] = jnp.exp2(m_old - m_new)
            m_ref[h] = m_new
        return 0

    n_far = jnp.maximum(i - 1, 0)
    _fori_by_two(0, n_far, functools.partial(attend, near=False), 0)
    lax.fori_loop(n_far, nblk, functools.partial(attend, near=True), 0)
    accumulate(i)
    out_t = jnp.concatenate(
        [acc_ref[h, :HEAD_DIM, :] / acc_ref[h, HEAD_DIM:HEAD_DIM + 1, :] for h in range(N_HEADS)], axis=0)
    o_ref[...] = out_t.T


def _prompt_attn(qt, qit, kwt, kh, vt, ki, bias_tiles, bsz, t, tq, topk):
    nq = t // tq
    col = lambda b, i: (0, 0, b * nq + i)
    whole = dict(pipeline_mode=pl.Buffered(1))
    return pl.pallas_call(
        functools.partial(_prompt_attn_body, tq=tq, topk=topk),
        grid=(bsz, nq),
        in_specs=[pl.BlockSpec((N_HEADS, HEAD_DIM, tq), col),
                  pl.BlockSpec((IDX_HEADS, IDX_DIM, tq), col),
                  pl.BlockSpec((1, 8, tq), lambda b, i: (b, IDX_DIM // 8, i)),
                  pl.BlockSpec((1, N_HEADS, t, HEAD_DIM), lambda b, i: (b, 0, 0, 0), **whole),
                  pl.BlockSpec((1, N_HEADS, V_ROWS, t), lambda b, i: (b, 0, 0, 0), **whole),
                  pl.BlockSpec((1, t, IDX_DIM), lambda b, i: (b, 0, 0), **whole),
                  pl.BlockSpec((N_HEADS, 2, tq, tq), lambda b, i: (0, 0, 0, 0), **whole)],
        out_specs=pl.BlockSpec((tq, ATT_WIDTH), lambda b, i: (b * nq + i, 0)),
        out_shape=jax.ShapeDtypeStruct((bsz * t, ATT_WIDTH), F32),
        scratch_shapes=[pltpu.VMEM((t, tq), I32),
                        pltpu.VMEM((N_HEADS, 1, tq), F32), pltpu.VMEM((N_HEADS, V_ROWS, tq), F32),
                        pltpu.VMEM((N_HEADS, 1, tq), F32), pltpu.VMEM((N_HEADS, tq, tq), BF16)],
        compiler_params=_params("arbitrary", "arbitrary"),
        name="prompt_attn",
    )(qt, qit, kwt, kh, vt, ki, bias_tiles)


def _sample_scores_body(pt_ref, qi_ref, w_ref, kn_ref, *rest, n_pages):
    page_refs, key_ref = rest[:n_pages], rest[n_pages]
    qi = qi_ref[...].astype(BF16)
    w = w_ref[...]

    def weighted(s):
        return jnp.sum(jnp.maximum(s * IDX_DIM ** -0.5, 0.0) * w, axis=0, keepdims=True)

    for p in range(n_pages):
        s = _dot(qi, page_refs[p][...].astype(BF16))
        key_ref[:, p * PAGE_SIZE:(p + 1) * PAGE_SIZE] = _sortable_key(weighted(s))
    s_self = jnp.sum(qi.astype(F32) * kn_ref[...].astype(BF16).astype(F32), axis=1, keepdims=True)
    lane = lax.broadcasted_iota(I32, (1, LANES), 1)
    key_ref[:, n_pages * PAGE_SIZE:] = jnp.where(lane == 0, _sortable_key(weighted(s_self)), KEY_EXCLUDED)


def _sample_scores(page_table, qi3, w3, kn3, cache_kidx):
    db, n_pages = page_table.shape
    n_cols = n_pages * PAGE_SIZE + LANES
    per = lambda s, pt: (s, 0, 0)
    page_specs = [pl.BlockSpec((None, IDX_DIM, PAGE_SIZE), functools.partial(lambda s, pt, p: (pt[s, p], 0, 0), p=p))
                  for p in range(n_pages)]
    return pl.pallas_call(
        functools.partial(_sample_scores_body, n_pages=n_pages),
        grid_spec=pltpu.PrefetchScalarGridSpec(
            num_scalar_prefetch=1, grid=(db,),
            in_specs=[pl.BlockSpec((None, IDX_HEADS, IDX_DIM), per),
                      pl.BlockSpec((None, IDX_HEADS, 1), per),
                      pl.BlockSpec((None, 1, IDX_DIM), per)] + page_specs,
            out_specs=pl.BlockSpec((None, 1, n_cols), per)),
        out_shape=jax.ShapeDtypeStruct((db, 1, n_cols), I32),
        compiler_params=_params("arbitrary"),
        name="sample_scores",
    )(page_table, qi3, w3, kn3, *([cache_kidx] * n_pages))


def _sample_select_body(key_ref, mask_ref, keys_scr, *, n_cols, topk):
    keys_scr[...] = key_ref[...].T
    _topk_mask(keys_scr, n_cols // LANES, LANES, topk)
    mask_ref[...] = pltpu.bitcast(keys_scr[...], F32).T


def _sample_select(keys, topk):
    n_rows, n_cols = keys.shape
    return pl.pallas_call(
        functools.partial(_sample_select_body, n_cols=n_cols, topk=topk),
        out_shape=jax.ShapeDtypeStruct((n_rows, n_cols), F32),
        scratch_shapes=[pltpu.VMEM((n_cols, n_rows), I32)],
        compiler_params=pltpu.CompilerParams(vmem_limit_bytes=VMEM_LIMIT),
        name="sample_select",
    )(keys)


def _sample_attn_body(pt_ref, q_ref, qt_ref, kn_ref, vnt_ref, mask_ref, bias_ref, *rest, n_pages):
    kt_refs, vt_refs, o_ref = rest[:n_pages], rest[n_pages:2 * n_pages], rest[2 * n_pages]
    past = n_pages * PAGE_SIZE
    mask = mask_ref[...]
    s_self_all = jnp.sum(q_ref[...] * kn_ref[...], axis=1, keepdims=True)
    for h in range(N_HEADS):
        q_col = qt_ref[:, h:h + 1]
        bias = bias_ref[h]
        s = jnp.concatenate([jnp.sum(kt_refs[p][h] * q_col, axis=0, keepdims=True) for p in range(n_pages)], axis=1)
        s = s + bias[:, :past] + mask[:, :past]
        s_self = s_self_all[h:h + 1, :] + bias[:, past:past + 1] + mask[:, past:past + 1]
        m = jnp.maximum(jnp.max(s, axis=1, keepdims=True), s_self)
        p_past = jnp.exp(s - m)
        p_self = jnp.exp(s_self - m)
        l = jnp.sum(p_past, axis=1, keepdims=True) + p_self
        acc = vt_refs[0][h] * p_past[:, :PAGE_SIZE]
        for p in range(1, n_pages):
            acc = acc + vt_refs[p][h] * p_past[:, p * PAGE_SIZE:(p + 1) * PAGE_SIZE]
        out = jnp.sum(acc, axis=1, keepdims=True) + p_self * vnt_ref[:, h:h + 1]
        o_ref[:, h:h + 1] = out / l


def _sample_attn(page_table, q3, qt3, kn3, vnt3, mask3, bias_row, cache_kt, cache_vt):
    db, n_pages = page_table.shape
    n_cols = n_pages * PAGE_SIZE + LANES
    per = lambda s, pt: (s, 0, 0)
    page_specs = [pl.BlockSpec((None, N_HEADS, HEAD_DIM, PAGE_SIZE),
                               functools.partial(lambda s, pt, p: (pt[s, p], 0, 0, 0), p=p)) for p in range(n_pages)]
    return pl.pallas_call(
        functools.partial(_sample_attn_body, n_pages=n_pages),
        grid_spec=pltpu.PrefetchScalarGridSpec(
            num_scalar_prefetch=1, grid=(db,),
            in_specs=[pl.BlockSpec((None, N_HEADS, HEAD_DIM), per),
                      pl.BlockSpec((None, HEAD_DIM, N_HEADS), per),
                      pl.BlockSpec((None, N_HEADS, HEAD_DIM), per),
                      pl.BlockSpec((None, HEAD_DIM, N_HEADS), per),
                      pl.BlockSpec((None, 1, n_cols), per),
                      pl.BlockSpec((N_HEADS, 1, n_cols), lambda s, pt: (0, 0, 0))] + page_specs + page_specs,
            out_specs=pl.BlockSpec((None, HEAD_DIM, N_HEADS), per)),
        out_shape=jax.ShapeDtypeStruct((db, HEAD_DIM, N_HEADS), F32),
        compiler_params=_params("arbitrary"),
        name="sample_attn",
    )(page_table, q3, qt3, kn3, vnt3, mask3, bias_row, *([cache_kt] * n_pages), *([cache_vt] * n_pages))


def _cmul(ar, ai, br, bi):
    return ar * br - ai * bi, ar * bi + ai * br


def _ssm_prep_body(lre_f, lim_f, ldt_f, lre_r, lim_r, ldt_r, bre_ref, bim_ref, pre_ref, pim_ref, bbr_ref, bbi_ref,
                   *, chunk):
    def lam_bar(lre, lim, ldt):
        dt = jnp.exp(ldt)
        mag = jnp.exp(lre * dt)
        return mag * jnp.cos(lim * dt), mag * jnp.sin(lim * dt)

    lbr, lbi = lam_bar(lre_f[...], lim_f[...], ldt_f[...])
    pr = jnp.broadcast_to(lbr, (chunk, N_STATE))
    pi = jnp.broadcast_to(lbi, (chunk, N_STATE))
    row = lax.broadcasted_iota(I32, (chunk, N_STATE), 0)
    d = 1
    while d < chunk:
        sr = jnp.where(row >= d, pltpu.roll(pr, d, 0), 1.0)
        si = jnp.where(row >= d, pltpu.roll(pi, d, 0), 0.0)
        pr, pi = _cmul(pr, pi, sr, si)
        d *= 2
    pre_ref[...] = pr
    pim_ref[...] = pi

    lre, lim = lre_r[...], lim_r[...]
    lbr, lbi = lam_bar(lre, lim, ldt_r[...])
    den = lre * lre + lim * lim
    nr, ni = lbr - 1.0, lbi
    cr = (nr * lre + ni * lim) / den
    ci = (ni * lre - nr * lim) / den
    bbr, bbi = _cmul(cr, ci, bre_ref[...], bim_ref[...])
    bbr_ref[...] = bbr
    bbi_ref[...] = bbi


def _ssm_prep(lam_re, lam_im, log_dt, b_re, b_im, chunk):
    flat = lambda a: a.reshape(1, N_STATE)
    rep = lambda a: jnp.repeat(a, GROUP, axis=0)
    ldt = jnp.broadcast_to(log_dt[:, None], (N_GROUPS, STATE_DIM))
    bt = lambda b: jnp.swapaxes(b, 1, 2).reshape(SSM_WIDTH, STATE_DIM)
    return pl.pallas_call(
        functools.partial(_ssm_prep_body, chunk=chunk),
        out_shape=[jax.ShapeDtypeStruct((chunk, N_STATE), F32), jax.ShapeDtypeStruct((chunk, N_STATE), F32),
                   jax.ShapeDtypeStruct((SSM_WIDTH, STATE_DIM), F32), jax.ShapeDtypeStruct((SSM_WIDTH, STATE_DIM), F32)],
        compiler_params=pltpu.CompilerParams(vmem_limit_bytes=VMEM_LIMIT),
        name="ssm_prep",
    )(flat(lam_re), flat(lam_im), flat(ldt), rep(lam_re), rep(lam_im), rep(ldt), bt(b_re), bt(b_im))


def _block_diag(blocks):
    g, r, c = blocks.shape
    eye = jnp.eye(g, dtype=blocks.dtype)
    return (blocks[:, :, None, :] * eye[:, None, :, None]).reshape(g * r, g * c)


def _ssm_readout(u, xr, xi, wc_ref, dskip_ref, wglu_ref, bglu_ref):
    y = _dot(jnp.concatenate([xr, xi], axis=1).astype(BF16), wc_ref[...]) + dskip_ref[...] * u
    g = jax.nn.gelu(y)
    return g * jax.nn.sigmoid(_dot(g.astype(BF16), wglu_ref[...]) + bglu_ref[...])


def _ssm_scan_body(u_ref, x0r_ref, x0i_ref, pre_ref, pim_ref, wb_ref, wc_ref, dskip_ref, wglu_ref, bglu_ref,
                   y_ref, sr_ref, si_ref, cr_ref, ci_ref, xr_ref, xi_ref, *, chunk):
    c = pl.program_id(1)

    @pl.when(c == 0)
    def _():
        cr_ref[...] = x0r_ref[0]
        ci_ref[...] = x0i_ref[0]

    u = u_ref[...]
    bu = _dot(u.astype(BF16), wb_ref[...])
    n_groups = chunk // SCAN_GROUP
    xr = bu[:, :N_STATE].reshape(n_groups, SCAN_GROUP, N_STATE)
    xi = bu[:, N_STATE:].reshape(n_groups, SCAN_GROUP, N_STATE)
    pos = lax.broadcasted_iota(I32, (SCAN_GROUP, N_STATE), 0)
    d = 1
    while d < SCAN_GROUP:
        ar = jnp.where(pos >= d, pre_ref[d - 1:d, :], 0.0)
        ai = jnp.where(pos >= d, pim_ref[d - 1:d, :], 0.0)
        tr, ti = _cmul(ar, ai, pltpu.roll(xr, d, 1), pltpu.roll(xi, d, 1))
        xr, xi = xr + tr, xi + ti
        d *= 2
    pr, pi = pre_ref[...], pim_ref[...]
    cr, ci = cr_ref[...], ci_ref[...]
    for g in range(n_groups):
        tr, ti = _cmul(pr, pi, cr, ci)
        gr, gi = xr[g] + tr, xi[g] + ti
        xr_ref[g * SCAN_GROUP:(g + 1) * SCAN_GROUP, :] = gr
        xi_ref[g * SCAN_GROUP:(g + 1) * SCAN_GROUP, :] = gi
        cr, ci = gr[SCAN_GROUP - 1:, :], gi[SCAN_GROUP - 1:, :]
    cr_ref[...] = cr
    ci_ref[...] = ci
    sr_ref[0] = cr
    si_ref[0] = ci
    y_ref[...] = _ssm_readout(u, xr_ref[...], xi_ref[...], wc_ref, dskip_ref, wglu_ref, bglu_ref)


def _ssm_scan(u, x0r, x0i, pre, pim, wb, wc, dskip, wglu, bglu, bsz, t, chunk):
    nc = t // chunk
    row = lambda b, c: (b * nc + c, 0)
    const = lambda b, c: (0, 0)
    state = pl.BlockSpec((1, 1, N_STATE), lambda b, c: (b, 0, 0))
    return pl.pallas_call(
        functools.partial(_ssm_scan_body, chunk=chunk),
        grid=(bsz, nc),
        in_specs=[pl.BlockSpec((chunk, SSM_WIDTH), row), state, state,
                  pl.BlockSpec((SCAN_GROUP, N_STATE), const), pl.BlockSpec((SCAN_GROUP, N_STATE), const),
                  pl.BlockSpec((SSM_WIDTH, 2 * N_STATE), const), pl.BlockSpec((2 * N_STATE, SSM_WIDTH), const),
                  pl.BlockSpec((1, SSM_WIDTH), const), pl.BlockSpec((SSM_WIDTH, SSM_WIDTH), const),
                  pl.BlockSpec((1, SSM_WIDTH), const)],
        out_specs=[pl.BlockSpec((chunk, SSM_WIDTH), row), state, state],
        out_shape=[jax.ShapeDtypeStruct((bsz * t, SSM_WIDTH), F32),
                   jax.ShapeDtypeStruct((bsz, 1, N_STATE), F32), jax.ShapeDtypeStruct((bsz, 1, N_STATE), F32)],
        scratch_shapes=[pltpu.VMEM((1, N_STATE), F32), pltpu.VMEM((1, N_STATE), F32),
                        pltpu.VMEM((chunk, N_STATE), F32), pltpu.VMEM((chunk, N_STATE), F32)],
        compiler_params=_params("arbitrary", "arbitrary"),
        name="ssm_scan",
    )(u, x0r, x0i, pre, pim, wb, wc, dskip, wglu, bglu)


def _ssm_step_body(u_ref, x0r_ref, x0i_ref, pre_ref, pim_ref, wb_ref, wc_ref, dskip_ref, wglu_ref, bglu_ref,
                   y_ref, sr_ref, si_ref):
    u = u_ref[...]
    bu = _dot(u.astype(BF16), wb_ref[...])
    tr, ti = _cmul(pre_ref[0:1, :], pim_ref[0:1, :], x0r_ref[...], x0i_ref[...])
    xr, xi = bu[:, :N_STATE] + tr, bu[:, N_STATE:] + ti
    sr_ref[...] = xr
    si_ref[...] = xi
    y_ref[...] = _ssm_readout(u, xr, xi, wc_ref, dskip_ref, wglu_ref, bglu_ref)


def _ssm_step(u, x0r, x0i, pre, pim, wb, wc, dskip, wglu, bglu):
    n = u.shape[0]
    return pl.pallas_call(
        _ssm_step_body,
        out_shape=[jax.ShapeDtypeStruct((n, SSM_WIDTH), F32),
                   jax.ShapeDtypeStruct((n, N_STATE), F32), jax.ShapeDtypeStruct((n, N_STATE), F32)],
        compiler_params=pltpu.CompilerParams(vmem_limit_bytes=VMEM_LIMIT),
        name="ssm_step",
    )(u, x0r, x0i, pre, pim, wb, wc, dskip, wglu, bglu)


def _merge_body(x_ref, ya_ref, ys_ref, sga_ref, sgs_ref, wa_ref, ws_ref, wo_ref, gpost_ref, gpre_ref,
                x1_ref, h2_ref):
    tm = x_ref.shape[0]
    half = tm // 2
    halves = [slice(0, half), slice(half, tm)]
    branch = [(_dot(ya_ref[rows, :].astype(BF16), wa_ref[...]), _dot(ys_ref[rows, :].astype(BF16), ws_ref[...]))
              for rows in halves]
    mixed = [_dot((sga_ref[rows, :] * att + sgs_ref[rows, :] * ssm).astype(BF16), wo_ref[...])
             for rows, (att, ssm) in zip(halves, branch)]
    for rows, o in zip(halves, mixed):
        x1 = x_ref[rows, :] + _rms(o, gpost_ref[...])
        x1_ref[rows, :] = x1
        h2_ref[rows, :] = _rms(x1, gpre_ref[...]).astype(BF16)


def _merge(x2, ya, ys, sga, sgs, wa, ws, wo, gpost, gpre, tm):
    m = x2.shape[0]
    row = lambda i: (i, 0)
    const = lambda i: (0, 0)
    return pl.pallas_call(
        _merge_body,
        grid=(m // tm,),
        in_specs=[pl.BlockSpec((tm, D_MODEL), row), pl.BlockSpec((tm, ATT_WIDTH), row),
                  pl.BlockSpec((tm, SSM_WIDTH), row), pl.BlockSpec((tm, D_MODEL), row),
                  pl.BlockSpec((tm, D_MODEL), row),
                  pl.BlockSpec((ATT_WIDTH, D_MODEL), const), pl.BlockSpec((SSM_WIDTH, D_MODEL), const),
                  pl.BlockSpec((D_MODEL, D_MODEL), const), pl.BlockSpec((1, D_MODEL), const),
                  pl.BlockSpec((1, D_MODEL), const)],
        out_specs=[pl.BlockSpec((tm, D_MODEL), row), pl.BlockSpec((tm, D_MODEL), row)],
        out_shape=[jax.ShapeDtypeStruct((m, D_MODEL), F32), jax.ShapeDtypeStruct((m, D_MODEL), BF16)],
        compiler_params=_params("arbitrary"),
        name="merge",
    )(x2, ya, ys, sga, sgs, wa, ws, wo, gpost, gpre)


def _ffn_body(h_ref, x_ref, wua_ref, wub_ref, cwa_ref, cwb_ref, cba_ref, cbb_ref, wd_ref, g_ref, pa_ref, pb_ref,
              y_ref, oa_ref, ob_ref, f_ref, *carry, seq, tiles_per_seq):
    i, j = pl.program_id(0), pl.program_id(1)
    nj = pl.num_programs(1)
    tm = h_ref.shape[0]
    half = min(FFN_PART_ROWS, tm // 2)
    halves = [slice(r, r + half) for r in range(0, tm, half)]
    ca, cb = carry if seq else (None, None)

    @pl.when(j == 0)
    def _():
        f_ref[...] = jnp.zeros(f_ref.shape, F32)

    if seq:
        @pl.when(i % tiles_per_seq == 0)
        def _():
            ca[j, 0:CONV_W - 1, :] = pa_ref[0]
            cb[j, 0:CONV_W - 1, :] = pb_ref[0]

    ups = [(_dot(h_ref[rows, :], wua_ref[...]), _dot(h_ref[rows, :], wub_ref[...])) for rows in halves]

    def conv(up, before, rows, cw_ref, cb_ref, prev_ref):
        if seq:
            row = lax.broadcasted_iota(I32, up.shape, 0)
            m1 = jnp.where(row == 0, before[1:2], pltpu.roll(up, 1, 0))
            m2 = jnp.where(row == 0, before[0:1], jnp.where(row == 1, before[1:2], pltpu.roll(up, 2, 0)))
        else:
            m2, m1 = prev_ref[rows, 0, :], prev_ref[rows, 1, :]
        return cb_ref[...] + m2 * cw_ref[0:1, :] + m1 * cw_ref[1:2, :] + up * cw_ref[2:3, :]

    before_a = ca[j, 0:CONV_W - 1, :] if seq else None
    before_b = cb[j, 0:CONV_W - 1, :] if seq else None
    for rows, (up_a, up_b) in zip(halves, ups):
        a = conv(up_a, before_a, rows, cwa_ref, cba_ref, pa_ref)
        b = conv(up_b, before_b, rows, cwb_ref, cbb_ref, pb_ref)
        f_ref[rows, :] += _dot((jax.nn.gelu(a) * b).astype(BF16), wd_ref[...])
        if seq:
            before_a, before_b = up_a[half - (CONV_W - 1):, :], up_b[half - (CONV_W - 1):, :]
        else:
            oa_ref[rows, :] = up_a
            ob_ref[rows, :] = up_b
    if seq:
        ca[j, 0:CONV_W - 1, :] = before_a
        cb[j, 0:CONV_W - 1, :] = before_b
        oa_ref[0] = before_a
        ob_ref[0] = before_b

    @pl.when(j == nj - 1)
    def _():
        y_ref[...] = x_ref[...] + _rms(f_ref[...], g_ref[...])


def _ffn(h2, x1, w_up, conv_w, conv_b, w_down, g_post, conv_prev, bsz, t, tm, tf, seq):
    m = bsz * t
    nj = D_FF // tf
    row = lambda i, j: (i, 0)
    if seq:
        tiles_per_seq = t // tm
        prev_a = pl.BlockSpec((1, CONV_W - 1, tf), lambda i, j: (i // tiles_per_seq, 0, j))
        prev_b = pl.BlockSpec((1, CONV_W - 1, tf), lambda i, j: (i // tiles_per_seq, 0, j + nj))
        out_tail = pl.BlockSpec((1, CONV_W - 1, tf), lambda i, j: (i, 0, j))
        tail_shape = jax.ShapeDtypeStruct((m // tm, CONV_W - 1, D_FF), F32)
        scratch = [pltpu.VMEM((nj, 8, tf), F32), pltpu.VMEM((nj, 8, tf), F32)]
    else:
        tiles_per_seq = 1
        prev_a = pl.BlockSpec((tm, CONV_W - 1, tf), lambda i, j: (i, 0, j))
        prev_b = pl.BlockSpec((tm, CONV_W - 1, tf), lambda i, j: (i, 0, j + nj))
        out_tail = pl.BlockSpec((tm, tf), lambda i, j: (i, j))
        tail_shape = jax.ShapeDtypeStruct((m, D_FF), F32)
        scratch = []
    return pl.pallas_call(
        functools.partial(_ffn_body, seq=seq, tiles_per_seq=tiles_per_seq),
        grid=(m // tm, nj),
        in_specs=[pl.BlockSpec((tm, D_MODEL), row), pl.BlockSpec((tm, D_MODEL), row),
                  pl.BlockSpec((D_MODEL, tf), lambda i, j: (0, j)),
                  pl.BlockSpec((D_MODEL, tf), lambda i, j: (0, j + nj)),
                  pl.BlockSpec((CONV_W, tf), lambda i, j: (0, j)),
                  pl.BlockSpec((CONV_W, tf), lambda i, j: (0, j + nj)),
                  pl.BlockSpec((1, tf), lambda i, j: (0, j)),
                  pl.BlockSpec((1, tf), lambda i, j: (0, j + nj)),
                  pl.BlockSpec((tf, D_MODEL), lambda i, j: (j, 0)),
                  pl.BlockSpec((1, D_MODEL), lambda i, j: (0, 0)),
                  prev_a, prev_b],
        out_specs=[pl.BlockSpec((tm, D_MODEL), row), out_tail, out_tail],
        out_shape=[jax.ShapeDtypeStruct((m, D_MODEL), F32), tail_shape, tail_shape],
        scratch_shapes=[pltpu.VMEM((tm, D_MODEL), F32)] + scratch,
        compiler_params=_params("arbitrary", "arbitrary"),
        name="conv_ffn",
    )(h2, x1, w_up, w_up, conv_w, conv_w, conv_b, conv_b, w_down, g_post, conv_prev, conv_prev)


def _pack_w_in(w_in):
    points = np.cumsum(SPLITS)[:-1].tolist()
    wq, wk, wv, wqi, wki, wwi, wu, wga, wgs = jnp.split(w_in, points, axis=-1)
    pad = jnp.zeros((D_MODEL, LANES - IDX_DIM - IDX_HEADS), w_in.dtype)
    return jnp.concatenate([wq, wk, wv, wqi, wki, wwi, pad, wu, wga, wgs], axis=-1).astype(BF16)


def _layer_weights(lw):
    (w_in, g_pre_mix, g_post_mix, lam_re, lam_im, log_dt, b_re, b_im, c_re, c_im, d_skip,
     w_glu, b_glu, w_att_out, w_ssm_out, w_o, g_pre_ffn, g_post_ffn, w_up, conv_w, conv_b, w_down) = lw
    pre, pim, bbr, bbi = _ssm_prep(lam_re, lam_im, log_dt, b_re, b_im, SCAN_GROUP)
    wb = jnp.concatenate([_block_diag(bbr.reshape(N_GROUPS, GROUP, STATE_DIM)),
                          _block_diag(bbi.reshape(N_GROUPS, GROUP, STATE_DIM))], axis=1).astype(BF16)
    wc = jnp.concatenate([_block_diag(jnp.swapaxes(c_re, 1, 2)),
                          _block_diag(-jnp.swapaxes(c_im, 1, 2))], axis=0).astype(BF16)
    vec = lambda a: a.reshape(1, -1)
    return dict(
        w_in=_pack_w_in(w_in), g_pre_mix=vec(g_pre_mix), g_post_mix=vec(g_post_mix),
        pre=pre, pim=pim, wb=wb, wc=wc, d_skip=vec(d_skip), w_glu=w_glu.astype(BF16), b_glu=vec(b_glu),
        w_att_out=w_att_out.astype(BF16), w_ssm_out=w_ssm_out.astype(BF16), w_o=w_o.astype(BF16),
        g_pre_ffn=vec(g_pre_ffn), g_post_ffn=vec(g_post_ffn), w_up=w_up.astype(BF16), conv_w=conv_w,
        conv_b=vec(conv_b), w_down=w_down.astype(BF16))


def _prompt_layer(x, w, bias_tiles, tq, tm, chunk, tf):
    bsz, t, _ = x.shape
    x2 = x.reshape(bsz * t, D_MODEL)
    k, v, kw, u, sga, sgs, qt, qit, kwt, kh, vt, ki = _in_proj(x2, w["g_pre_mix"], w["w_in"], bsz, t, tm, True)
    y_att = _prompt_attn(qt, qit, kwt, kh, vt, ki, bias_tiles, bsz, t, tq, min(TOPK_MAX, t // 4))
    zero_state = jnp.zeros((bsz, 1, N_STATE), F32)
    y_ssm, s_re, s_im = _ssm_scan(u, zero_state, zero_state, w["pre"], w["pim"], w["wb"], w["wc"], w["d_skip"],
                                  w["w_glu"], w["b_glu"], bsz, t, chunk)
    x1, h2 = _merge(x2, y_att, y_ssm, sga, sgs, w["w_att_out"], w["w_ssm_out"], w["w_o"],
                    w["g_post_mix"], w["g_pre_ffn"], tm)
    zero_conv = jnp.zeros((bsz, CONV_W - 1, 2 * D_FF), F32)
    tm_ffn = min(2 * tm, t)
    y, tail_a, tail_b = _ffn(h2, x1, w["w_up"], w["conv_w"], w["conv_b"], w["w_down"], w["g_post_ffn"],
                             zero_conv, bsz, t, tm_ffn, tf, True)
    state = (jnp.transpose(k, (0, 3, 1, 2)), jnp.transpose(v, (0, 3, 1, 2)),
             jnp.swapaxes(kwt[:, :IDX_DIM, :], 1, 2),
             s_re.reshape(bsz, N_GROUPS, STATE_DIM), s_im.reshape(bsz, N_GROUPS, STATE_DIM),
             jnp.concatenate([tail_a, tail_b], axis=-1)[t // tm_ffn - 1::t // tm_ffn])
    return y.reshape(bsz, t, D_MODEL), state


def _sample_layer(x, w, bias_row, cache_k, cache_v, cache_kidx, page_table, st_re, st_im, st_conv, tf):
    db, tq, _ = x.shape
    assert tq == 1, "the sample group is decoded one token per sequence"
    n_pool = cache_k.shape[0]
    past = page_table.shape[1] * PAGE_SIZE
    x2 = x.reshape(db, D_MODEL)
    k, v, kw, u, sga, sgs, q, qi = _in_proj(x2, w["g_pre_mix"], w["w_in"], 1, db, db, False)
    keys = _sample_scores(page_table, qi.reshape(db, IDX_HEADS, IDX_DIM),
                          kw[:, IDX_DIM:IDX_DIM + IDX_HEADS].reshape(db, IDX_HEADS, 1),
                          kw[:, :IDX_DIM].reshape(db, 1, IDX_DIM), jnp.swapaxes(cache_kidx, 1, 2))
    mask = _sample_select(keys.reshape(db, past + LANES), min(TOPK_MAX, (past + tq) // 4))
    heads = lambda a: a.astype(F32).reshape(db, N_HEADS, HEAD_DIM)
    heads_t = lambda a: jnp.swapaxes(heads(a), 1, 2)
    pages_t = lambda c: jnp.transpose(c, (0, 2, 3, 1))
    y_att_t = _sample_attn(page_table, heads(q), heads_t(q), heads(k), heads_t(v), mask[:, None, :], bias_row,
                           pages_t(cache_k), pages_t(cache_v))
    y_att = jnp.swapaxes(y_att_t, 1, 2).reshape(db, ATT_WIDTH)
    y_ssm, s_re, s_im = _ssm_step(u, st_re.reshape(db, N_STATE), st_im.reshape(db, N_STATE), w["pre"], w["pim"],
                                  w["wb"], w["wc"], w["d_skip"], w["w_glu"], w["b_glu"])
    x1, h2 = _merge(x2, y_att, y_ssm, sga, sgs, w["w_att_out"], w["w_ssm_out"], w["w_o"],
                    w["g_post_mix"], w["g_pre_ffn"], db)
    y, up_a, up_b = _ffn(h2, x1, w["w_up"], w["conv_w"], w["conv_b"], w["w_down"], w["g_post_ffn"],
                         st_conv, db, 1, db, tf, False)
    conv_new = jnp.concatenate([st_conv[:, 1:], jnp.concatenate([up_a, up_b], axis=-1)[:, None, :]], axis=1)
    state = (k.reshape(db, 1, N_HEADS, HEAD_DIM), v.reshape(db, 1, N_HEADS, HEAD_DIM),
             kw[:, :IDX_DIM].reshape(db, 1, IDX_DIM),
             s_re.reshape(db, N_GROUPS, STATE_DIM), s_im.reshape(db, N_GROUPS, STATE_DIM), conv_new)
    return y.reshape(db, 1, D_MODEL), state


def kernel(x_prompt, x_sample, cache_k, cache_v, cache_kidx, state_ssm_re, state_ssm_im, state_conv, page_table,
           rel_bias, w_in, g_pre_mix, g_post_mix, lam_re, lam_im, log_dt, b_re, b_im, c_re, c_im, d_skip, w_glu,
           b_glu, w_att_out, w_ssm_out, w_o, g_pre_ffn, g_post_ffn, w_up, conv_w, conv_b, w_down):
    depth = w_in.shape[0]
    t = x_prompt.shape[1]
    past = page_table.shape[1] * PAGE_SIZE
    tq = min(256, t)
    tm = min(512, t)
    chunk = min(256, t)
    tf = 512
    bias_tiles, bias_row = _bias_tiles(rel_bias, tq, past)
    y_p, y_s = x_prompt, x_sample
    outs_p, outs_s = [], []
    for l in range(depth):
        lw = (w_in[l], g_pre_mix[l], g_post_mix[l], lam_re[l], lam_im[l], log_dt[l], b_re[l], b_im[l],
              c_re[l], c_im[l], d_skip[l], w_glu[l], b_glu[l], w_att_out[l], w_ssm_out[l], w_o[l],
              g_pre_ffn[l], g_post_ffn[l], w_up[l], conv_w[l], conv_b[l], w_down[l])
        w = _layer_weights(lw)
        y_p, st_p = _prompt_layer(y_p, w, bias_tiles, tq, tm, chunk, tf)
        y_s, st_s = _sample_layer(y_s, w, bias_row, cache_k[l], cache_v[l], cache_kidx[l], page_table,
                                  state_ssm_re[l], state_ssm_im[l], state_conv[l], tf)
        outs_p.append(st_p)
        outs_s.append(st_s)
    k_p, v_p, ki_p, sr_p, si_p, cv_p = [jnp.stack(a) for a in zip(*outs_p)]
    k_s, v_s, ki_s, sr_s, si_s, cv_s = [jnp.stack(a) for a in zip(*outs_s)]
    return (y_p, y_s, k_p, v_p, ki_p, sr_p, si_p, cv_p, k_s, v_s, ki_s, sr_s, si_s, cv_s)
```

```python
import functools
import math

import numpy as np
import jax
import jax.numpy as jnp
from jax import lax
from jax.experimental import pallas as pl
from jax.experimental.pallas import tpu as pltpu

F32 = jnp.float32
BF16 = jnp.bfloat16
I32 = jnp.int32

D_MODEL = 1024
PAGE_SIZE = 128
N_HEADS = 8
HEAD_DIM = 64
ATT_WIDTH = N_HEADS * HEAD_DIM
IDX_HEADS = 4
IDX_DIM = 64
TOPK_MAX = 256
N_BUCKETS = 32
MAX_DISTANCE = 128
SSM_WIDTH = 512
GROUP = 16
N_GROUPS = SSM_WIDTH // GROUP
STATE_DIM = 64
N_STATE = N_GROUPS * STATE_DIM
D_FF = 4 * D_MODEL
CONV_W = 3
EPS = 1e-6
SPLITS = (ATT_WIDTH, ATT_WIDTH, ATT_WIDTH, IDX_HEADS * IDX_DIM, IDX_DIM, IDX_HEADS, SSM_WIDTH, D_MODEL, D_MODEL)

LANES = 128
KEY_EXCLUDED = -(2 ** 31)
IDX_BITS = 14
BF16_SUBLANES = 16
V_ROWS = HEAD_DIM + BF16_SUBLANES
LOG2_E = math.log2(math.e)
FFN_PART_ROWS = 256
SUBLANES = 8
SCAN_GROUP = SUBLANES
MASKED = -1e30
VMEM_LIMIT = 56 * 1024 * 1024

_C_Q, _C_K, _C_V = 0, ATT_WIDTH, 2 * ATT_WIDTH
_C_QI = 3 * ATT_WIDTH
_C_KW = _C_QI + IDX_HEADS * IDX_DIM
_C_U = _C_KW + LANES
_C_GA = _C_U + SSM_WIDTH
_C_GS = _C_GA + D_MODEL
_C_END = _C_GS + D_MODEL


def _params(*sem):
    return pltpu.CompilerParams(dimension_semantics=sem, vmem_limit_bytes=VMEM_LIMIT)


def _rms(x, g):
    inv = lax.rsqrt(jnp.mean(x * x, axis=-1, keepdims=True) + EPS)
    return (x * inv) * g


def _dot(a, b):
    return jnp.dot(a, b, preferred_element_type=F32)


def _dot_nt(a, b):
    return lax.dot_general(a, b, (((1,), (1,)), ((), ())), preferred_element_type=F32)


def _bucket_starts():
    n = np.arange(0, 1 << IDX_BITS, dtype=np.int32)
    max_exact = N_BUCKETS // 2
    nf = np.maximum(n, 1).astype(np.float32)
    large = max_exact + (np.log(nf / np.float32(max_exact)) / np.float32(math.log(MAX_DISTANCE / max_exact))
                         * np.float32(N_BUCKETS - max_exact)).astype(np.int32)
    large = np.minimum(large, N_BUCKETS - 1)
    bucket = np.where(n < max_exact, n, large)
    assert np.all(np.diff(bucket) >= 0)
    starts = [int(np.argmax(bucket >= b)) for b in range(N_BUCKETS)]
    assert all(bucket[s] == b for b, s in enumerate(starts))
    return starts


_BUCKET_START = _bucket_starts()


def _bias_tiles_body(rb_ref, tile_ref, row_ref, *, tq, past):
    h = pl.program_id(0)

    def bias_of(dist):
        val = jnp.full(dist.shape, rb_ref[0, h], F32)
        for b in range(1, N_BUCKETS):
            val = jnp.where(dist >= _BUCKET_START[b], rb_ref[b, h], val)
        return val

    key_off = lax.broadcasted_iota(I32, (tq, tq), 0)
    qry_off = lax.broadcasted_iota(I32, (tq, tq), 1)
    far = rb_ref[N_BUCKETS - 1, h]
    tile_ref[0, 1] = (bias_of(qry_off - key_off) - far) * LOG2_E
    tile_ref[0, 0] = (bias_of(qry_off - key_off + tq) - far) * LOG2_E
    lane = lax.broadcasted_iota(I32, (1, past + LANES), 1)
    row_ref[0] = bias_of(past - lane)


def _bias_tiles(rel_bias, tq, past):
    assert tq + 1 >= _BUCKET_START[N_BUCKETS - 1]
    return pl.pallas_call(
        functools.partial(_bias_tiles_body, tq=tq, past=past),
        grid=(N_HEADS,),
        in_specs=[pl.BlockSpec(memory_space=pltpu.SMEM)],
        out_specs=[pl.BlockSpec((1, 2, tq, tq), lambda h: (h, 0, 0, 0)),
                   pl.BlockSpec((1, 1, past + LANES), lambda h: (h, 0, 0))],
        out_shape=[jax.ShapeDtypeStruct((N_HEADS, 2, tq, tq), F32),
                   jax.ShapeDtypeStruct((N_HEADS, 1, past + LANES), F32)],
        compiler_params=_params("arbitrary"),
        name="bias_tiles",
    )(rel_bias)


def _in_proj_body(x_ref, g_ref, w_ref, k_ref, v_ref, kw_ref, u_ref, sga_ref, sgs_ref, *more_refs, seq):
    h = _rms(x_ref[...], g_ref[...]).astype(BF16)

    def proj(c0, c1):
        return _dot(h, w_ref[:, c0:c1])

    q = proj(_C_Q, _C_K) * HEAD_DIM ** -0.5
    k = proj(_C_K, _C_V)
    v = proj(_C_V, _C_QI)
    qi = proj(_C_QI, _C_KW)
    kw = proj(_C_KW, _C_U)
    lane = lax.broadcasted_iota(I32, kw.shape, 1)
    kw = jnp.where(lane >= IDX_DIM, kw * IDX_HEADS ** -0.5, kw)
    kw_ref[...] = kw
    u_ref[...] = proj(_C_U, _C_GA)
    sga_ref[...] = jax.nn.sigmoid(proj(_C_GA, _C_GS))
    sgs_ref[...] = jax.nn.sigmoid(proj(_C_GS, _C_END))
    if seq:
        qt_ref, qit_ref, kwt_ref, kh_ref, vt_ref, ki_ref = more_refs
        tm = k.shape[0]
        vt = v.T.reshape(N_HEADS, HEAD_DIM, tm)
        k_ref[0] = k.T.reshape(N_HEADS, HEAD_DIM, tm)
        v_ref[0] = vt
        qt_ref[...] = (q * LOG2_E).T.reshape(N_HEADS, HEAD_DIM, tm).astype(BF16)
        qit_ref[...] = (qi * IDX_DIM ** -0.5).T.reshape(IDX_HEADS, IDX_DIM, tm).astype(BF16)
        kwt_ref[0] = kw.T
        for hd in range(N_HEADS):
            kh_ref[0, hd] = k[:, hd * HEAD_DIM:(hd + 1) * HEAD_DIM].astype(BF16)
        vt_ref[0, :, :HEAD_DIM, :] = vt.astype(BF16)
        vt_ref[0, :, HEAD_DIM:, :] = jnp.ones((N_HEADS, V_ROWS - HEAD_DIM, tm), BF16)
        ki_ref[0] = kw[:, :IDX_DIM].astype(BF16)
    else:
        q_ref, qi_ref = more_refs
        k_ref[...] = k
        v_ref[...] = v
        q_ref[...] = q.astype(BF16)
        qi_ref[...] = qi


def _in_proj(x2, g, w_packed, bsz, t, tm, seq):
    m = bsz * t
    nt = t // tm
    row = lambda b, i: (b * nt + i, 0)
    widths = (LANES, SSM_WIDTH, D_MODEL, D_MODEL)
    if seq:
        kv_spec = pl.BlockSpec((1, N_HEADS, HEAD_DIM, tm), lambda b, i: (b, 0, 0, i))
        kv_shape = jax.ShapeDtypeStruct((bsz, N_HEADS, HEAD_DIM, t), F32)
    else:
        kv_spec = pl.BlockSpec((tm, ATT_WIDTH), row)
        kv_shape = jax.ShapeDtypeStruct((m, ATT_WIDTH), F32)
    out_specs = [kv_spec, kv_spec] + [pl.BlockSpec((tm, w), row) for w in widths]
    out_shape = [kv_shape, kv_shape] + [jax.ShapeDtypeStruct((m, w), F32) for w in widths]
    if seq:
        col = lambda b, i: (0, 0, b * nt + i)
        out_specs += [pl.BlockSpec((N_HEADS, HEAD_DIM, tm), col),
                      pl.BlockSpec((IDX_HEADS, IDX_DIM, tm), col),
                      pl.BlockSpec((1, LANES, tm), lambda b, i: (b, 0, i)),
                      pl.BlockSpec((1, N_HEADS, tm, HEAD_DIM), lambda b, i: (b, 0, i, 0)),
                      pl.BlockSpec((1, N_HEADS, V_ROWS, tm), lambda b, i: (b, 0, 0, i)),
                      pl.BlockSpec((1, tm, IDX_DIM), lambda b, i: (b, i, 0))]
        out_shape += [jax.ShapeDtypeStruct((N_HEADS, HEAD_DIM, m), BF16),
                      jax.ShapeDtypeStruct((IDX_HEADS, IDX_DIM, m), BF16),
                      jax.ShapeDtypeStruct((bsz, LANES, t), F32),
                      jax.ShapeDtypeStruct((bsz, N_HEADS, t, HEAD_DIM), BF16),
                      jax.ShapeDtypeStruct((bsz, N_HEADS, V_ROWS, t), BF16),
                      jax.ShapeDtypeStruct((bsz, t, IDX_DIM), BF16)]
    else:
        out_specs += [pl.BlockSpec((tm, ATT_WIDTH), row), pl.BlockSpec((tm, IDX_HEADS * IDX_DIM), row)]
        out_shape += [jax.ShapeDtypeStruct((m, ATT_WIDTH), BF16),
                      jax.ShapeDtypeStruct((m, IDX_HEADS * IDX_DIM), F32)]
    return pl.pallas_call(
        functools.partial(_in_proj_body, seq=seq),
        grid=(bsz, nt),
        in_specs=[pl.BlockSpec((tm, D_MODEL), row),
                  pl.BlockSpec((1, D_MODEL), lambda b, i: (0, 0)),
                  pl.BlockSpec((D_MODEL, _C_END), lambda b, i: (0, 0))],
        out_specs=out_specs,
        out_shape=out_shape,
        compiler_params=_params("arbitrary", "arbitrary"),
        name="in_proj",
    )(x2, g, w_packed)


def _fori_by_two(lo, hi, body, init):
    n_pairs = (hi - lo) // 2
    carry = lax.fori_loop(0, n_pairs, lambda t, c: body(lo + 2 * t + 1, body(lo + 2 * t, c)), init)
    return lax.fori_loop(lo + 2 * n_pairs, hi, body, carry)


def _sortable_key(score):
    bits = pltpu.bitcast(score, I32)
    return jnp.where(bits < 0, bits ^ 0x7FFFFFFF, bits)


def _topk_mask(keys_ref, nblk, height, topk):
    n_lanes = keys_ref.shape[1]
    n_parts = height // SUBLANES
    assert n_parts * SUBLANES == height and n_parts & (n_parts - 1) == 0

    def row_block(j):
        return pl.ds(pl.multiple_of(j * height, height), height)

    def count(pred, bound):
        def block_hits(j):
            hit = jnp.where(pred(keys_ref[row_block(j), :], bound), 1, 0)
            parts = [hit[r:r + SUBLANES] for r in range(0, height, SUBLANES)]
            while len(parts) > 1:
                parts = [a + b for a, b in zip(parts[::2], parts[1::2])]
            return parts[0]

        acc = _fori_by_two(0, nblk, lambda j, acc: acc + block_hits(j), jnp.zeros((SUBLANES, n_lanes), I32))
        return jnp.sum(acc, axis=0, keepdims=True)

    zero = jnp.zeros((1, n_lanes), I32)

    prefix = jnp.where(count(lambda k, b: k >= b, zero) >= topk, 0, KEY_EXCLUDED).astype(I32)

    def value_bit(b, prefix):
        cand = prefix | jnp.left_shift(1, 30 - b)
        return jnp.where(count(lambda k, c: k >= c, cand) >= topk, cand, prefix)
    thr = lax.fori_loop(0, 31, value_bit, prefix)

    n_above = count(lambda k, t: k > t, thr)
    need = jnp.where(thr == KEY_EXCLUDED, 0, topk - n_above).astype(F32)
    at_or_before = (lax.broadcasted_iota(I32, (height, height), 0)
                    >= lax.broadcasted_iota(I32, (height, height), 1))
    lower_ones = jnp.where(at_or_before, 1.0, 0.0).astype(BF16)

    def emit(j, seen):
        k = keys_ref[row_block(j), :]
        tie = k == thr
        rank = seen + _dot(lower_ones, jnp.where(tie, 1.0, 0.0).astype(BF16))
        mask = jnp.where(k > thr, 0.0, jnp.where(tie, jnp.where(rank <= need, 0.0, MASKED), MASKED))
        keys_ref[row_block(j), :] = pltpu.bitcast(mask.astype(F32), I32)
        return rank[height - 1:, :]
    lax.fori_loop(0, nblk, emit, jnp.zeros((1, n_lanes), F32))


def _prompt_attn_body(qt_ref, qit_ref, wt_ref, kh_ref, vt_ref, ki_ref, bias_ref, o_ref, keys_ref, m_ref, acc_ref,
                      alpha_ref, p_ref, *, tq, topk):
    i = pl.program_id(1)
    nblk = i + 1
    key_off = lax.broadcasted_iota(I32, (tq, tq), 0)
    qry_off = lax.broadcasted_iota(I32, (tq, tq), 1)

    def key_block(j):
        return pl.ds(pl.multiple_of(j * tq, tq), tq)

    qit = jnp.concatenate([qit_ref[h] for h in range(IDX_HEADS)], axis=1)
    wt = [wt_ref[0, h:h + 1, :] for h in range(IDX_HEADS)]

    def score_block(j, _, diagonal):
        ki = ki_ref[0, key_block(j), :]
        dots = _dot(ki, qit)
        s = None
        for h in range(IDX_HEADS):
            sh = jnp.maximum(dots[:, h * tq:(h + 1) * tq], 0.0) * wt[h]
            s = sh if s is None else s + sh
        key = _sortable_key(s)
        if diagonal:
            key = jnp.where(key_off <= qry_off, key, KEY_EXCLUDED)
        keys_ref[key_block(j), :] = key
        return 0
    _fori_by_two(0, i, functools.partial(score_block, diagonal=False), 0)
    score_block(i, 0, diagonal=True)

    _topk_mask(keys_ref, nblk, tq, topk)

    m_ref[...] = jnp.full(m_ref.shape, MASKED, F32)
    acc_ref[...] = jnp.zeros(acc_ref.shape, F32)
    p_ref[...] = jnp.zeros(p_ref.shape, BF16)
    alpha_ref[...] = jnp.ones(alpha_ref.shape, F32)

    def accumulate(j):
        for h in range(N_HEADS):
            acc_ref[h] = alpha_ref[h] * acc_ref[h] + _dot(vt_ref[0, h, :, key_block(j)], p_ref[h])

    def attend(j, _, near):
        accumulate(jnp.maximum(j - 1, 0))
        masked = pltpu.bitcast(keys_ref[key_block(j), :], F32)
        for h in range(N_HEADS):
            s = _dot(kh_ref[0, h, key_block(j), :], qt_ref[h]) + masked
            if near:
                s = s + bias_ref[h, j - i + 1]
            m_old = m_ref[h]
            m_new = jnp.maximum(m_old, jnp.max(s, axis=0, keepdims=True))
            p_ref[h] = jnp.exp2(s - m_new).astype(BF16)
            alpha_ref[h] = jnp.exp2(m_old - m_new)
            m_ref[h] = m_new
        return 0

    n_far = jnp.maximum(i - 1, 0)
    _fori_by_two(0, n_far, functools.partial(attend, near=False), 0)
    lax.fori_loop(n_far, nblk, functools.partial(attend, near=True), 0)
    accumulate(i)
    out_t = jnp.concatenate(
        [acc_ref[h, :HEAD_DIM, :] / acc_ref[h, HEAD_DIM:HEAD_DIM + 1, :] for h in range(N_HEADS)], axis=0)
    o_ref[...] = out_t.T


def _prompt_attn(qt, qit, kwt, kh, vt, ki, bias_tiles, bsz, t, tq, topk):
    nq = t // tq
    col = lambda b, i: (0, 0, b * nq + i)
    whole = dict(pipeline_mode=pl.Buffered(1))
    return pl.pallas_call(
        functools.partial(_prompt_attn_body, tq=tq, topk=topk),
        grid=(bsz, nq),
        in_specs=[pl.BlockSpec((N_HEADS, HEAD_DIM, tq), col),
                  pl.BlockSpec((IDX_HEADS, IDX_DIM, tq), col),
                  pl.BlockSpec((1, SUBLANES, tq), lambda b, i: (b, IDX_DIM // SUBLANES, i)),
                  pl.BlockSpec((1, N_HEADS, t, HEAD_DIM), lambda b, i: (b, 0, 0, 0), **whole),
                  pl.BlockSpec((1, N_HEADS, V_ROWS, t), lambda b, i: (b, 0, 0, 0), **whole),
                  pl.BlockSpec((1, t, IDX_DIM), lambda b, i: (b, 0, 0), **whole),
                  pl.BlockSpec((N_HEADS, 2, tq, tq), lambda b, i: (0, 0, 0, 0), **whole)],
        out_specs=pl.BlockSpec((tq, ATT_WIDTH), lambda b, i: (b * nq + i, 0)),
        out_shape=jax.ShapeDtypeStruct((bsz * t, ATT_WIDTH), F32),
        scratch_shapes=[pltpu.VMEM((t, tq), I32),
                        pltpu.VMEM((N_HEADS, 1, tq), F32), pltpu.VMEM((N_HEADS, V_ROWS, tq), F32),
                        pltpu.VMEM((N_HEADS, 1, tq), F32), pltpu.VMEM((N_HEADS, tq, tq), BF16)],
        compiler_params=_params("arbitrary", "arbitrary"),
        name="prompt_attn",
    )(qt, qit, kwt, kh, vt, ki, bias_tiles)


def _sample_scores_body(pt_ref, qi_ref, w_ref, kn_ref, *rest, n_pages):
    page_refs, key_ref = rest[:n_pages], rest[n_pages]
    qi = qi_ref[...].astype(BF16)
    w = w_ref[...]

    def weighted(s):
        return jnp.sum(jnp.maximum(s * IDX_DIM ** -0.5, 0.0) * w, axis=0, keepdims=True)

    for p in range(n_pages):
        s = _dot(qi, page_refs[p][...].astype(BF16))
        key_ref[:, p * PAGE_SIZE:(p + 1) * PAGE_SIZE] = _sortable_key(weighted(s))
    s_self = jnp.sum(qi.astype(F32) * kn_ref[...].astype(BF16).astype(F32), axis=1, keepdims=True)
    lane = lax.broadcasted_iota(I32, (1, LANES), 1)
    key_ref[:, n_pages * PAGE_SIZE:] = jnp.where(lane == 0, _sortable_key(weighted(s_self)), KEY_EXCLUDED)


def _sample_scores(page_table, qi3, w3, kn3, cache_kidx):
    db, n_pages = page_table.shape
    n_cols = n_pages * PAGE_SIZE + LANES
    per = lambda s, pt: (s, 0, 0)
    page_specs = [pl.BlockSpec((None, IDX_DIM, PAGE_SIZE), functools.partial(lambda s, pt, p: (pt[s, p], 0, 0), p=p))
                  for p in range(n_pages)]
    return pl.pallas_call(
        functools.partial(_sample_scores_body, n_pages=n_pages),
        grid_spec=pltpu.PrefetchScalarGridSpec(
            num_scalar_prefetch=1, grid=(db,),
            in_specs=[pl.BlockSpec((None, IDX_HEADS, IDX_DIM), per),
                      pl.BlockSpec((None, IDX_HEADS, 1), per),
                      pl.BlockSpec((None, 1, IDX_DIM), per)] + page_specs,
            out_specs=pl.BlockSpec((None, 1, n_cols), per)),
        out_shape=jax.ShapeDtypeStruct((db, 1, n_cols), I32),
        compiler_params=_params("arbitrary"),
        name="sample_scores",
    )(page_table, qi3, w3, kn3, *([cache_kidx] * n_pages))


def _sample_select_body(key_ref, mask_ref, keys_scr, *, n_cols, topk):
    keys_scr[...] = key_ref[...].T
    _topk_mask(keys_scr, n_cols // LANES, LANES, topk)
    mask_ref[...] = pltpu.bitcast(keys_scr[...], F32).T


def _sample_select(keys, topk):
    n_rows, n_cols = keys.shape
    return pl.pallas_call(
        functools.partial(_sample_select_body, n_cols=n_cols, topk=topk),
        out_shape=jax.ShapeDtypeStruct((n_rows, n_cols), F32),
        scratch_shapes=[pltpu.VMEM((n_cols, n_rows), I32)],
        compiler_params=pltpu.CompilerParams(vmem_limit_bytes=VMEM_LIMIT),
        name="sample_select",
    )(keys)


def _sample_attn_body(pt_ref, q_ref, qt_ref, kn_ref, vnt_ref, mask_ref, bias_ref, *rest, n_pages):
    kt_refs, vt_refs, o_ref = rest[:n_pages], rest[n_pages:2 * n_pages], rest[2 * n_pages]
    past = n_pages * PAGE_SIZE
    mask = mask_ref[...]
    s_self_all = jnp.sum(q_ref[...] * kn_ref[...], axis=1, keepdims=True)
    for h in range(N_HEADS):
        q_col = qt_ref[:, h:h + 1]
        bias = bias_ref[h]
        s = jnp.concatenate([jnp.sum(kt_refs[p][h] * q_col, axis=0, keepdims=True) for p in range(n_pages)], axis=1)
        s = s + bias[:, :past] + mask[:, :past]
        s_self = s_self_all[h:h + 1, :] + bias[:, past:past + 1] + mask[:, past:past + 1]
        m = jnp.maximum(jnp.max(s, axis=1, keepdims=True), s_self)
        p_past = jnp.exp(s - m)
        p_self = jnp.exp(s_self - m)
        l = jnp.sum(p_past, axis=1, keepdims=True) + p_self
        acc = vt_refs[0][h] * p_past[:, :PAGE_SIZE]
        for p in range(1, n_pages):
            acc = acc + vt_refs[p][h] * p_past[:, p * PAGE_SIZE:(p + 1) * PAGE_SIZE]
        out = jnp.sum(acc, axis=1, keepdims=True) + p_self * vnt_ref[:, h:h + 1]
        o_ref[:, h:h + 1] = out / l


def _sample_attn(page_table, q3, qt3, kn3, vnt3, mask3, bias_row, cache_kt, cache_vt):
    db, n_pages = page_table.shape
    n_cols = n_pages * PAGE_SIZE + LANES
    per = lambda s, pt: (s, 0, 0)
    page_specs = [pl.BlockSpec((None, N_HEADS, HEAD_DIM, PAGE_SIZE),
                               functools.partial(lambda s, pt, p: (pt[s, p], 0, 0, 0), p=p)) for p in range(n_pages)]
    return pl.pallas_call(
        functools.partial(_sample_attn_body, n_pages=n_pages),
        grid_spec=pltpu.PrefetchScalarGridSpec(
            num_scalar_prefetch=1, grid=(db,),
            in_specs=[pl.BlockSpec((None, N_HEADS, HEAD_DIM), per),
                      pl.BlockSpec((None, HEAD_DIM, N_HEADS), per),
                      pl.BlockSpec((None, N_HEADS, HEAD_DIM), per),
                      pl.BlockSpec((None, HEAD_DIM, N_HEADS), per),
                      pl.BlockSpec((None, 1, n_cols), per),
                      pl.BlockSpec((N_HEADS, 1, n_cols), lambda s, pt: (0, 0, 0))] + page_specs + page_specs,
            out_specs=pl.BlockSpec((None, HEAD_DIM, N_HEADS), per)),
        out_shape=jax.ShapeDtypeStruct((db, HEAD_DIM, N_HEADS), F32),
        compiler_params=_params("arbitrary"),
        name="sample_attn",
    )(page_table, q3, qt3, kn3, vnt3, mask3, bias_row, *([cache_kt] * n_pages), *([cache_vt] * n_pages))


def _cmul(ar, ai, br, bi):
    return ar * br - ai * bi, ar * bi + ai * br


def _ssm_prep_body(lre_f, lim_f, ldt_f, lre_r, lim_r, ldt_r, bre_ref, bim_ref, pre_ref, pim_ref, bbr_ref, bbi_ref,
                   *, chunk):
    def lam_bar(lre, lim, ldt):
        dt = jnp.exp(ldt)
        mag = jnp.exp(lre * dt)
        return mag * jnp.cos(lim * dt), mag * jnp.sin(lim * dt)

    lbr, lbi = lam_bar(lre_f[...], lim_f[...], ldt_f[...])
    pr = jnp.broadcast_to(lbr, (chunk, N_STATE))
    pi = jnp.broadcast_to(lbi, (chunk, N_STATE))
    row = lax.broadcasted_iota(I32, (chunk, N_STATE), 0)
    d = 1
    while d < chunk:
        sr = jnp.where(row >= d, pltpu.roll(pr, d, 0), 1.0)
        si = jnp.where(row >= d, pltpu.roll(pi, d, 0), 0.0)
        pr, pi = _cmul(pr, pi, sr, si)
        d *= 2
    pre_ref[...] = pr
    pim_ref[...] = pi

    lre, lim = lre_r[...], lim_r[...]
    lbr, lbi = lam_bar(lre, lim, ldt_r[...])
    den = lre * lre + lim * lim
    nr, ni = lbr - 1.0, lbi
    cr = (nr * lre + ni * lim) / den
    ci = (ni * lre - nr * lim) / den
    bbr, bbi = _cmul(cr, ci, bre_ref[...], bim_ref[...])
    bbr_ref[...] = bbr
    bbi_ref[...] = bbi


def _ssm_prep(lam_re, lam_im, log_dt, b_re, b_im, chunk):
    flat = lambda a: a.reshape(1, N_STATE)
    rep = lambda a: jnp.repeat(a, GROUP, axis=0)
    ldt = jnp.broadcast_to(log_dt[:, None], (N_GROUPS, STATE_DIM))
    bt = lambda b: jnp.swapaxes(b, 1, 2).reshape(SSM_WIDTH, STATE_DIM)
    return pl.pallas_call(
        functools.partial(_ssm_prep_body, chunk=chunk),
        out_shape=[jax.ShapeDtypeStruct((chunk, N_STATE), F32), jax.ShapeDtypeStruct((chunk, N_STATE), F32),
                   jax.ShapeDtypeStruct((SSM_WIDTH, STATE_DIM), F32), jax.ShapeDtypeStruct((SSM_WIDTH, STATE_DIM), F32)],
        compiler_params=pltpu.CompilerParams(vmem_limit_bytes=VMEM_LIMIT),
        name="ssm_prep",
    )(flat(lam_re), flat(lam_im), flat(ldt), rep(lam_re), rep(lam_im), rep(ldt), bt(b_re), bt(b_im))


def _block_diag(blocks):
    g, r, c = blocks.shape
    eye = jnp.eye(g, dtype=blocks.dtype)
    return (blocks[:, :, None, :] * eye[:, None, :, None]).reshape(g * r, g * c)


def _ssm_readout(u, xr, xi, wc_ref, dskip_ref, wglu_ref, bglu_ref):
    y = _dot(jnp.concatenate([xr, xi], axis=1).astype(BF16), wc_ref[...]) + dskip_ref[...] * u
    g = jax.nn.gelu(y)
    return g * jax.nn.sigmoid(_dot(g.astype(BF16), wglu_ref[...]) + bglu_ref[...])


def _ssm_scan_body(u_ref, x0r_ref, x0i_ref, pre_ref, pim_ref, wb_ref, wc_ref, dskip_ref, wglu_ref, bglu_ref,
                   y_ref, sr_ref, si_ref, cr_ref, ci_ref, xr_ref, xi_ref, *, chunk):
    c = pl.program_id(1)

    @pl.when(c == 0)
    def _():
        cr_ref[...] = x0r_ref[0]
        ci_ref[...] = x0i_ref[0]

    u = u_ref[...]
    bu = _dot(u.astype(BF16), wb_ref[...])
    n_groups = chunk // SCAN_GROUP
    xr = bu[:, :N_STATE].reshape(n_groups, SCAN_GROUP, N_STATE)
    xi = bu[:, N_STATE:].reshape(n_groups, SCAN_GROUP, N_STATE)
    pos = lax.broadcasted_iota(I32, (SCAN_GROUP, N_STATE), 0)
    d = 1
    while d < SCAN_GROUP:
        ar = jnp.where(pos >= d, pre_ref[d - 1:d, :], 0.0)
        ai = jnp.where(pos >= d, pim_ref[d - 1:d, :], 0.0)
        tr, ti = _cmul(ar, ai, pltpu.roll(xr, d, 1), pltpu.roll(xi, d, 1))
        xr, xi = xr + tr, xi + ti
        d *= 2
    pr, pi = pre_ref[...], pim_ref[...]
    cr, ci = cr_ref[...], ci_ref[...]
    for g in range(n_groups):
        tr, ti = _cmul(pr, pi, cr, ci)
        gr, gi = xr[g] + tr, xi[g] + ti
        xr_ref[g * SCAN_GROUP:(g + 1) * SCAN_GROUP, :] = gr
        xi_ref[g * SCAN_GROUP:(g + 1) * SCAN_GROUP, :] = gi
        cr, ci = gr[SCAN_GROUP - 1:, :], gi[SCAN_GROUP - 1:, :]
    cr_ref[...] = cr
    ci_ref[...] = ci
    sr_ref[0] = cr
    si_ref[0] = ci
    y_ref[...] = _ssm_readout(u, xr_ref[...], xi_ref[...], wc_ref, dskip_ref, wglu_ref, bglu_ref)


def _ssm_scan(u, x0r, x0i, pre, pim, wb, wc, dskip, wglu, bglu, bsz, t, chunk):
    nc = t // chunk
    row = lambda b, c: (b * nc + c, 0)
    const = lambda b, c: (0, 0)
    state = pl.BlockSpec((1, 1, N_STATE), lambda b, c: (b, 0, 0))
    return pl.pallas_call(
        functools.partial(_ssm_scan_body, chunk=chunk),
        grid=(bsz, nc),
        in_specs=[pl.BlockSpec((chunk, SSM_WIDTH), row), state, state,
                  pl.BlockSpec((SCAN_GROUP, N_STATE), const), pl.BlockSpec((SCAN_GROUP, N_STATE), const),
                  pl.BlockSpec((SSM_WIDTH, 2 * N_STATE), const), pl.BlockSpec((2 * N_STATE, SSM_WIDTH), const),
                  pl.BlockSpec((1, SSM_WIDTH), const), pl.BlockSpec((SSM_WIDTH, SSM_WIDTH), const),
                  pl.BlockSpec((1, SSM_WIDTH), const)],
        out_specs=[pl.BlockSpec((chunk, SSM_WIDTH), row), state, state],
        out_shape=[jax.ShapeDtypeStruct((bsz * t, SSM_WIDTH), F32),
                   jax.ShapeDtypeStruct((bsz, 1, N_STATE), F32), jax.ShapeDtypeStruct((bsz, 1, N_STATE), F32)],
        scratch_shapes=[pltpu.VMEM((1, N_STATE), F32), pltpu.VMEM((1, N_STATE), F32),
                        pltpu.VMEM((chunk, N_STATE), F32), pltpu.VMEM((chunk, N_STATE), F32)],
        compiler_params=_params("arbitrary", "arbitrary"),
        name="ssm_scan",
    )(u, x0r, x0i, pre, pim, wb, wc, dskip, wglu, bglu)


def _ssm_step_body(u_ref, x0r_ref, x0i_ref, pre_ref, pim_ref, wb_ref, wc_ref, dskip_ref, wglu_ref, bglu_ref,
                   y_ref, sr_ref, si_ref):
    u = u_ref[...]
    bu = _dot(u.astype(BF16), wb_ref[...])
    tr, ti = _cmul(pre_ref[0:1, :], pim_ref[0:1, :], x0r_ref[...], x0i_ref[...])
    xr, xi = bu[:, :N_STATE] + tr, bu[:, N_STATE:] + ti
    sr_ref[...] = xr
    si_ref[...] = xi
    y_ref[...] = _ssm_readout(u, xr, xi, wc_ref, dskip_ref, wglu_ref, bglu_ref)


def _ssm_step(u, x0r, x0i, pre, pim, wb, wc, dskip, wglu, bglu):
    n = u.shape[0]
    return pl.pallas_call(
        _ssm_step_body,
        out_shape=[jax.ShapeDtypeStruct((n, SSM_WIDTH), F32),
                   jax.ShapeDtypeStruct((n, N_STATE), F32), jax.ShapeDtypeStruct((n, N_STATE), F32)],
        compiler_params=pltpu.CompilerParams(vmem_limit_bytes=VMEM_LIMIT),
        name="ssm_step",
    )(u, x0r, x0i, pre, pim, wb, wc, dskip, wglu, bglu)


def _merge_body(x_ref, ya_ref, ys_ref, sga_ref, sgs_ref, wa_ref, ws_ref, wo_ref, gpost_ref, gpre_ref,
                x1_ref, h2_ref):
    merged = (sga_ref[...] * _dot(ya_ref[...].astype(BF16), wa_ref[...])
              + sgs_ref[...] * _dot(ys_ref[...].astype(BF16), ws_ref[...]))
    x1 = x_ref[...] + _rms(_dot(merged.astype(BF16), wo_ref[...]), gpost_ref[...])
    x1_ref[...] = x1
    h2_ref[...] = _rms(x1, gpre_ref[...]).astype(BF16)


def _merge(x2, ya, ys, sga, sgs, wa, ws, wo, gpost, gpre, tm):
    m = x2.shape[0]
    row = lambda i: (i, 0)
    const = lambda i: (0, 0)
    return pl.pallas_call(
        _merge_body,
        grid=(m // tm,),
        in_specs=[pl.BlockSpec((tm, D_MODEL), row), pl.BlockSpec((tm, ATT_WIDTH), row),
                  pl.BlockSpec((tm, SSM_WIDTH), row), pl.BlockSpec((tm, D_MODEL), row),
                  pl.BlockSpec((tm, D_MODEL), row),
                  pl.BlockSpec((ATT_WIDTH, D_MODEL), const), pl.BlockSpec((SSM_WIDTH, D_MODEL), const),
                  pl.BlockSpec((D_MODEL, D_MODEL), const), pl.BlockSpec((1, D_MODEL), const),
                  pl.BlockSpec((1, D_MODEL), const)],
        out_specs=[pl.BlockSpec((tm, D_MODEL), row), pl.BlockSpec((tm, D_MODEL), row)],
        out_shape=[jax.ShapeDtypeStruct((m, D_MODEL), F32), jax.ShapeDtypeStruct((m, D_MODEL), BF16)],
        compiler_params=_params("arbitrary"),
        name="merge",
    )(x2, ya, ys, sga, sgs, wa, ws, wo, gpost, gpre)


def _ffn_body(h_ref, x_ref, wua_ref, wub_ref, cwa_ref, cwb_ref, cba_ref, cbb_ref, wd_ref, g_ref, pa_ref, pb_ref,
              y_ref, oa_ref, ob_ref, f_ref, *carry, seq, tiles_per_seq):
    i, j = pl.program_id(0), pl.program_id(1)
    nj = pl.num_programs(1)
    tm = h_ref.shape[0]
    half = min(FFN_PART_ROWS, tm // 2)
    halves = [slice(r, r + half) for r in range(0, tm, half)]
    ca, cb = carry if seq else (None, None)

    @pl.when(j == 0)
    def _():
        f_ref[...] = jnp.zeros(f_ref.shape, F32)

    if seq:
        @pl.when(i % tiles_per_seq == 0)
        def _():
            ca[j, 0:CONV_W - 1, :] = pa_ref[0]
            cb[j, 0:CONV_W - 1, :] = pb_ref[0]

    ups = [(_dot(h_ref[rows, :], wua_ref[...]), _dot(h_ref[rows, :], wub_ref[...])) for rows in halves]

    def conv(up, before, rows, cw_ref, cb_ref, prev_ref):
        if seq:
            row = lax.broadcasted_iota(I32, up.shape, 0)
            m1 = jnp.where(row == 0, before[1:2], pltpu.roll(up, 1, 0))
            m2 = jnp.where(row == 0, before[0:1], jnp.where(row == 1, before[1:2], pltpu.roll(up, 2, 0)))
        else:
            m2, m1 = prev_ref[rows, 0, :], prev_ref[rows, 1, :]
        return cb_ref[...] + m2 * cw_ref[0:1, :] + m1 * cw_ref[1:2, :] + up * cw_ref[2:3, :]

    before_a = ca[j, 0:CONV_W - 1, :] if seq else None
    before_b = cb[j, 0:CONV_W - 1, :] if seq else None
    for rows, (up_a, up_b) in zip(halves, ups):
        a = conv(up_a, before_a, rows, cwa_ref, cba_ref, pa_ref)
        b = conv(up_b, before_b, rows, cwb_ref, cbb_ref, pb_ref)
        f_ref[rows, :] += _dot((jax.nn.gelu(a) * b).astype(BF16), wd_ref[...])
        if seq:
            before_a, before_b = up_a[half - (CONV_W - 1):, :], up_b[half - (CONV_W - 1):, :]
        else:
            oa_ref[rows, :] = up_a
            ob_ref[rows, :] = up_b
    if seq:
        ca[j, 0:CONV_W - 1, :] = before_a
        cb[j, 0:CONV_W - 1, :] = before_b
        oa_ref[0] = before_a
        ob_ref[0] = before_b

    @pl.when(j == nj - 1)
    def _():
        y_ref[...] = x_ref[...] + _rms(f_ref[...], g_ref[...])


def _ffn(h2, x1, w_up, conv_w, conv_b, w_down, g_post, conv_prev, bsz, t, tm, tf, seq):
    m = bsz * t
    nj = D_FF // tf
    row = lambda i, j: (i, 0)
    if seq:
        tiles_per_seq = t // tm
        prev_a = pl.BlockSpec((1, CONV_W - 1, tf), lambda i, j: (i // tiles_per_seq, 0, j))
        prev_b = pl.BlockSpec((1, CONV_W - 1, tf), lambda i, j: (i // tiles_per_seq, 0, j + nj))
        out_tail = pl.BlockSpec((1, CONV_W - 1, tf), lambda i, j: (i, 0, j))
        tail_shape = jax.ShapeDtypeStruct((m // tm, CONV_W - 1, D_FF), F32)
        scratch = [pltpu.VMEM((nj, 8, tf), F32), pltpu.VMEM((nj, 8, tf), F32)]
    else:
        tiles_per_seq = 1
        prev_a = pl.BlockSpec((tm, CONV_W - 1, tf), lambda i, j: (i, 0, j))
        prev_b = pl.BlockSpec((tm, CONV_W - 1, tf), lambda i, j: (i, 0, j + nj))
        out_tail = pl.BlockSpec((tm, tf), lambda i, j: (i, j))
        tail_shape = jax.ShapeDtypeStruct((m, D_FF), F32)
        scratch = []
    return pl.pallas_call(
        functools.partial(_ffn_body, seq=seq, tiles_per_seq=tiles_per_seq),
        grid=(m // tm, nj),
        in_specs=[pl.BlockSpec((tm, D_MODEL), row), pl.BlockSpec((tm, D_MODEL), row),
                  pl.BlockSpec((D_MODEL, tf), lambda i, j: (0, j)),
                  pl.BlockSpec((D_MODEL, tf), lambda i, j: (0, j + nj)),
                  pl.BlockSpec((CONV_W, tf), lambda i, j: (0, j)),
                  pl.BlockSpec((CONV_W, tf), lambda i, j: (0, j + nj)),
                  pl.BlockSpec((1, tf), lambda i, j: (0, j)),
                  pl.BlockSpec((1, tf), lambda i, j: (0, j + nj)),
                  pl.BlockSpec((tf, D_MODEL), lambda i, j: (j, 0)),
                  pl.BlockSpec((1, D_MODEL), lambda i, j: (0, 0)),
                  prev_a, prev_b],
        out_specs=[pl.BlockSpec((tm, D_MODEL), row), out_tail, out_tail],
        out_shape=[jax.ShapeDtypeStruct((m, D_MODEL), F32), tail_shape, tail_shape],
        scratch_shapes=[pltpu.VMEM((tm, D_MODEL), F32)] + scratch,
        compiler_params=_params("arbitrary", "arbitrary"),
        name="conv_ffn",
    )(h2, x1, w_up, w_up, conv_w, conv_w, conv_b, conv_b, w_down, g_post, conv_prev, conv_prev)


def _pack_w_in(w_in):
    points = np.cumsum(SPLITS)[:-1].tolist()
    wq, wk, wv, wqi, wki, wwi, wu, wga, wgs = jnp.split(w_in, points, axis=-1)
    pad = jnp.zeros((D_MODEL, LANES - IDX_DIM - IDX_HEADS), w_in.dtype)
    return jnp.concatenate([wq, wk, wv, wqi, wki, wwi, pad, wu, wga, wgs], axis=-1).astype(BF16)


def _layer_weights(lw):
    (w_in, g_pre_mix, g_post_mix, lam_re, lam_im, log_dt, b_re, b_im, c_re, c_im, d_skip,
     w_glu, b_glu, w_att_out, w_ssm_out, w_o, g_pre_ffn, g_post_ffn, w_up, conv_w, conv_b, w_down) = lw
    pre, pim, bbr, bbi = _ssm_prep(lam_re, lam_im, log_dt, b_re, b_im, SCAN_GROUP)
    wb = jnp.concatenate([_block_diag(bbr.reshape(N_GROUPS, GROUP, STATE_DIM)),
                          _block_diag(bbi.reshape(N_GROUPS, GROUP, STATE_DIM))], axis=1).astype(BF16)
    wc = jnp.concatenate([_block_diag(jnp.swapaxes(c_re, 1, 2)),
                          _block_diag(-jnp.swapaxes(c_im, 1, 2))], axis=0).astype(BF16)
    vec = lambda a: a.reshape(1, -1)
    return dict(
        w_in=_pack_w_in(w_in), g_pre_mix=vec(g_pre_mix), g_post_mix=vec(g_post_mix),
        pre=pre, pim=pim, wb=wb, wc=wc, d_skip=vec(d_skip), w_glu=w_glu.astype(BF16), b_glu=vec(b_glu),
        w_att_out=w_att_out.astype(BF16), w_ssm_out=w_ssm_out.astype(BF16), w_o=w_o.astype(BF16),
        g_pre_ffn=vec(g_pre_ffn), g_post_ffn=vec(g_post_ffn), w_up=w_up.astype(BF16), conv_w=conv_w,
        conv_b=vec(conv_b), w_down=w_down.astype(BF16))


def _prompt_layer(x, w, bias_tiles, tq, tm, chunk, tf):
    bsz, t, _ = x.shape
    x2 = x.reshape(bsz * t, D_MODEL)
    k, v, kw, u, sga, sgs, qt, qit, kwt, kh, vt, ki = _in_proj(x2, w["g_pre_mix"], w["w_in"], bsz, t, tm, True)
    y_att = _prompt_attn(qt, qit, kwt, kh, vt, ki, bias_tiles, bsz, t, tq, min(TOPK_MAX, t // 4))
    zero_state = jnp.zeros((bsz, 1, N_STATE), F32)
    y_ssm, s_re, s_im = _ssm_scan(u, zero_state, zero_state, w["pre"], w["pim"], w["wb"], w["wc"], w["d_skip"],
                                  w["w_glu"], w["b_glu"], bsz, t, chunk)
    x1, h2 = _merge(x2, y_att, y_ssm, sga, sgs, w["w_att_out"], w["w_ssm_out"], w["w_o"],
                    w["g_post_mix"], w["g_pre_ffn"], tm)
    zero_conv = jnp.zeros((bsz, CONV_W - 1, 2 * D_FF), F32)
    tm_ffn = min(2 * tm, t)
    y, tail_a, tail_b = _ffn(h2, x1, w["w_up"], w["conv_w"], w["conv_b"], w["w_down"], w["g_post_ffn"],
                             zero_conv, bsz, t, tm_ffn, tf, True)
    state = (jnp.transpose(k, (0, 3, 1, 2)), jnp.transpose(v, (0, 3, 1, 2)),
             jnp.swapaxes(kwt[:, :IDX_DIM, :], 1, 2),
             s_re.reshape(bsz, N_GROUPS, STATE_DIM), s_im.reshape(bsz, N_GROUPS, STATE_DIM),
             jnp.concatenate([tail_a, tail_b], axis=-1)[t // tm_ffn - 1::t // tm_ffn])
    return y.reshape(bsz, t, D_MODEL), state


def _sample_layer(x, w, bias_row, cache_k, cache_v, cache_kidx, page_table, st_re, st_im, st_conv, tf):
    db, tq, _ = x.shape
    assert tq == 1, "the sample group is decoded one token per sequence"
    n_pool = cache_k.shape[0]
    past = page_table.shape[1] * PAGE_SIZE
    x2 = x.reshape(db, D_MODEL)
    k, v, kw, u, sga, sgs, q, qi = _in_proj(x2, w["g_pre_mix"], w["w_in"], 1, db, db, False)
    keys = _sample_scores(page_table, qi.reshape(db, IDX_HEADS, IDX_DIM),
                          kw[:, IDX_DIM:IDX_DIM + IDX_HEADS].reshape(db, IDX_HEADS, 1),
                          kw[:, :IDX_DIM].reshape(db, 1, IDX_DIM), jnp.swapaxes(cache_kidx, 1, 2))
    mask = _sample_select(keys.reshape(db, past + LANES), min(TOPK_MAX, (past + tq) // 4))
    heads = lambda a: a.astype(F32).reshape(db, N_HEADS, HEAD_DIM)
    heads_t = lambda a: jnp.swapaxes(heads(a), 1, 2)
    pages_t = lambda c: jnp.transpose(c, (0, 2, 3, 1))
    y_att_t = _sample_attn(page_table, heads(q), heads_t(q), heads(k), heads_t(v), mask[:, None, :], bias_row,
                           pages_t(cache_k), pages_t(cache_v))
    y_att = jnp.swapaxes(y_att_t, 1, 2).reshape(db, ATT_WIDTH)
    y_ssm, s_re, s_im = _ssm_step(u, st_re.reshape(db, N_STATE), st_im.reshape(db, N_STATE), w["pre"], w["pim"],
                                  w["wb"], w["wc"], w["d_skip"], w["w_glu"], w["b_glu"])
    x1, h2 = _merge(x2, y_att, y_ssm, sga, sgs, w["w_att_out"], w["w_ssm_out"], w["w_o"],
                    w["g_post_mix"], w["g_pre_ffn"], db)
    y, up_a, up_b = _ffn(h2, x1, w["w_up"], w["conv_w"], w["conv_b"], w["w_down"], w["g_post_ffn"],
                         st_conv, db, 1, db, tf, False)
    conv_new = jnp.concatenate([st_conv[:, 1:], jnp.concatenate([up_a, up_b], axis=-1)[:, None, :]], axis=1)
    state = (k.reshape(db, 1, N_HEADS, HEAD_DIM), v.reshape(db, 1, N_HEADS, HEAD_DIM),
             kw[:, :IDX_DIM].reshape(db, 1, IDX_DIM),
             s_re.reshape(db, N_GROUPS, STATE_DIM), s_im.reshape(db, N_GROUPS, STATE_DIM), conv_new)
    return y.reshape(db, 1, D_MODEL), state


def kernel(x_prompt, x_sample, cache_k, cache_v, cache_kidx, state_ssm_re, state_ssm_im, state_conv, page_table,
           rel_bias, w_in, g_pre_mix, g_post_mix, lam_re, lam_im, log_dt, b_re, b_im, c_re, c_im, d_skip, w_glu,
           b_glu, w_att_out, w_ssm_out, w_o, g_pre_ffn, g_post_ffn, w_up, conv_w, conv_b, w_down):
    depth = w_in.shape[0]
    t = x_prompt.shape[1]
    past = page_table.shape[1] * PAGE_SIZE
    tq = min(256, t)
    tm = min(512, t)
    chunk = min(256, t)
    tf = 512
    bias_tiles, bias_row = _bias_tiles(rel_bias, tq, past)
    y_p, y_s = x_prompt, x_sample
    outs_p, outs_s = [], []
    for l in range(depth):
        lw = (w_in[l], g_pre_mix[l], g_post_mix[l], lam_re[l], lam_im[l], log_dt[l], b_re[l], b_im[l],
              c_re[l], c_im[l], d_skip[l], w_glu[l], b_glu[l], w_att_out[l], w_ssm_out[l], w_o[l],
              g_pre_ffn[l], g_post_ffn[l], w_up[l], conv_w[l], conv_b[l], w_down[l])
        w = _layer_weights(lw)
        y_p, st_p = _prompt_layer(y_p, w, bias_tiles, tq, tm, chunk, tf)
        y_s, st_s = _sample_layer(y_s, w, bias_row, cache_k[l], cache_v[l], cache_kidx[l], page_table,
                                  state_ssm_re[l], state_ssm_im[l], state_conv[l], tf)
        outs_p.append(st_p)
        outs_s.append(st_s)
    k_p, v_p, ki_p, sr_p, si_p, cv_p = [jnp.stack(a) for a in zip(*outs_p)]
    k_s, v_s, ki_s, sr_s, si_s, cv_s = [jnp.stack(a) for a in zip(*outs_s)]
    return (y_p, y_s, k_p, v_p, ki_p, sr_p, si_p, cv_p, k_s, v_s, ki_s, sr_s, si_s, cv_s)
```
